```python
import math
import jax, jax.numpy as jnp
from jax import lax
import numpy as np

D_MODEL = 2048
BATCH = 4
SEQ = 2048
DEPTH = 4
DEC_BATCH = 128
DEC_SEQ = 1
PAST_LEN = 16384
PAGE_SIZE = 128

N_MIXERS = 2
EXPAND = 2
D_INNER = EXPAND * D_MODEL
CONV_WIDTH = 3
GROUP_SIZE = 16
N_GROUPS = D_INNER // GROUP_SIZE
STATE_DIM = 64
SCAN_CHUNK = 128
N_CONV_LAYERS = (DEPTH + 1) // N_MIXERS
N_SSM_LAYERS = DEPTH // N_MIXERS
EPS = 1e-6

kernel_name = "hybrid_shortconv_s5_decode_step"


def rmsnorm(x, g):
    xf = x.astype(jnp.float32)
    xf = xf * lax.rsqrt(jnp.mean(xf * xf, axis=-1, keepdims=True) + EPS)
    return (xf * g.astype(jnp.float32)).astype(x.dtype)


def short_conv_mixer(h, state, w_in, conv_w, w_out):
    L = h.shape[1]
    proj = h @ w_in
    b_gate, c_gate, v, z = jnp.split(proj, 4, axis=-1)
    u = c_gate * v
    padded = jnp.concatenate([state.astype(u.dtype), u], axis=1)
    conv = sum(conv_w[k] * padded[:, k:k + L] for k in range(CONV_WIDTH))
    y = b_gate * conv * jax.nn.silu(z)
    new_state = padded[:, -(CONV_WIDTH - 1):]
    return y @ w_out, new_state


def _scan_combine(e1, e2):
    a1, b1 = e1
    a2, b2 = e2
    return (a2 * a1, a2 * b1 + b2)


def s5_mixer(h, state_re, state_im, w_in, a_re, a_im, log_dt, b_re, b_im, c_re, c_im,
             d_skip, w_glu, b_glu, w_out):
    Bsz, L, _ = h.shape
    proj = h @ w_in
    u, z = jnp.split(proj, 2, axis=-1)
    uf = u.astype(jnp.float32)
    f32 = jnp.float32
    lam = lax.complex(a_re.astype(f32), a_im.astype(f32))
    dt = jnp.exp(log_dt.astype(f32))[:, None]
    lam_bar = jnp.exp(lam * dt)
    b_c = lax.complex(b_re.astype(f32), b_im.astype(f32))
    b_bar = ((lam_bar - 1.0) / lam)[..., None] * b_c
    c_c = lax.complex(c_re.astype(f32), c_im.astype(f32))
    h0 = lax.complex(state_re.astype(f32), state_im.astype(f32))

    chunk = SCAN_CHUNK if (L % SCAN_CHUNK == 0) else L
    n_chunks = L // chunk
    u_chunks = uf.reshape(Bsz, n_chunks, chunk, N_GROUPS, GROUP_SIZE).transpose(1, 2, 0, 3, 4)
    a_elems = jnp.broadcast_to(lam_bar, (chunk, 1, N_GROUPS, STATE_DIM))

    def step(carry, u_c):
        bu = jnp.einsum('tbgh,gph->tbgp', u_c.astype(jnp.complex64), b_bar)
        bu = bu.at[0].add(lam_bar * carry)
        _, states = lax.associative_scan(_scan_combine, (a_elems, bu), axis=0)
        y_c = jnp.real(jnp.einsum('ghp,tbgp->tbgh', c_c, states))
        return states[-1], y_c

    h_last, ys = lax.scan(step, h0, u_chunks)
    y = ys.transpose(2, 0, 1, 3, 4).reshape(Bsz, L, D_INNER)
    y = y + d_skip.astype(f32) * uf
    y = jax.nn.gelu(y)
    y = y * jax.nn.sigmoid(y @ w_glu.astype(f32) + b_glu.astype(f32))
    out = (y.astype(h.dtype) * jax.nn.silu(z)) @ w_out
    return out, jnp.real(h_last), jnp.imag(h_last)


def setup_inputs(seed: int = 0) -> dict:
    key = jax.random.key(seed)
    ks = jax.random.split(key, 24)
    f32 = jnp.float32
    nc, ns = N_CONV_LAYERS, N_SSM_LAYERS
    x_prompt = jax.random.normal(ks[0], (BATCH, SEQ, D_MODEL), f32)
    x_sample = jax.random.normal(ks[1], (DEC_BATCH, DEC_SEQ, D_MODEL), f32)
    state_conv = jax.random.normal(ks[2], (nc, DEC_BATCH, CONV_WIDTH - 1, D_INNER), f32)
    state_ssm_re = jax.random.normal(ks[3], (ns, DEC_BATCH, N_GROUPS, STATE_DIM), f32)
    state_ssm_im = jax.random.normal(ks[4], (ns, DEC_BATCH, N_GROUPS, STATE_DIM), f32)

    conv_norm = 1.0 + 0.02 * jax.random.normal(ks[5], (nc, D_MODEL), f32)
    conv_w_in = jax.random.normal(ks[6], (nc, D_MODEL, 4 * D_INNER), f32) * D_MODEL ** -0.5
    conv_w = jax.random.normal(ks[7], (nc, CONV_WIDTH, D_INNER), f32) * CONV_WIDTH ** -0.5
    conv_w_out = jax.random.normal(ks[8], (nc, D_INNER, D_MODEL), f32) * D_INNER ** -0.5

    ssm_norm = 1.0 + 0.02 * jax.random.normal(ks[9], (ns, D_MODEL), f32)
    ssm_w_in = jax.random.normal(ks[10], (ns, D_MODEL, 2 * D_INNER), f32) * D_MODEL ** -0.5
    ssm_a_re = -0.5 + 0.01 * jax.random.normal(ks[11], (ns, N_GROUPS, STATE_DIM), f32)
    n_idx = jnp.arange(STATE_DIM, dtype=f32)
    ssm_a_im = math.pi * n_idx + 0.01 * jax.random.normal(ks[12], (ns, N_GROUPS, STATE_DIM), f32)
    ssm_log_dt = jax.random.uniform(ks[13], (ns, N_GROUPS), f32, math.log(1e-3), math.log(1e-1))
    ssm_b_re = jax.random.normal(ks[14], (ns, N_GROUPS, STATE_DIM, GROUP_SIZE), f32) * (2 * GROUP_SIZE) ** -0.5
    ssm_b_im = jax.random.normal(ks[15], (ns, N_GROUPS, STATE_DIM, GROUP_SIZE), f32) * (2 * GROUP_SIZE) ** -0.5
    ssm_c_re = jax.random.normal(ks[16], (ns, N_GROUPS, GROUP_SIZE, STATE_DIM), f32) * (2 * STATE_DIM) ** -0.5
    ssm_c_im = jax.random.normal(ks[17], (ns, N_GROUPS, GROUP_SIZE, STATE_DIM), f32) * (2 * STATE_DIM) ** -0.5
    ssm_d = jax.random.normal(ks[18], (ns, D_INNER), f32) * 0.5
    ssm_w_glu = jax.random.normal(ks[19], (ns, D_INNER, D_INNER), f32) * D_INNER ** -0.5
    ssm_b_glu = 0.01 * jax.random.normal(ks[20], (ns, D_INNER), f32)
    ssm_w_out = jax.random.normal(ks[21], (ns, D_INNER, D_MODEL), f32) * D_INNER ** -0.5
    final_norm = 1.0 + 0.02 * jax.random.normal(ks[22], (D_MODEL,), f32)
    return {
        "x_prompt": x_prompt, "x_sample": x_sample,
        "state_conv": state_conv, "state_ssm_re": state_ssm_re, "state_ssm_im": state_ssm_im,
        "conv_norm": conv_norm, "conv_w_in": conv_w_in, "conv_w": conv_w, "conv_w_out": conv_w_out,
        "ssm_norm": ssm_norm, "ssm_w_in": ssm_w_in, "ssm_a_re": ssm_a_re, "ssm_a_im": ssm_a_im,
        "ssm_log_dt": ssm_log_dt, "ssm_b_re": ssm_b_re, "ssm_b_im": ssm_b_im,
        "ssm_c_re": ssm_c_re, "ssm_c_im": ssm_c_im, "ssm_d": ssm_d,
        "ssm_w_glu": ssm_w_glu, "ssm_b_glu": ssm_b_glu, "ssm_w_out": ssm_w_out,
        "final_norm": final_norm,
    }


def reference(x_prompt, x_sample, state_conv, state_ssm_re, state_ssm_im,
              conv_norm, conv_w_in, conv_w, conv_w_out,
              ssm_norm, ssm_w_in, ssm_a_re, ssm_a_im, ssm_log_dt, ssm_b_re, ssm_b_im,
              ssm_c_re, ssm_c_im, ssm_d, ssm_w_glu, ssm_b_glu, ssm_w_out, final_norm):
    xp, xs = x_prompt, x_sample
    conv_p, conv_s = [], []
    re_p, im_p, re_s, im_s = [], [], [], []
    for i in range(DEPTH):
        j = i // N_MIXERS
        if i % N_MIXERS == 0:
            zero_buf = jnp.zeros((xp.shape[0], CONV_WIDTH - 1, D_INNER), xp.dtype)
            op, sp = short_conv_mixer(rmsnorm(xp, conv_norm[j]), zero_buf,
                                      conv_w_in[j], conv_w[j], conv_w_out[j])
            os_, ss = short_conv_mixer(rmsnorm(xs, conv_norm[j]), state_conv[j],
                                       conv_w_in[j], conv_w[j], conv_w_out[j])
            xp = xp + op
            xs = xs + os_
            conv_p.append(sp)
            conv_s.append(ss)
        else:
            params = (ssm_w_in[j], ssm_a_re[j], ssm_a_im[j], ssm_log_dt[j], ssm_b_re[j], ssm_b_im[j],
                      ssm_c_re[j], ssm_c_im[j], ssm_d[j], ssm_w_glu[j], ssm_b_glu[j], ssm_w_out[j])
            zero_h = jnp.zeros((xp.shape[0], N_GROUPS, STATE_DIM), jnp.float32)
            op, hr_p, hi_p = s5_mixer(rmsnorm(xp, ssm_norm[j]), zero_h, zero_h, *params)
            os_, hr_s, hi_s = s5_mixer(rmsnorm(xs, ssm_norm[j]), state_ssm_re[j], state_ssm_im[j], *params)
            xp = xp + op
            xs = xs + os_
            re_p.append(hr_p)
            im_p.append(hi_p)
            re_s.append(hr_s)
            im_s.append(hi_s)
    y_prompt = rmsnorm(xp, final_norm)
    y_sample = rmsnorm(xs, final_norm)
    new_conv_prompt = jnp.stack(conv_p)
    new_conv_sample = jnp.stack(conv_s)
    new_ssm_re_prompt = jnp.stack(re_p)
    new_ssm_im_prompt = jnp.stack(im_p)
    new_ssm_re_sample = jnp.stack(re_s)
    new_ssm_im_sample = jnp.stack(im_s)
    return (y_prompt, y_sample, new_conv_prompt, new_conv_sample,
            new_ssm_re_prompt, new_ssm_im_prompt, new_ssm_re_sample, new_ssm_im_sample)
```

```python
import functools

import jax
import jax.numpy as jnp
from jax import lax
from jax.experimental import pallas as pl
from jax.experimental.pallas import tpu as pltpu

D_MODEL = 2048
D_INNER = 2 * D_MODEL
CONV_WIDTH = 3
GROUP_SIZE = 16
N_GROUPS = D_INNER // GROUP_SIZE
STATE_DIM = 64
EPS = 1e-6

CHUNK = 256
N_CHUNKS = D_INNER // CHUNK
GROUPS_PER_CHUNK = CHUNK // GROUP_SIZE
STATES_PER_CHUNK = GROUPS_PER_CHUNK * STATE_DIM
SUBLANES = 8
VMEM_LIMIT_BYTES = 56 * 1024 * 1024

_F32 = jnp.float32
_BF16 = jnp.bfloat16


def _dot(a, b):
    return jnp.dot(a, b, preferred_element_type=_F32)


def _rmsnorm(x, g):
    return x * lax.rsqrt(jnp.mean(x * x, axis=-1, keepdims=True) + EPS) * g


def _compiler_params():
    return pltpu.CompilerParams(
        dimension_semantics=("arbitrary", "arbitrary"),
        vmem_limit_bytes=VMEM_LIMIT_BYTES)


def _conv_layer_kernel(x_ref, g_ref, wb_ref, wc_ref, wv_ref, wz_ref, cw_ref, wo_ref, st_ref,
                       o_ref, ns_ref, h_scr, ubuf, carry, *, rows_per_step):
    i = pl.program_id(0)
    j = pl.program_id(1)
    tm = x_ref.shape[0]
    r = rows_per_step

    @pl.when(j == 0)
    def _():
        x = x_ref[...]
        h_scr[...] = _rmsnorm(x, g_ref[...]).astype(_BF16)
        o_ref[...] = x

    @pl.when(i == 0)
    def _():
        carry[j] = st_ref[...]

    h = h_scr[...]
    b = _dot(h, wb_ref[...])
    c = _dot(h, wc_ref[...])
    v = _dot(h, wv_ref[...])
    z = _dot(h, wz_ref[...])
    ubuf[0:2 * r, :] = carry[j]
    ubuf[2 * r:2 * r + tm, :] = c * v
    cw = cw_ref[...]
    conv = (cw[0:1, :] * ubuf[0:tm, :] + cw[1:2, :] * ubuf[r:r + tm, :]
            + cw[2:3, :] * ubuf[2 * r:2 * r + tm, :])
    y = b * conv * (z * jax.nn.sigmoid(z))
    last_two = ubuf[tm:tm + 2 * r, :]
    carry[j] = last_two
    ns_ref[...] = last_two
    o_ref[...] += _dot(y.astype(_BF16), wo_ref[...])


def _conv_layer(x, norm_g, w_in, conv_w, w_out, state, *, rows_per_step, tm):
    m = x.shape[0]
    r = rows_per_step
    grid = (m // tm, N_CHUNKS)
    w_in_spec = lambda part: pl.BlockSpec((D_MODEL, CHUNK), lambda i, j: (0, part * N_CHUNKS + j))
    return pl.pallas_call(
        functools.partial(_conv_layer_kernel, rows_per_step=r),
        grid=grid,
        in_specs=[
            pl.BlockSpec((tm, D_MODEL), lambda i, j: (i, 0)),
            pl.BlockSpec((1, D_MODEL), lambda i, j: (0, 0)),
            w_in_spec(0), w_in_spec(1), w_in_spec(2), w_in_spec(3),
            pl.BlockSpec((CONV_WIDTH, CHUNK), lambda i, j: (0, j)),
            pl.BlockSpec((CHUNK, D_MODEL), lambda i, j: (j, 0)),
            pl.BlockSpec((2 * r, CHUNK), lambda i, j: (0, j)),
        ],
        out_specs=[
            pl.BlockSpec((tm, D_MODEL), lambda i, j: (i, 0)),
            pl.BlockSpec((2 * r, CHUNK), lambda i, j: (0, j)),
        ],
        out_shape=[
            jax.ShapeDtypeStruct((m, D_MODEL), _F32),
            jax.ShapeDtypeStruct((2 * r, D_INNER), _F32),
        ],
        scratch_shapes=[
            pltpu.VMEM((tm, D_MODEL), _BF16),
            pltpu.VMEM((tm + 2 * r, CHUNK), _F32),
            pltpu.VMEM((N_CHUNKS, 2 * r, CHUNK), _F32),
        ],
        compiler_params=_compiler_params(),
        name="conv_layer",
    )(x, norm_g, w_in, w_in, w_in, w_in, conv_w, w_out, state)


def _ssm_prep_kernel(are_ref, aim_ref, logdt_ref, btre_ref, btim_ref,
                     lre_ref, lim_ref, bbre_ref, bbim_ref):
    a_re = are_ref[...]
    a_im = aim_ref[...]
    dt = jnp.exp(logdt_ref[...])
    mag = jnp.exp(a_re * dt)
    l_re = mag * jnp.cos(a_im * dt)
    l_im = mag * jnp.sin(a_im * dt)
    lre_ref[...] = l_re
    lim_ref[...] = l_im
    n_re = l_re - 1.0
    den = a_re * a_re + a_im * a_im
    k_re = (n_re * a_re + l_im * a_im) / den
    k_im = (l_im * a_re - n_re * a_im) / den
    g = a_re.shape[0]
    expand = lambda k: jnp.broadcast_to(
        k[:, None, :], (g, GROUP_SIZE, STATE_DIM)).reshape(g * GROUP_SIZE, STATE_DIM)
    k_re = expand(k_re)
    k_im = expand(k_im)
    bt_re = btre_ref[...]
    bt_im = btim_ref[...]
    bbre_ref[...] = k_re * bt_re - k_im * bt_im
    bbim_ref[...] = k_re * bt_im + k_im * bt_re


def _ssm_prep(a_re, a_im, log_dt, b_re, b_im):
    gb = SUBLANES
    bt_re = b_re.transpose(0, 2, 1).reshape(N_GROUPS * GROUP_SIZE, STATE_DIM)
    bt_im = b_im.transpose(0, 2, 1).reshape(N_GROUPS * GROUP_SIZE, STATE_DIM)
    gp_spec = pl.BlockSpec((gb, STATE_DIM), lambda i: (i, 0))
    ghp_spec = pl.BlockSpec((gb * GROUP_SIZE, STATE_DIM), lambda i: (i, 0))
    return pl.pallas_call(
        _ssm_prep_kernel,
        grid=(N_GROUPS // gb,),
        in_specs=[gp_spec, gp_spec, pl.BlockSpec((gb, 1), lambda i: (i, 0)), ghp_spec, ghp_spec],
        out_specs=[gp_spec, gp_spec, ghp_spec, ghp_spec],
        out_shape=[
            jax.ShapeDtypeStruct((N_GROUPS, STATE_DIM), _F32),
            jax.ShapeDtypeStruct((N_GROUPS, STATE_DIM), _F32),
            jax.ShapeDtypeStruct((N_GROUPS * GROUP_SIZE, STATE_DIM), _F32),
            jax.ShapeDtypeStruct((N_GROUPS * GROUP_SIZE, STATE_DIM), _F32),
        ],
        name="ssm_prep",
    )(a_re, a_im, log_dt.reshape(N_GROUPS, 1), bt_re, bt_im)


def _block_diag_in(bb):
    v = bb.reshape(N_CHUNKS, GROUPS_PER_CHUNK, GROUP_SIZE, 1, STATE_DIM)
    eye = jnp.eye(GROUPS_PER_CHUNK, dtype=bool)[None, :, None, :, None]
    out = jnp.where(eye, v, 0.0)
    return out.reshape(N_CHUNKS, CHUNK, STATES_PER_CHUNK)


def _block_diag_out(c):
    v = c.reshape(N_CHUNKS, GROUPS_PER_CHUNK, GROUP_SIZE, STATE_DIM).transpose(0, 1, 3, 2)
    v = v[:, :, :, None, :]
    eye = jnp.eye(GROUPS_PER_CHUNK, dtype=bool)[None, :, None, :, None]
    out = jnp.where(eye, v, 0.0)
    return out.reshape(N_CHUNKS, STATES_PER_CHUNK, CHUNK)


def _ssm_scan_kernel(x_ref, g_ref, wu_ref, wz_ref, bd_ref, cre_ref, cim_ref, lre_ref, lim_ref,
                     d_ref, s0re_ref, s0im_ref,
                     y_ref, zs_ref, nsre_ref, nsim_ref,
                     h_scr, xre, xim, carry_re, carry_im, *, rows_per_step):
    i = pl.program_id(0)
    j = pl.program_id(1)
    tm = x_ref.shape[0]
    r = rows_per_step
    sp = STATES_PER_CHUNK

    @pl.when(j == 0)
    def _():
        h_scr[...] = _rmsnorm(x_ref[...], g_ref[...]).astype(_BF16)

    @pl.when(i == 0)
    def _():
        carry_re[j] = s0re_ref[...]
        carry_im[j] = s0im_ref[...]

    h = h_scr[...]
    u = _dot(h, wu_ref[...])
    z = _dot(h, wz_ref[...])
    bu = _dot(u.astype(_BF16), bd_ref[0])
    xre[...] = bu[:, :sp]
    xim[...] = bu[:, sp:]

    c_re = carry_re[j]
    c_im = carry_im[j]
    if r % SUBLANES == 0:
        l_re = lre_ref[0]
        l_im = lim_ref[0]

        def step(s, c):
            p_re, p_im = c
            rows = pl.ds(pl.multiple_of(s * r, r), r)
            n_re = l_re * p_re - l_im * p_im + xre[rows, :]
            n_im = l_re * p_im + l_im * p_re + xim[rows, :]
            xre[rows, :] = n_re
            xim[rows, :] = n_im
            return n_re, n_im

        if tm == r:
            c_re, c_im = step(0, (c_re, c_im))
        else:
            c_re, c_im = lax.fori_loop(0, tm // r, step, (c_re, c_im))
    else:
        assert 2 * r == SUBLANES
        l_re = jnp.broadcast_to(lre_ref[0], (r, sp))
        l_im = jnp.broadcast_to(lim_ref[0], (r, sp))

        def pair(k, c):
            p_re, p_im = c
            rows = pl.ds(pl.multiple_of(k * SUBLANES, SUBLANES), SUBLANES)
            v_re = xre[rows, :]
            v_im = xim[rows, :]
            a_re = l_re * p_re - l_im * p_im + v_re[0:r]
            a_im = l_re * p_im + l_im * p_re + v_im[0:r]
            b_re = l_re * a_re - l_im * a_im + v_re[r:2 * r]
            b_im = l_re * a_im + l_im * a_re + v_im[r:2 * r]
            xre[rows, :] = jnp.concatenate([a_re, b_re], axis=0)
            xim[rows, :] = jnp.concatenate([a_im, b_im], axis=0)
            return b_re, b_im

        c_re, c_im = lax.fori_loop(0, tm // SUBLANES, pair, (c_re, c_im))
    carry_re[j] = c_re
    carry_im[j] = c_im
    nsre_ref[...] = c_re
    nsim_ref[...] = c_im

    y = _dot(xre[...].astype(_BF16), cre_ref[0]) - _dot(xim[...].astype(_BF16), cim_ref[0])
    y = jax.nn.gelu(y + d_ref[...] * u)
    y_ref[...] = y.astype(_BF16)
    zs_ref[...] = (z * jax.nn.sigmoid(z)).astype(_BF16)


def _ssm_scan(x, norm_g, w_in, bd, cd_re, cd_im, lam_re, lam_im, d_skip, s0_re, s0_im,
              *, rows_per_step, tm):
    m = x.shape[0]
    r = rows_per_step
    sp = STATES_PER_CHUNK
    chunk3 = lambda a, b: pl.BlockSpec((1, a, b), lambda i, j: (j, 0, 0))
    state_spec = pl.BlockSpec((r, sp), lambda i, j: (0, j))
    act_spec = pl.BlockSpec((tm, CHUNK), lambda i, j: (i, j))
    return pl.pallas_call(
        functools.partial(_ssm_scan_kernel, rows_per_step=r),
        grid=(m // tm, N_CHUNKS),
        in_specs=[
            pl.BlockSpec((tm, D_MODEL), lambda i, j: (i, 0)),
            pl.BlockSpec((1, D_MODEL), lambda i, j: (0, 0)),
            pl.BlockSpec((D_MODEL, CHUNK), lambda i, j: (0, j)),
            pl.BlockSpec((D_MODEL, CHUNK), lambda i, j: (0, N_CHUNKS + j)),
            chunk3(CHUNK, 2 * sp), chunk3(sp, CHUNK), chunk3(sp, CHUNK),
            chunk3(1, sp), chunk3(1, sp),
            pl.BlockSpec((1, CHUNK), lambda i, j: (0, j)),
            state_spec, state_spec,
        ],
        out_specs=[act_spec, act_spec, state_spec, state_spec],
        out_shape=[
            jax.ShapeDtypeStruct((m, D_INNER), _BF16),
            jax.ShapeDtypeStruct((m, D_INNER), _BF16),
            jax.ShapeDtypeStruct((r, N_GROUPS * STATE_DIM), _F32),
            jax.ShapeDtypeStruct((r, N_GROUPS * STATE_DIM), _F32),
        ],
        scratch_shapes=[
            pltpu.VMEM((tm, D_MODEL), _BF16),
            pltpu.VMEM((tm, sp), _F32),
            pltpu.VMEM((tm, sp), _F32),
            pltpu.VMEM((N_CHUNKS, r, sp), _F32),
            pltpu.VMEM((N_CHUNKS, r, sp), _F32),
        ],
        compiler_params=_compiler_params(),
        name="ssm_scan",
    )(x, norm_g, w_in, w_in, bd, cd_re, cd_im, lam_re, lam_im, d_skip, s0_re, s0_im)


def _glu_out_kernel(y_ref, yj_ref, zs_ref, wg_ref, bg_ref, wo_ref, x_ref, fn_ref, o_ref,
                    *, final_norm):
    j = pl.program_id(1)

    @pl.when(j == 0)
    def _():
        o_ref[...] = x_ref[...]

    gate = jax.nn.sigmoid(_dot(y_ref[...], wg_ref[...]) + bg_ref[...])
    yy = yj_ref[...].astype(_F32) * gate * zs_ref[...].astype(_F32)
    o_ref[...] += _dot(yy.astype(_BF16), wo_ref[...])

    if final_norm:
        @pl.when(j == pl.num_programs(1) - 1)
        def _():
            o_ref[...] = _rmsnorm(o_ref[...], fn_ref[...])


def _glu_out(y, zs, w_glu, b_glu, w_out, x, final_g, *, final_norm, tm):
    m = x.shape[0]
    return pl.pallas_call(
        functools.partial(_glu_out_kernel, final_norm=final_norm),
        grid=(m // tm, N_CHUNKS),
        in_specs=[
            pl.BlockSpec((tm, D_INNER), lambda i, j: (i, 0)),
            pl.BlockSpec((tm, CHUNK), lambda i, j: (i, j)),
            pl.BlockSpec((tm, CHUNK), lambda i, j: (i, j)),
            pl.BlockSpec((D_INNER, CHUNK), lambda i, j: (0, j)),
            pl.BlockSpec((1, CHUNK), lambda i, j: (0, j)),
            pl.BlockSpec((CHUNK, D_MODEL), lambda i, j: (j, 0)),
            pl.BlockSpec((tm, D_MODEL), lambda i, j: (i, 0)),
            pl.BlockSpec((1, D_MODEL), lambda i, j: (0, 0)),
        ],
        out_specs=pl.BlockSpec((tm, D_MODEL), lambda i, j: (i, 0)),
        out_shape=jax.ShapeDtypeStruct((m, D_MODEL), _F32),
        compiler_params=_compiler_params(),
        name="glu_out",
    )(y, y, zs, w_glu, b_glu, w_out, x, final_g)


def kernel(x_prompt, x_sample, state_conv, state_ssm_re, state_ssm_im,
           conv_norm, conv_w_in, conv_w, conv_w_out,
           ssm_norm, ssm_w_in, ssm_a_re, ssm_a_im, ssm_log_dt, ssm_b_re, ssm_b_im,
           ssm_c_re, ssm_c_im, ssm_d, ssm_w_glu, ssm_b_glu, ssm_w_out, final_norm):
    batch, seq, _ = x_prompt.shape
    dec_batch = x_sample.shape[0]
    n_conv = conv_w_in.shape[0]
    n_ssm = ssm_w_in.shape[0]
    depth = n_conv + n_ssm
    n_states = N_GROUPS * STATE_DIM
    tm_p = 512

    xp = x_prompt.transpose(1, 0, 2).reshape(seq * batch, D_MODEL)
    xs = x_sample.reshape(dec_batch, D_MODEL)
    final_g = final_norm.reshape(1, D_MODEL)

    conv_p, conv_s, re_p, im_p, re_s, im_s = [], [], [], [], [], []
    for layer in range(depth):
        l = layer // 2
        last = layer == depth - 1
        if layer % 2 == 0:
            g = conv_norm[l].reshape(1, D_MODEL)
            w_in = conv_w_in[l].astype(_BF16)
            w_out = conv_w_out[l].astype(_BF16)
            zero_state = jnp.zeros((2 * batch, D_INNER), _F32)
            sample_state = state_conv[l].transpose(1, 0, 2).reshape(2 * dec_batch, D_INNER)
            xp, ns_p = _conv_layer(xp, g, w_in, conv_w[l], w_out, zero_state,
                                   rows_per_step=batch, tm=tm_p)
            xs, ns_s = _conv_layer(xs, g, w_in, conv_w[l], w_out, sample_state,
                                   rows_per_step=dec_batch, tm=dec_batch)
            conv_p.append(ns_p.reshape(2, batch, D_INNER).transpose(1, 0, 2))
            conv_s.append(ns_s.reshape(2, dec_batch, D_INNER).transpose(1, 0, 2))
        else:
            g = ssm_norm[l].reshape(1, D_MODEL)
            w_in = ssm_w_in[l].astype(_BF16)
            w_glu = ssm_w_glu[l].astype(_BF16)
            w_out = ssm_w_out[l].astype(_BF16)
            lam_re, lam_im, bb_re, bb_im = _ssm_prep(
                ssm_a_re[l], ssm_a_im[l], ssm_log_dt[l], ssm_b_re[l], ssm_b_im[l])
            bd = jnp.concatenate([_block_diag_in(bb_re), _block_diag_in(bb_im)],
                                 axis=-1).astype(_BF16)
            cd_re = _block_diag_out(ssm_c_re[l]).astype(_BF16)
            cd_im = _block_diag_out(ssm_c_im[l]).astype(_BF16)
            lam_re = lam_re.reshape(N_CHUNKS, 1, STATES_PER_CHUNK)
            lam_im = lam_im.reshape(N_CHUNKS, 1, STATES_PER_CHUNK)
            d_skip = ssm_d[l].reshape(1, D_INNER)
            b_glu = ssm_b_glu[l].reshape(1, D_INNER)
            zero_state = jnp.zeros((batch, n_states), _F32)
            y, zs, hr_p, hi_p = _ssm_scan(
                xp, g, w_in, bd, cd_re, cd_im, lam_re, lam_im, d_skip, zero_state, zero_state,
                rows_per_step=batch, tm=tm_p)
            xp = _glu_out(y, zs, w_glu, b_glu, w_out, xp, final_g, final_norm=last, tm=tm_p)
            y, zs, hr_s, hi_s = _ssm_scan(
                xs, g, w_in, bd, cd_re, cd_im, lam_re, lam_im, d_skip,
                state_ssm_re[l].reshape(dec_batch, n_states),
                state_ssm_im[l].reshape(dec_batch, n_states),
                rows_per_step=dec_batch, tm=dec_batch)
            xs = _glu_out(y, zs, w_glu, b_glu, w_out, xs, final_g, final_norm=last, tm=dec_batch)
            re_p.append(hr_p.reshape(batch, N_GROUPS, STATE_DIM))
            im_p.append(hi_p.reshape(batch, N_GROUPS, STATE_DIM))
            re_s.append(hr_s.reshape(dec_batch, N_GROUPS, STATE_DIM))
            im_s.append(hi_s.reshape(dec_batch, N_GROUPS, STATE_DIM))

    y_prompt = xp.reshape(seq, batch, D_MODEL).transpose(1, 0, 2)
    y_sample = xs.reshape(dec_batch, 1, D_MODEL)
    return (y_prompt, y_sample, jnp.stack(conv_p), jnp.stack(conv_s),
            jnp.stack(re_p), jnp.stack(im_p), jnp.stack(re_s), jnp.stack(im_s))
```

```python
import functools

import jax
import jax.numpy as jnp
from jax import lax
from jax.experimental import pallas as pl
from jax.experimental.pallas import tpu as pltpu

D_MODEL = 2048
D_INNER = 2 * D_MODEL
CONV_WIDTH = 3
GROUP_SIZE = 16
N_GROUPS = D_INNER // GROUP_SIZE
STATE_DIM = 64
N_STATES = N_GROUPS * STATE_DIM
EPS = 1e-6

LANES = 128
SUBLANES = 8
CHUNK = 2 * LANES
N_CHUNKS = D_INNER // CHUNK
N_OUT_CHUNKS = D_MODEL // CHUNK
GROUPS_PER_HALF = LANES // GROUP_SIZE
STATES_PER_HALF = GROUPS_PER_HALF * STATE_DIM
STATES_PER_CHUNK = 2 * STATES_PER_HALF
VMEM_LIMIT_BYTES = 56 * 1024 * 1024

_F32 = jnp.float32
_BF16 = jnp.bfloat16


def _dot(a, b):
    return jnp.dot(a, b, preferred_element_type=_F32)


def _wdot(a, w_ref):
    return _dot(a, w_ref[...].astype(_BF16))


def _rmsnorm(x, g):
    return x * lax.rsqrt(jnp.mean(x * x, axis=-1, keepdims=True) + EPS) * g


def _silu(z):
    return z * jax.nn.sigmoid(z)


def _compiler_params(n_axes=2):
    return pltpu.CompilerParams(
        dimension_semantics=("arbitrary",) * n_axes,
        vmem_limit_bytes=VMEM_LIMIT_BYTES)


def _last_block_only(n_blocks):
    return lambda i, j: (0, jnp.where(i == n_blocks - 1, j, 0))


def _conv_in_kernel(x_ref, g_ref, wb_ref, wc_ref, wv_ref, wz_ref, cw_ref, s2_ref, s1_ref,
                    y_ref, n2_ref, n1_ref, h_scr, ubuf, carry, *, rows_per_step):
    i = pl.program_id(0)
    j = pl.program_id(1)
    tm = x_ref.shape[0]
    r = rows_per_step

    @pl.when(j == 0)
    def _():
        h_scr[...] = _rmsnorm(x_ref[...], g_ref[...]).astype(_BF16)

    @pl.when(i == 0)
    def _():
        carry[j, 0:r, :] = s2_ref[...]
        carry[j, r:2 * r, :] = s1_ref[...]

    h = h_scr[...]
    b = _wdot(h, wb_ref)
    c = _wdot(h, wc_ref)
    v = _wdot(h, wv_ref)
    z = _wdot(h, wz_ref)
    ubuf[0:2 * r, :] = carry[j]
    ubuf[2 * r:2 * r + tm, :] = c * v
    cw = cw_ref[...]
    conv = (cw[0:1, :] * ubuf[0:tm, :] + cw[1:2, :] * ubuf[r:r + tm, :]
            + cw[2:3, :] * ubuf[2 * r:2 * r + tm, :])
    y_ref[...] = (b * conv * _silu(z)).astype(_BF16)
    carry[j] = ubuf[tm:tm + 2 * r, :]

    @pl.when(i == pl.num_programs(0) - 1)
    def _():
        n2_ref[...] = ubuf[tm:tm + r, :]
        n1_ref[...] = ubuf[tm + r:tm + 2 * r, :]


def _conv_in(x, norm_g, w_in, conv_w, state, layer, *, rows_per_step, tm):
    m = x.shape[0]
    r = rows_per_step
    n_blocks = m // tm
    w_in_spec = lambda part: pl.BlockSpec(
        (None, D_MODEL, CHUNK), lambda i, j: (layer, 0, part * N_CHUNKS + j))
    state_in = lambda k: pl.BlockSpec((r, CHUNK), lambda i, j: (0, k * N_CHUNKS + j))
    state_out = pl.BlockSpec((r, CHUNK), _last_block_only(n_blocks))
    y, n2, n1 = pl.pallas_call(
        functools.partial(_conv_in_kernel, rows_per_step=r),
        grid=(n_blocks, N_CHUNKS),
        in_specs=[
            pl.BlockSpec((tm, D_MODEL), lambda i, j: (i, 0)),
            pl.BlockSpec((None, 1, D_MODEL), lambda i, j: (layer, 0, 0)),
            w_in_spec(0), w_in_spec(1), w_in_spec(2), w_in_spec(3),
            pl.BlockSpec((None, CONV_WIDTH, CHUNK), lambda i, j: (layer, 0, j)),
            state_in(0), state_in(1),
        ],
        out_specs=[
            pl.BlockSpec((tm, CHUNK), lambda i, j: (i, j)),
            state_out, state_out,
        ],
        out_shape=[
            jax.ShapeDtypeStruct((m, D_INNER), _BF16),
            jax.ShapeDtypeStruct((r, D_INNER), _F32),
            jax.ShapeDtypeStruct((r, D_INNER), _F32),
        ],
        scratch_shapes=[
            pltpu.VMEM((tm, D_MODEL), _BF16),
            pltpu.VMEM((tm + 2 * r, CHUNK), _F32),
            pltpu.VMEM((N_CHUNKS, 2 * r, CHUNK), _F32),
        ],
        compiler_params=_compiler_params(),
        name="conv_in",
    )(x, norm_g, w_in, w_in, w_in, w_in, conv_w, state, state)
    return y, jnp.stack([n2, n1], axis=1)


def _out_proj_kernel(y_ref, w_ref, x_ref, o_ref):
    o_ref[...] = x_ref[...] + _wdot(y_ref[...], w_ref)


def _out_proj(y, w_out, x, layer, *, tm):
    m = x.shape[0]
    return pl.pallas_call(
        _out_proj_kernel,
        grid=(m // tm, N_OUT_CHUNKS),
        in_specs=[
            pl.BlockSpec((tm, D_INNER), lambda i, n: (i, 0)),
            pl.BlockSpec((None, D_INNER, CHUNK), lambda i, n: (layer, 0, n)),
            pl.BlockSpec((tm, CHUNK), lambda i, n: (i, n)),
        ],
        out_specs=pl.BlockSpec((tm, CHUNK), lambda i, n: (i, n)),
        out_shape=jax.ShapeDtypeStruct((m, D_MODEL), _F32),
        compiler_params=_compiler_params(),
        name="out_proj",
    )(y, w_out, x)


def _ssm_prep_kernel(are_ref, aim_ref, logdt_ref, btre_ref, btim_ref,
                     lre_ref, lim_ref, bbre_ref, bbim_ref):
    a_re = are_ref[...]
    a_im = aim_ref[...]
    dt = jnp.exp(logdt_ref[...])
    mag = jnp.exp(a_re * dt)
    l_re = mag * jnp.cos(a_im * dt)
    l_im = mag * jnp.sin(a_im * dt)
    lre_ref[...] = l_re
    lim_ref[...] = l_im
    n_re = l_re - 1.0
    den = a_re * a_re + a_im * a_im
    k_re = (n_re * a_re + l_im * a_im) / den
    k_im = (l_im * a_re - n_re * a_im) / den
    g = a_re.shape[0]
    expand = lambda k: jnp.broadcast_to(
        k[:, None, :], (g, GROUP_SIZE, STATE_DIM)).reshape(g * GROUP_SIZE, STATE_DIM)
    k_re = expand(k_re)
    k_im = expand(k_im)
    bt_re = btre_ref[...]
    bt_im = btim_ref[...]
    bbre_ref[...] = k_re * bt_re - k_im * bt_im
    bbim_ref[...] = k_re * bt_im + k_im * bt_re


def _ssm_prep(a_re, a_im, log_dt, b_re, b_im):
    gb = SUBLANES
    bt_re = b_re.transpose(0, 2, 1).reshape(N_GROUPS * GROUP_SIZE, STATE_DIM)
    bt_im = b_im.transpose(0, 2, 1).reshape(N_GROUPS * GROUP_SIZE, STATE_DIM)
    gp_spec = pl.BlockSpec((gb, STATE_DIM), lambda i: (i, 0))
    ghp_spec = pl.BlockSpec((gb * GROUP_SIZE, STATE_DIM), lambda i: (i, 0))
    return pl.pallas_call(
        _ssm_prep_kernel,
        grid=(N_GROUPS // gb,),
        in_specs=[gp_spec, gp_spec, pl.BlockSpec((gb, 1), lambda i: (i, 0)), ghp_spec, ghp_spec],
        out_specs=[gp_spec, gp_spec, ghp_spec, ghp_spec],
        out_shape=[
            jax.ShapeDtypeStruct((N_GROUPS, STATE_DIM), _F32),
            jax.ShapeDtypeStruct((N_GROUPS, STATE_DIM), _F32),
            jax.ShapeDtypeStruct((N_GROUPS * GROUP_SIZE, STATE_DIM), _F32),
            jax.ShapeDtypeStruct((N_GROUPS * GROUP_SIZE, STATE_DIM), _F32),
        ],
        name="ssm_prep",
    )(a_re, a_im, log_dt.reshape(N_GROUPS, 1), bt_re, bt_im)


def _group_eye():
    return jnp.eye(GROUPS_PER_HALF, dtype=bool)


def _block_diag_in(bb):
    v = bb.reshape(N_CHUNKS, 2, GROUPS_PER_HALF, GROUP_SIZE, 1, STATE_DIM)
    eye = _group_eye()[None, None, :, None, :, None]
    return jnp.where(eye, v, 0.0).reshape(N_CHUNKS, CHUNK, STATES_PER_HALF)


def _block_diag_out(c):
    v = c.reshape(N_CHUNKS, 2, GROUPS_PER_HALF, GROUP_SIZE, STATE_DIM)
    v = v.transpose(0, 4, 1, 2, 3)[:, None]
    eye = _group_eye()[None, :, None, None, :, None]
    return jnp.where(eye, v, 0.0).reshape(N_CHUNKS, STATES_PER_HALF, CHUNK)


def _ssm_scan_kernel(x_ref, g_ref, wu_ref, wz_ref, bd_ref, cre_ref, cim_ref, lre_ref, lim_ref,
                     d_ref, s0re_ref, s0im_ref,
                     y_ref, zs_ref, nsre_ref, nsim_ref,
                     h_scr, xre, xim, carry_re, carry_im, *, rows_per_step):
    i = pl.program_id(0)
    j = pl.program_id(1)
    tm = x_ref.shape[0]
    r = rows_per_step
    sh = STATES_PER_HALF
    paired = 2 * r == SUBLANES
    assert paired or (r % SUBLANES == 0 and tm == r)

    @pl.when(j == 0)
    def _():
        h_scr[...] = _rmsnorm(x_ref[...], g_ref[...]).astype(_BF16)

    @pl.when(i == 0)
    def _():
        carry_re[j, 0:r, :] = s0re_ref[:, 0:sh]
        carry_re[j, r:2 * r, :] = s0re_ref[:, sh:2 * sh]
        carry_im[j, 0:r, :] = s0im_ref[:, 0:sh]
        carry_im[j, r:2 * r, :] = s0im_ref[:, sh:2 * sh]

    h = h_scr[...]
    u = _wdot(h, wu_ref)
    z = _wdot(h, wz_ref)

    first_lanes = lax.broadcasted_iota(jnp.int32, (tm, CHUNK), 1) < LANES
    u_first = jnp.where(first_lanes, u, 0.0)
    u_second = jnp.where(first_lanes, 0.0, u)
    if paired:
        low_rows = (lax.broadcasted_iota(jnp.int32, (tm, CHUNK), 0) & r) == 0
        u_down = pltpu.roll(u, r, 0)
        u_up = pltpu.roll(u, tm - r, 0)
        even = jnp.where(low_rows, u_first, jnp.where(first_lanes, 0.0, u_down))
        odd = jnp.where(low_rows, jnp.where(first_lanes, u_up, 0.0), u_second)
        lhs = jnp.concatenate([even, odd], axis=0)
    else:
        lhs = jnp.concatenate([u_first, u_second], axis=0)
    bu = _dot(lhs.astype(_BF16), bd_ref[...])
    xre[...] = bu[:, :sh]
    xim[...] = bu[:, sh:]

    half_rows = lax.broadcasted_iota(jnp.int32, (2 * r, sh), 0) < r
    l_re = jnp.where(half_rows, lre_ref[0:1, :], lre_ref[1:2, :])
    l_im = jnp.where(half_rows, lim_ref[0:1, :], lim_ref[1:2, :])

    def step(rows, p_re, p_im):
        n_re = l_re * p_re - l_im * p_im + xre[rows, :]
        n_im = l_re * p_im + l_im * p_re + xim[rows, :]
        xre[rows, :] = n_re
        xim[rows, :] = n_im
        return n_re, n_im

    c_re = carry_re[j]
    c_im = carry_im[j]
    if paired:
        def pair(k, c):
            base = pl.multiple_of(k * SUBLANES, SUBLANES)
            c = step(pl.ds(base, SUBLANES), *c)
            return step(pl.ds(tm + base, SUBLANES), *c)

        c_re, c_im = lax.fori_loop(0, tm // SUBLANES, pair, (c_re, c_im), unroll=4)
    else:
        c_re, c_im = step(pl.ds(0, 2 * r), c_re, c_im)
    carry_re[j] = c_re
    carry_im[j] = c_im

    @pl.when(i == pl.num_programs(0) - 1)
    def _():
        nsre_ref[:, 0:sh] = c_re[0:r]
        nsre_ref[:, sh:2 * sh] = c_re[r:2 * r]
        nsim_ref[:, 0:sh] = c_im[0:r]
        nsim_ref[:, sh:2 * sh] = c_im[r:2 * r]

    yy = _dot(xre[...].astype(_BF16), cre_ref[...]) - _dot(xim[...].astype(_BF16), cim_ref[...])
    if paired:
        low_rows = (lax.broadcasted_iota(jnp.int32, (tm, LANES), 0) & r) == 0
        y_even = yy[:tm]
        y_odd = yy[tm:]
        y_first = jnp.where(low_rows, y_even[:, :LANES], pltpu.roll(y_odd[:, :LANES], r, 0))
        y_second = jnp.where(low_rows, pltpu.roll(y_even[:, LANES:], tm - r, 0), y_odd[:, LANES:])
    else:
        y_first = yy[:r, :LANES]
        y_second = yy[r:, LANES:]
    y = jnp.concatenate([y_first, y_second], axis=1)
    y_ref[...] = jax.nn.gelu(y + d_ref[...] * u).astype(_BF16)
    zs_ref[...] = _silu(z).astype(_BF16)


def _ssm_scan(x, norm_g, w_in, bd, cd_re, cd_im, lam_re, lam_im, d_skip, s0_re, s0_im, layer,
              *, rows_per_step, tm):
    m = x.shape[0]
    r = rows_per_step
    n_blocks = m // tm
    sh = STATES_PER_HALF
    per_chunk = lambda a, b: pl.BlockSpec((None, a, b), lambda i, j: (j, 0, 0))
    act_spec = pl.BlockSpec((tm, CHUNK), lambda i, j: (i, j))
    state_in = pl.BlockSpec((r, STATES_PER_CHUNK), lambda i, j: (0, j))
    state_out = pl.BlockSpec((r, STATES_PER_CHUNK), _last_block_only(n_blocks))
    return pl.pallas_call(
        functools.partial(_ssm_scan_kernel, rows_per_step=r),
        grid=(n_blocks, N_CHUNKS),
        in_specs=[
            pl.BlockSpec((tm, D_MODEL), lambda i, j: (i, 0)),
            pl.BlockSpec((None, 1, D_MODEL), lambda i, j: (layer, 0, 0)),
            pl.BlockSpec((None, D_MODEL, CHUNK), lambda i, j: (layer, 0, j)),
            pl.BlockSpec((None, D_MODEL, CHUNK), lambda i, j: (layer, 0, N_CHUNKS + j)),
            per_chunk(CHUNK, 2 * sh), per_chunk(sh, CHUNK), per_chunk(sh, CHUNK),
            per_chunk(2, sh), per_chunk(2, sh),
            pl.BlockSpec((None, 1, CHUNK), lambda i, j: (layer, 0, j)),
            state_in, state_in,
        ],
        out_specs=[act_spec, act_spec, state_out, state_out],
        out_shape=[
            jax.ShapeDtypeStruct((m, D_INNER), _BF16),
            jax.ShapeDtypeStruct((m, D_INNER), _BF16),
            jax.ShapeDtypeStruct((r, N_STATES), _F32),
            jax.ShapeDtypeStruct((r, N_STATES), _F32),
        ],
        scratch_shapes=[
            pltpu.VMEM((tm, D_MODEL), _BF16),
            pltpu.VMEM((2 * tm, sh), _F32),
            pltpu.VMEM((2 * tm, sh), _F32),
            pltpu.VMEM((N_CHUNKS, 2 * r, sh), _F32),
            pltpu.VMEM((N_CHUNKS, 2 * r, sh), _F32),
        ],
        compiler_params=_compiler_params(),
        name="ssm_scan",
    )(x, norm_g, w_in, w_in, bd, cd_re, cd_im, lam_re, lam_im, d_skip, s0_re, s0_im)


def _glu_kernel(y_ref, yj_ref, zs_ref, wg_ref, bg_ref, o_ref):
    gate = jax.nn.sigmoid(_wdot(y_ref[...], wg_ref) + bg_ref[...])
    o_ref[...] = (yj_ref[...].astype(_F32) * gate * zs_ref[...].astype(_F32)).astype(_BF16)


def _glu(y, zs, w_glu, b_glu, layer, *, tm):
    m = y.shape[0]
    act_spec = pl.BlockSpec((tm, CHUNK), lambda i, j: (i, j))
    return pl.pallas_call(
        _glu_kernel,
        grid=(m // tm, N_CHUNKS),
        in_specs=[
            pl.BlockSpec((tm, D_INNER), lambda i, j: (i, 0)),
            act_spec, act_spec,
            pl.BlockSpec((None, D_INNER, CHUNK), lambda i, j: (layer, 0, j)),
            pl.BlockSpec((None, 1, CHUNK), lambda i, j: (layer, 0, j)),
        ],
        out_specs=act_spec,
        out_shape=jax.ShapeDtypeStruct((m, D_INNER), _BF16),
        compiler_params=_compiler_params(),
        name="glu",
    )(y, y, zs, w_glu, b_glu)


def _final_norm_kernel(x_ref, g_ref, o_ref):
    o_ref[...] = _rmsnorm(x_ref[...], g_ref[...])


def _final_norm(x, g, *, tm):
    m = x.shape[0]
    row_spec = pl.BlockSpec((tm, D_MODEL), lambda i: (i, 0))
    return pl.pallas_call(
        _final_norm_kernel,
        grid=(m // tm,),
        in_specs=[row_spec, pl.BlockSpec((1, D_MODEL), lambda i: (0, 0))],
        out_specs=row_spec,
        out_shape=jax.ShapeDtypeStruct((m, D_MODEL), _F32),
        compiler_params=_compiler_params(1),
        name="final_norm",
    )(x, g)


def kernel(x_prompt, x_sample, state_conv, state_ssm_re, state_ssm_im,
           conv_norm, conv_w_in, conv_w, conv_w_out,
           ssm_norm, ssm_w_in, ssm_a_re, ssm_a_im, ssm_log_dt, ssm_b_re, ssm_b_im,
           ssm_c_re, ssm_c_im, ssm_d, ssm_w_glu, ssm_b_glu, ssm_w_out, final_norm):
    batch, seq, _ = x_prompt.shape
    dec_batch = x_sample.shape[0]
    n_conv = conv_w_in.shape[0]
    n_ssm = ssm_w_in.shape[0]
    depth = n_conv + n_ssm
    tm_wide = 1024
    tm_scan = 512

    xp = x_prompt.transpose(1, 0, 2).reshape(seq * batch, D_MODEL)
    xs = x_sample.reshape(dec_batch, D_MODEL)
    conv_norm3 = conv_norm.reshape(n_conv, 1, D_MODEL)
    ssm_norm3 = ssm_norm.reshape(n_ssm, 1, D_MODEL)
    ssm_d3 = ssm_d.reshape(n_ssm, 1, D_INNER)
    ssm_b_glu3 = ssm_b_glu.reshape(n_ssm, 1, D_INNER)

    conv_p, conv_s, re_p, im_p, re_s, im_s = [], [], [], [], [], []
    for layer in range(depth):
        l = layer // 2
        if layer % 2 == 0:
            zero_state = jnp.zeros((batch, 2 * D_INNER), _F32)
            sample_state = state_conv[l].reshape(dec_batch, 2 * D_INNER)
            y, ns_p = _conv_in(xp, conv_norm3, conv_w_in, conv_w, zero_state, l,
                               rows_per_step=batch, tm=tm_wide)
            xp = _out_proj(y, conv_w_out, xp, l, tm=tm_wide)
            y, ns_s = _conv_in(xs, conv_norm3, conv_w_in, conv_w, sample_state, l,
                               rows_per_step=dec_batch, tm=dec_batch)
            xs = _out_proj(y, conv_w_out, xs, l, tm=dec_batch)
            conv_p.append(ns_p.reshape(batch, CONV_WIDTH - 1, D_INNER))
            conv_s.append(ns_s.reshape(dec_batch, CONV_WIDTH - 1, D_INNER))
        else:
            lam_re, lam_im, bb_re, bb_im = _ssm_prep(
                ssm_a_re[l], ssm_a_im[l], ssm_log_dt[l], ssm_b_re[l], ssm_b_im[l])
            bd = jnp.concatenate([_block_diag_in(bb_re), _block_diag_in(bb_im)],
                                 axis=-1).astype(_BF16)
            cd_re = _block_diag_out(ssm_c_re[l]).astype(_BF16)
            cd_im = _block_diag_out(ssm_c_im[l]).astype(_BF16)
            lam_re = lam_re.reshape(N_CHUNKS, 2, STATES_PER_HALF)
            lam_im = lam_im.reshape(N_CHUNKS, 2, STATES_PER_HALF)
            zero_state = jnp.zeros((batch, N_STATES), _F32)
            y, zs, hr_p, hi_p = _ssm_scan(
                xp, ssm_norm3, ssm_w_in, bd, cd_re, cd_im, lam_re, lam_im, ssm_d3,
                zero_state, zero_state, l, rows_per_step=batch, tm=tm_scan)
            yy = _glu(y, zs, ssm_w_glu, ssm_b_glu3, l, tm=tm_wide)
            xp = _out_proj(yy, ssm_w_out, xp, l, tm=tm_wide)
            y, zs, hr_s, hi_s = _ssm_scan(
                xs, ssm_norm3, ssm_w_in, bd, cd_re, cd_im, lam_re, lam_im, ssm_d3,
                state_ssm_re[l].reshape(dec_batch, N_STATES),
                state_ssm_im[l].reshape(dec_batch, N_STATES),
                l, rows_per_step=dec_batch, tm=dec_batch)
            yy = _glu(y, zs, ssm_w_glu, ssm_b_glu3, l, tm=dec_batch)
            xs = _out_proj(yy, ssm_w_out, xs, l, tm=dec_batch)
            re_p.append(hr_p.reshape(batch, N_GROUPS, STATE_DIM))
            im_p.append(hi_p.reshape(batch, N_GROUPS, STATE_DIM))
            re_s.append(hr_s.reshape(dec_batch, N_GROUPS, STATE_DIM))
            im_s.append(hi_s.reshape(dec_batch, N_GROUPS, STATE_DIM))

    final_g = final_norm.reshape(1, D_MODEL)
    xp = _final_norm(xp, final_g, tm=tm_wide)
    xs = _final_norm(xs, final_g, tm=dec_batch)
    y_prompt = xp.reshape(seq, batch, D_MODEL).transpose(1, 0, 2)
    y_sample = xs.reshape(dec_batch, 1, D_MODEL)
    return (y_prompt, y_sample, jnp.stack(conv_p), jnp.stack(conv_s),
            jnp.stack(re_p), jnp.stack(im_p), jnp.stack(re_s), jnp.stack(im_s))
```

```python
import functools

import jax
import jax.numpy as jnp
from jax import lax
from jax.experimental import pallas as pl
from jax.experimental.pallas import tpu as pltpu

D_MODEL = 2048
D_INNER = 2 * D_MODEL
CONV_WIDTH = 3
GROUP_SIZE = 16
N_GROUPS = D_INNER // GROUP_SIZE
STATE_DIM = 64
N_STATES = N_GROUPS * STATE_DIM
EPS = 1e-6

LANES = 128
SUBLANES = 8
CHUNK = 2 * LANES
N_CHUNKS = D_INNER // CHUNK
N_OUT_CHUNKS = D_MODEL // CHUNK
GROUPS_PER_HALF = LANES // GROUP_SIZE
STATES_PER_HALF = GROUPS_PER_HALF * STATE_DIM
STATES_PER_CHUNK = 2 * STATES_PER_HALF
VMEM_LIMIT_BYTES = 60000 * 1024

TM_ROWWISE = 512
TM_CONV = 1024
TM_SCAN = 512
WIDE = 1024

_F32 = jnp.float32
_BF16 = jnp.bfloat16


def _dot(a, b):
    return jnp.dot(a, b, preferred_element_type=_F32)


def _rmsnorm(x, g):
    return x * lax.rsqrt(jnp.mean(x * x, axis=-1, keepdims=True) + EPS) * g


def _silu(z):
    return z * jax.nn.sigmoid(z)


def _compiler_params(n_axes=2):
    return pltpu.CompilerParams(
        dimension_semantics=("arbitrary",) * n_axes,
        vmem_limit_bytes=VMEM_LIMIT_BYTES)


def _last_block_cols(n_blocks, offset=0):
    return lambda i, j: (0, offset + jnp.where(i == n_blocks - 1, j, 0))


def _conv_in_kernel(x_ref, xs_ref, g_ref, wb_ref, wc_ref, wv_ref, wz_ref, cw_ref,
                    s2s_ref, s1s_ref,
                    y_ref, n2_ref, n1_ref, ys_ref, n2s_ref, n1s_ref,
                    h_scr, hs_scr, ubuf, carry, *, rows_per_step):
    i = pl.program_id(0)
    j = pl.program_id(1)
    last = i == pl.num_programs(0) - 1
    tm = x_ref.shape[0]
    r = rows_per_step

    @pl.when(j == 0)
    def _():
        h_scr[...] = _rmsnorm(x_ref[...], g_ref[...]).astype(_BF16)

    @pl.when(i == 0)
    def _():
        carry[j] = jnp.zeros(carry.shape[1:], _F32)

    w16 = [w[...].astype(_BF16) for w in (wb_ref, wc_ref, wv_ref, wz_ref)]
    cw = cw_ref[...]

    h = h_scr[...]
    b, c, v, z = [_dot(h, w) for w in w16]
    ubuf[0:2 * r, :] = carry[j]
    ubuf[2 * r:2 * r + tm, :] = c * v
    conv = (cw[0:1, :] * ubuf[0:tm, :] + cw[1:2, :] * ubuf[r:r + tm, :]
            + cw[2:3, :] * ubuf[2 * r:2 * r + tm, :])
    y_ref[...] = (b * conv * _silu(z)).astype(_BF16)
    carry[j] = ubuf[tm:tm + 2 * r, :]

    @pl.when(last)
    def _():
        n2_ref[...] = ubuf[tm:tm + r, :]
        n1_ref[...] = ubuf[tm + r:tm + 2 * r, :]

        @pl.when(j == 0)
        def _():
            hs_scr[...] = _rmsnorm(xs_ref[...], g_ref[...]).astype(_BF16)

        hs = hs_scr[...]
        bs, cs, vs, zs = [_dot(hs, w) for w in w16]
        us = cs * vs
        s1s = s1s_ref[...]
        convs = cw[0:1, :] * s2s_ref[...] + cw[1:2, :] * s1s + cw[2:3, :] * us
        ys_ref[...] = (bs * convs * _silu(zs)).astype(_BF16)
        n2s_ref[...] = s1s
        n1s_ref[...] = us


def _conv_in(x, xs, norm_g, w_in, conv_w, state_s, layer, *, rows_per_step):
    m = x.shape[0]
    rs = xs.shape[0]
    r = rows_per_step
    tm = TM_CONV
    n_blocks = m // tm
    w_in_spec = lambda part: pl.BlockSpec(
        (None, D_MODEL, CHUNK), lambda i, j: (layer, 0, part * N_CHUNKS + j))
    live_last = lambda rows, offset=0: pl.BlockSpec(
        (rows, CHUNK), _last_block_cols(n_blocks, offset))
    y, n2, n1, ys, n2s, n1s = pl.pallas_call(
        functools.partial(_conv_in_kernel, rows_per_step=r),
        grid=(n_blocks, N_CHUNKS),
        in_specs=[
            pl.BlockSpec((tm, D_MODEL), lambda i, j: (i, 0)),
            pl.BlockSpec((rs, D_MODEL), lambda i, j: (0, 0)),
            pl.BlockSpec((None, 1, D_MODEL), lambda i, j: (layer, 0, 0)),
            w_in_spec(0), w_in_spec(1), w_in_spec(2), w_in_spec(3),
            pl.BlockSpec((None, CONV_WIDTH, CHUNK), lambda i, j: (layer, 0, j)),
            live_last(rs), live_last(rs, N_CHUNKS),
        ],
        out_specs=[
            pl.BlockSpec((tm, CHUNK), lambda i, j: (i, j)),
            live_last(r), live_last(r),
            live_last(rs), live_last(rs), live_last(rs),
        ],
        out_shape=[
            jax.ShapeDtypeStruct((m, D_INNER), _BF16),
            jax.ShapeDtypeStruct((r, D_INNER), _F32),
            jax.ShapeDtypeStruct((r, D_INNER), _F32),
            jax.ShapeDtypeStruct((rs, D_INNER), _BF16),
            jax.ShapeDtypeStruct((rs, D_INNER), _F32),
            jax.ShapeDtypeStruct((rs, D_INNER), _F32),
        ],
        scratch_shapes=[
            pltpu.VMEM((tm, D_MODEL), _BF16),
            pltpu.VMEM((rs, D_MODEL), _BF16),
            pltpu.VMEM((tm + 2 * r, CHUNK), _F32),
            pltpu.VMEM((N_CHUNKS, 2 * r, CHUNK), _F32),
        ],
        compiler_params=_compiler_params(),
        name="conv_in",
    )(x, xs, norm_g, w_in, w_in, w_in, w_in, conv_w, state_s, state_s)
    return y, ys, jnp.stack([n2, n1], axis=1), jnp.stack([n2s, n1s], axis=1)


def _out_proj_kernel(y_ref, ys_ref, w_ref, x_ref, xs_ref, o_ref, os_ref, w16):
    i = pl.program_id(1)

    @pl.when(i == 0)
    def _():
        w16[...] = w_ref[...].astype(_BF16)

    o_ref[...] = x_ref[...] + _dot(y_ref[...], w16[...])

    @pl.when(i == pl.num_programs(1) - 1)
    def _():
        os_ref[...] = xs_ref[...] + _dot(ys_ref[...], w16[...])


def _stationary_weight_spec(k_dim, layer):
    return pl.BlockSpec((None, k_dim, WIDE), lambda n, i: (layer, 0, n),
                        pipeline_mode=pl.Buffered(1))


def _out_proj(y, ys, w_out, x, xs, layer):
    m = x.shape[0]
    rs = xs.shape[0]
    tm = TM_ROWWISE
    rows = lambda width: pl.BlockSpec((tm, width), lambda n, i: (i, n))
    sample_rows = lambda width: pl.BlockSpec((rs, width), lambda n, i: (0, n))
    return pl.pallas_call(
        _out_proj_kernel,
        grid=(D_MODEL // WIDE, m // tm),
        in_specs=[
            pl.BlockSpec((tm, D_INNER), lambda n, i: (i, 0)),
            pl.BlockSpec((rs, D_INNER), lambda n, i: (0, 0)),
            _stationary_weight_spec(D_INNER, layer),
            rows(WIDE), sample_rows(WIDE),
        ],
        out_specs=[rows(WIDE), sample_rows(WIDE)],
        out_shape=[
            jax.ShapeDtypeStruct((m, D_MODEL), _F32),
            jax.ShapeDtypeStruct((rs, D_MODEL), _F32),
        ],
        scratch_shapes=[pltpu.VMEM((D_INNER, WIDE), _BF16)],
        compiler_params=_compiler_params(),
        name="out_proj",
    )(y, ys, w_out, x, xs)


def _ssm_prep_kernel(are_ref, aim_ref, logdt_ref, btre_ref, btim_ref,
                     lre_ref, lim_ref, bbre_ref, bbim_ref):
    a_re = are_ref[...]
    a_im = aim_ref[...]
    dt = jnp.exp(logdt_ref[...])
    mag = jnp.exp(a_re * dt)
    l_re = mag * jnp.cos(a_im * dt)
    l_im = mag * jnp.sin(a_im * dt)
    lre_ref[...] = l_re
    lim_ref[...] = l_im
    n_re = l_re - 1.0
    den = a_re * a_re + a_im * a_im
    k_re = (n_re * a_re + l_im * a_im) / den
    k_im = (l_im * a_re - n_re * a_im) / den
    g = a_re.shape[0]
    expand = lambda k: jnp.broadcast_to(
        k[:, None, :], (g, GROUP_SIZE, STATE_DIM)).reshape(g * GROUP_SIZE, STATE_DIM)
    k_re = expand(k_re)
    k_im = expand(k_im)
    bt_re = btre_ref[...]
    bt_im = btim_ref[...]
    bbre_ref[...] = k_re * bt_re - k_im * bt_im
    bbim_ref[...] = k_re * bt_im + k_im * bt_re


def _ssm_prep(a_re, a_im, log_dt, b_re, b_im):
    gb = SUBLANES
    bt_re = b_re.transpose(0, 2, 1).reshape(N_GROUPS * GROUP_SIZE, STATE_DIM)
    bt_im = b_im.transpose(0, 2, 1).reshape(N_GROUPS * GROUP_SIZE, STATE_DIM)
    gp_spec = pl.BlockSpec((gb, STATE_DIM), lambda i: (i, 0))
    ghp_spec = pl.BlockSpec((gb * GROUP_SIZE, STATE_DIM), lambda i: (i, 0))
    return pl.pallas_call(
        _ssm_prep_kernel,
        grid=(N_GROUPS // gb,),
        in_specs=[gp_spec, gp_spec, pl.BlockSpec((gb, 1), lambda i: (i, 0)), ghp_spec, ghp_spec],
        out_specs=[gp_spec, gp_spec, ghp_spec, ghp_spec],
        out_shape=[
            jax.ShapeDtypeStruct((N_GROUPS, STATE_DIM), _F32),
            jax.ShapeDtypeStruct((N_GROUPS, STATE_DIM), _F32),
            jax.ShapeDtypeStruct((N_GROUPS * GROUP_SIZE, STATE_DIM), _F32),
            jax.ShapeDtypeStruct((N_GROUPS * GROUP_SIZE, STATE_DIM), _F32),
        ],
        name="ssm_prep",
    )(a_re, a_im, log_dt.reshape(N_GROUPS, 1), bt_re, bt_im)


def _group_eye():
    return jnp.eye(GROUPS_PER_HALF, dtype=bool)


def _block_diag_in(bb):
    v = bb.reshape(N_CHUNKS, 2, GROUPS_PER_HALF, GROUP_SIZE, 1, STATE_DIM)
    eye = _group_eye()[None, None, :, None, :, None]
    return jnp.where(eye, v, 0.0).reshape(N_CHUNKS, CHUNK, STATES_PER_HALF)


def _block_diag_out(c):
    v = c.reshape(N_CHUNKS, 2, GROUPS_PER_HALF, GROUP_SIZE, STATE_DIM)
    v = v.transpose(0, 4, 1, 2, 3)[:, None]
    eye = _group_eye()[None, :, None, None, :, None]
    return jnp.where(eye, v, 0.0).reshape(N_CHUNKS, STATES_PER_HALF, CHUNK)


def _ssm_scan_kernel(x_ref, xs_ref, g_ref, wu_ref, wz_ref, bd_ref, cre_ref, cim_ref,
                     lre_ref, lim_ref, d_ref, s0re_ref, s0im_ref,
                     y_ref, zs_ref, nre_ref, nim_ref, ys_ref, zss_ref, nres_ref, nims_ref,
                     h_scr, hs_scr, xre, xim, carry_re, carry_im, *, rows_per_step):
    i = pl.program_id(0)
    j = pl.program_id(1)
    last = i == pl.num_programs(0) - 1
    tm = x_ref.shape[0]
    r = rows_per_step
    rs = xs_ref.shape[0]
    sh = STATES_PER_HALF
    assert 2 * r == SUBLANES

    @pl.when(j == 0)
    def _():
        h_scr[...] = _rmsnorm(x_ref[...], g_ref[...]).astype(_BF16)

    @pl.when(i == 0)
    def _():
        zeros = jnp.zeros((2 * r, sh), _F32)
        carry_re[j] = zeros
        carry_im[j] = zeros

    wu16 = wu_ref[...].astype(_BF16)
    wz16 = wz_ref[...].astype(_BF16)
    bd = bd_ref[...]
    c_re_mat = cre_ref[...]
    c_im_mat = cim_ref[...]
    d_skip = d_ref[...]

    def lam_rows(rows_per_half):
        first_half = lax.broadcasted_iota(jnp.int32, (2 * rows_per_half, sh), 0) < rows_per_half
        return (jnp.where(first_half, lre_ref[0:1, :], lre_ref[1:2, :]),
                jnp.where(first_half, lim_ref[0:1, :], lim_ref[1:2, :]))

    def readout(x_re, x_im):
        return _dot(x_re.astype(_BF16), c_re_mat) - _dot(x_im.astype(_BF16), c_im_mat)

    h = h_scr[...]
    u = _dot(h, wu16)
    z = _dot(h, wz16)

    first_lanes = lax.broadcasted_iota(jnp.int32, (tm, CHUNK), 1) < LANES
    low_rows = (lax.broadcasted_iota(jnp.int32, (tm, CHUNK), 0) & r) == 0
    u_first = jnp.where(first_lanes, u, 0.0)
    u_second = jnp.where(first_lanes, 0.0, u)
    u_down = pltpu.roll(u, r, 0)
    u_up = pltpu.roll(u, tm - r, 0)
    even = jnp.where(low_rows, u_first, jnp.where(first_lanes, 0.0, u_down))
    odd = jnp.where(low_rows, jnp.where(first_lanes, u_up, 0.0), u_second)
    lhs = jnp.concatenate([even, odd], axis=0)
    bu = _dot(lhs.astype(_BF16), bd)
    xre[...] = bu[:, :sh]
    xim[...] = bu[:, sh:]

    l_re, l_im = lam_rows(r)

    def step(rows, p_re, p_im):
        n_re = l_re * p_re - l_im * p_im + xre[rows, :]
        n_im = l_re * p_im + l_im * p_re + xim[rows, :]
        xre[rows, :] = n_re
        xim[rows, :] = n_im
        return n_re, n_im

    def pair(k, c):
        base = pl.multiple_of(k * SUBLANES, SUBLANES)
        c = step(pl.ds(base, SUBLANES), *c)
        return step(pl.ds(tm + base, SUBLANES), *c)

    c_re, c_im = lax.fori_loop(0, tm // SUBLANES, pair, (carry_re[j], carry_im[j]), unroll=4)
    carry_re[j] = c_re
    carry_im[j] = c_im

    yy = readout(xre[...], xim[...])
    low_rows = (lax.broadcasted_iota(jnp.int32, (tm, LANES), 0) & r) == 0
    y_even = yy[:tm]
    y_odd = yy[tm:]
    y_first = jnp.where(low_rows, y_even[:, :LANES], pltpu.roll(y_odd[:, :LANES], r, 0))
    y_second = jnp.where(low_rows, pltpu.roll(y_even[:, LANES:], tm - r, 0), y_odd[:, LANES:])
    y = jnp.concatenate([y_first, y_second], axis=1)
    y_ref[...] = jax.nn.gelu(y + d_skip * u).astype(_BF16)
    zs_ref[...] = _silu(z).astype(_BF16)

    @pl.when(last)
    def _():
        nre_ref[:, 0:sh] = c_re[0:r]
        nre_ref[:, sh:2 * sh] = c_re[r:2 * r]
        nim_ref[:, 0:sh] = c_im[0:r]
        nim_ref[:, sh:2 * sh] = c_im[r:2 * r]

        @pl.when(j == 0)
        def _():
            hs_scr[...] = _rmsnorm(xs_ref[...], g_ref[...]).astype(_BF16)

        hs = hs_scr[...]
        us = _dot(hs, wu16)
        zz = _dot(hs, wz16)
        first = lax.broadcasted_iota(jnp.int32, (rs, CHUNK), 1) < LANES
        lhs_s = jnp.concatenate([jnp.where(first, us, 0.0), jnp.where(first, 0.0, us)], axis=0)
        bus = _dot(lhs_s.astype(_BF16), bd)
        p_re = jnp.concatenate([s0re_ref[:, 0:sh], s0re_ref[:, sh:2 * sh]], axis=0)
        p_im = jnp.concatenate([s0im_ref[:, 0:sh], s0im_ref[:, sh:2 * sh]], axis=0)
        ls_re, ls_im = lam_rows(rs)
        n_re = ls_re * p_re - ls_im * p_im + bus[:, :sh]
        n_im = ls_re * p_im + ls_im * p_re + bus[:, sh:]
        nres_ref[:, 0:sh] = n_re[0:rs]
        nres_ref[:, sh:2 * sh] = n_re[rs:2 * rs]
        nims_ref[:, 0:sh] = n_im[0:rs]
        nims_ref[:, sh:2 * sh] = n_im[rs:2 * rs]
        yys = readout(n_re, n_im)
        y_s = jnp.concatenate([yys[:rs, :LANES], yys[rs:, LANES:]], axis=1)
        ys_ref[...] = jax.nn.gelu(y_s + d_skip * us).astype(_BF16)
        zss_ref[...] = _silu(zz).astype(_BF16)


def _ssm_scan(x, xs, norm_g, w_in, bd, cd_re, cd_im, lam_re, lam_im, d_skip, s0_re, s0_im, layer,
              *, rows_per_step):
    m = x.shape[0]
    rs = xs.shape[0]
    r = rows_per_step
    tm = TM_SCAN
    n_blocks = m // tm
    sh = STATES_PER_HALF
    per_chunk = lambda a, b: pl.BlockSpec((None, a, b), lambda i, j: (j, 0, 0))
    act_spec = pl.BlockSpec((tm, CHUNK), lambda i, j: (i, j))
    live_last = lambda rows, cols: pl.BlockSpec((rows, cols), _last_block_cols(n_blocks))
    return pl.pallas_call(
        functools.partial(_ssm_scan_kernel, rows_per_step=r),
        grid=(n_blocks, N_CHUNKS),
        in_specs=[
            pl.BlockSpec((tm, D_MODEL), lambda i, j: (i, 0)),
            pl.BlockSpec((rs, D_MODEL), lambda i, j: (0, 0)),
            pl.BlockSpec((None, 1, D_MODEL), lambda i, j: (layer, 0, 0)),
            pl.BlockSpec((None, D_MODEL, CHUNK), lambda i, j: (layer, 0, j)),
            pl.BlockSpec((None, D_MODEL, CHUNK), lambda i, j: (layer, 0, N_CHUNKS + j)),
            per_chunk(CHUNK, 2 * sh), per_chunk(sh, CHUNK), per_chunk(sh, CHUNK),
            per_chunk(2, sh), per_chunk(2, sh),
            pl.BlockSpec((None, 1, CHUNK), lambda i, j: (layer, 0, j)),
            live_last(rs, STATES_PER_CHUNK), live_last(rs, STATES_PER_CHUNK),
        ],
        out_specs=[
            act_spec, act_spec,
            live_last(r, STATES_PER_CHUNK), live_last(r, STATES_PER_CHUNK),
            live_last(rs, CHUNK), live_last(rs, CHUNK),
            live_last(rs, STATES_PER_CHUNK), live_last(rs, STATES_PER_CHUNK),
        ],
        out_shape=[
            jax.ShapeDtypeStruct((m, D_INNER), _BF16),
            jax.ShapeDtypeStruct((m, D_INNER), _BF16),
            jax.ShapeDtypeStruct((r, N_STATES), _F32),
            jax.ShapeDtypeStruct((r, N_STATES), _F32),
            jax.ShapeDtypeStruct((rs, D_INNER), _BF16),
            jax.ShapeDtypeStruct((rs, D_INNER), _BF16),
            jax.ShapeDtypeStruct((rs, N_STATES), _F32),
            jax.ShapeDtypeStruct((rs, N_STATES), _F32),
        ],
        scratch_shapes=[
            pltpu.VMEM((tm, D_MODEL), _BF16),
            pltpu.VMEM((rs, D_MODEL), _BF16),
            pltpu.VMEM((2 * tm, sh), _F32),
            pltpu.VMEM((2 * tm, sh), _F32),
            pltpu.VMEM((N_CHUNKS, 2 * r, sh), _F32),
            pltpu.VMEM((N_CHUNKS, 2 * r, sh), _F32),
        ],
        compiler_params=_compiler_params(),
        name="ssm_scan",
    )(x, xs, norm_g, w_in, w_in, bd, cd_re, cd_im, lam_re, lam_im, d_skip, s0_re, s0_im)


def _glu_kernel(y_ref, yj_ref, zs_ref, ys_ref, yjs_ref, zss_ref, wg_ref, bg_ref, o_ref, os_ref,
                w16):
    i = pl.program_id(1)

    @pl.when(i == 0)
    def _():
        w16[...] = wg_ref[...].astype(_BF16)

    bias = bg_ref[...]

    def gated(y_all, y_chunk, z_chunk):
        gate = jax.nn.sigmoid(_dot(y_all, w16[...]) + bias)
        return (y_chunk.astype(_F32) * gate * z_chunk.astype(_F32)).astype(_BF16)

    o_ref[...] = gated(y_ref[...], yj_ref[...], zs_ref[...])

    @pl.when(i == pl.num_programs(1) - 1)
    def _():
        os_ref[...] = gated(ys_ref[...], yjs_ref[...], zss_ref[...])


def _glu(y, zs, ys, zss, w_glu, b_glu, layer):
    m = y.shape[0]
    rs = ys.shape[0]
    tm = TM_ROWWISE
    rows = pl.BlockSpec((tm, WIDE), lambda n, i: (i, n))
    sample_rows = pl.BlockSpec((rs, WIDE), lambda n, i: (0, n))
    return pl.pallas_call(
        _glu_kernel,
        grid=(D_INNER // WIDE, m // tm),
        in_specs=[
            pl.BlockSpec((tm, D_INNER), lambda n, i: (i, 0)),
            rows, rows,
            pl.BlockSpec((rs, D_INNER), lambda n, i: (0, 0)),
            sample_rows, sample_rows,
            _stationary_weight_spec(D_INNER, layer),
            pl.BlockSpec((None, 1, WIDE), lambda n, i: (layer, 0, n)),
        ],
        out_specs=[rows, sample_rows],
        out_shape=[
            jax.ShapeDtypeStruct((m, D_INNER), _BF16),
            jax.ShapeDtypeStruct((rs, D_INNER), _BF16),
        ],
        scratch_shapes=[pltpu.VMEM((D_INNER, WIDE), _BF16)],
        compiler_params=_compiler_params(),
        name="glu",
    )(y, y, zs, ys, ys, zss, w_glu, b_glu)


def _final_norm_kernel(x_ref, g_ref, o_ref):
    o_ref[...] = _rmsnorm(x_ref[...], g_ref[...])


def _final_norm(x, g):
    m = x.shape[0]
    tm = min(m, TM_CONV)
    row_spec = pl.BlockSpec((tm, D_MODEL), lambda i: (i, 0))
    return pl.pallas_call(
        _final_norm_kernel,
        grid=(m // tm,),
        in_specs=[row_spec, pl.BlockSpec((1, D_MODEL), lambda i: (0, 0))],
        out_specs=row_spec,
        out_shape=jax.ShapeDtypeStruct((m, D_MODEL), _F32),
        compiler_params=_compiler_params(1),
        name="final_norm",
    )(x, g)


def kernel(x_prompt, x_sample, state_conv, state_ssm_re, state_ssm_im,
           conv_norm, conv_w_in, conv_w, conv_w_out,
           ssm_norm, ssm_w_in, ssm_a_re, ssm_a_im, ssm_log_dt, ssm_b_re, ssm_b_im,
           ssm_c_re, ssm_c_im, ssm_d, ssm_w_glu, ssm_b_glu, ssm_w_out, final_norm):
    batch, seq, _ = x_prompt.shape
    dec_batch = x_sample.shape[0]
    n_conv = conv_w_in.shape[0]
    n_ssm = ssm_w_in.shape[0]
    depth = n_conv + n_ssm

    xp = x_prompt.transpose(1, 0, 2).reshape(seq * batch, D_MODEL)
    xs = x_sample.reshape(dec_batch, D_MODEL)
    conv_norm3 = conv_norm.reshape(n_conv, 1, D_MODEL)
    ssm_norm3 = ssm_norm.reshape(n_ssm, 1, D_MODEL)
    ssm_d3 = ssm_d.reshape(n_ssm, 1, D_INNER)
    ssm_b_glu3 = ssm_b_glu.reshape(n_ssm, 1, D_INNER)

    conv_p, conv_s, re_p, im_p, re_s, im_s = [], [], [], [], [], []
    for layer in range(depth):
        l = layer // 2
        if layer % 2 == 0:
            y, ys, ns_p, ns_s = _conv_in(
                xp, xs, conv_norm3, conv_w_in, conv_w,
                state_conv[l].reshape(dec_batch, (CONV_WIDTH - 1) * D_INNER), l,
                rows_per_step=batch)
            xp, xs = _out_proj(y, ys, conv_w_out, xp, xs, l)
            conv_p.append(ns_p)
            conv_s.append(ns_s)
        else:
            lam_re, lam_im, bb_re, bb_im = _ssm_prep(
                ssm_a_re[l], ssm_a_im[l], ssm_log_dt[l], ssm_b_re[l], ssm_b_im[l])
            bd = jnp.concatenate([_block_diag_in(bb_re), _block_diag_in(bb_im)],
                                 axis=-1).astype(_BF16)
            cd_re = _block_diag_out(ssm_c_re[l]).astype(_BF16)
            cd_im = _block_diag_out(ssm_c_im[l]).astype(_BF16)
            lam_re = lam_re.reshape(N_CHUNKS, 2, STATES_PER_HALF)
            lam_im = lam_im.reshape(N_CHUNKS, 2, STATES_PER_HALF)
            y, zs, hr_p, hi_p, ys, zss, hr_s, hi_s = _ssm_scan(
                xp, xs, ssm_norm3, ssm_w_in, bd, cd_re, cd_im, lam_re, lam_im, ssm_d3,
                state_ssm_re[l].reshape(dec_batch, N_STATES),
                state_ssm_im[l].reshape(dec_batch, N_STATES),
                l, rows_per_step=batch)
            yy, yys = _glu(y, zs, ys, zss, ssm_w_glu, ssm_b_glu3, l)
            xp, xs = _out_proj(yy, yys, ssm_w_out, xp, xs, l)
            re_p.append(hr_p.reshape(batch, N_GROUPS, STATE_DIM))
            im_p.append(hi_p.reshape(batch, N_GROUPS, STATE_DIM))
            re_s.append(hr_s.reshape(dec_batch, N_GROUPS, STATE_DIM))
            im_s.append(hi_s.reshape(dec_batch, N_GROUPS, STATE_DIM))

    final_g = final_norm.reshape(1, D_MODEL)
    xp = _final_norm(xp, final_g)
    xs = _final_norm(xs, final_g)
    y_prompt = xp.reshape(seq, batch, D_MODEL).transpose(1, 0, 2)
    y_sample = xs.reshape(dec_batch, 1, D_MODEL)
    return (y_prompt, y_sample, jnp.stack(conv_p), jnp.stack(conv_s),
            jnp.stack(re_p), jnp.stack(im_p), jnp.stack(re_s), jnp.stack(im_s))
```

```python
import functools

import jax
import jax.numpy as jnp
from jax import lax
from jax.experimental import pallas as pl
from jax.experimental.pallas import tpu as pltpu

D_MODEL = 2048
D_INNER = 2 * D_MODEL
CONV_WIDTH = 3
GROUP_SIZE = 16
N_GROUPS = D_INNER // GROUP_SIZE
STATE_DIM = 64
N_STATES = N_GROUPS * STATE_DIM
EPS = 1e-6

LANES = 128
SUBLANES = 8
CHUNK = 2 * LANES
N_CHUNKS = D_INNER // CHUNK
GROUPS_PER_HALF = LANES // GROUP_SIZE
STATES_PER_HALF = GROUPS_PER_HALF * STATE_DIM
STATES_PER_CHUNK = 2 * STATES_PER_HALF
VMEM_LIMIT_BYTES = 60000 * 1024

TM_ROWWISE = 512
TM_CONV = 1024
TM_SCAN = 512
WIDE = 1024

_F32 = jnp.float32
_BF16 = jnp.bfloat16


def _dot(a, b):
    return jnp.dot(a, b, preferred_element_type=_F32)


def _rmsnorm(x, g):
    return x * lax.rsqrt(jnp.mean(x * x, axis=-1, keepdims=True) + EPS) * g


def _silu(z):
    return z * jax.nn.sigmoid(z)


def _compiler_params(n_axes=2):
    return pltpu.CompilerParams(
        dimension_semantics=("arbitrary",) * n_axes,
        vmem_limit_bytes=VMEM_LIMIT_BYTES)


def _last_block_cols(n_blocks, offset=0):
    return lambda i, j: (0, offset + jnp.where(i == n_blocks - 1, j, 0))


def _conv_in_kernel(x_ref, xs_ref, g_ref, wb_ref, wc_ref, wv_ref, wz_ref, cw_ref,
                    s2s_ref, s1s_ref,
                    y_ref, n2_ref, n1_ref, ys_ref, n2s_ref, n1s_ref,
                    h_scr, hs_scr, ubuf, carry, *, rows_per_step):
    i = pl.program_id(0)
    j = pl.program_id(1)
    last = i == pl.num_programs(0) - 1
    tm = x_ref.shape[0]
    r = rows_per_step

    @pl.when(j == 0)
    def _():
        h_scr[...] = _rmsnorm(x_ref[...], g_ref[...]).astype(_BF16)

    @pl.when(i == 0)
    def _():
        carry[j] = jnp.zeros(carry.shape[1:], _F32)

    w16 = [w[...].astype(_BF16) for w in (wb_ref, wc_ref, wv_ref, wz_ref)]
    cw = cw_ref[...]

    h = h_scr[...]
    b, c, v, z = [_dot(h, w) for w in w16]
    ubuf[0:2 * r, :] = carry[j]
    ubuf[2 * r:2 * r + tm, :] = c * v
    conv = (cw[0:1, :] * ubuf[0:tm, :] + cw[1:2, :] * ubuf[r:r + tm, :]
            + cw[2:3, :] * ubuf[2 * r:2 * r + tm, :])
    y_ref[...] = (b * conv * _silu(z)).astype(_BF16)
    carry[j] = ubuf[tm:tm + 2 * r, :]

    @pl.when(last)
    def _():
        n2_ref[...] = ubuf[tm:tm + r, :]
        n1_ref[...] = ubuf[tm + r:tm + 2 * r, :]

        @pl.when(j == 0)
        def _():
            hs_scr[...] = _rmsnorm(xs_ref[...], g_ref[...]).astype(_BF16)

        hs = hs_scr[...]
        bs, cs, vs, zs = [_dot(hs, w) for w in w16]
        us = cs * vs
        s1s = s1s_ref[...]
        convs = cw[0:1, :] * s2s_ref[...] + cw[1:2, :] * s1s + cw[2:3, :] * us
        ys_ref[...] = (bs * convs * _silu(zs)).astype(_BF16)
        n2s_ref[...] = s1s
        n1s_ref[...] = us


def _conv_in(x, xs, norm_g, w_in, conv_w, state_s, layer, *, rows_per_step):
    m = x.shape[0]
    rs = xs.shape[0]
    r = rows_per_step
    tm = TM_CONV
    n_blocks = m // tm
    w_in_spec = lambda part: pl.BlockSpec(
        (None, D_MODEL, CHUNK), lambda i, j: (layer, 0, part * N_CHUNKS + j))
    live_last = lambda rows, offset=0: pl.BlockSpec(
        (rows, CHUNK), _last_block_cols(n_blocks, offset))
    y, n2, n1, ys, n2s, n1s = pl.pallas_call(
        functools.partial(_conv_in_kernel, rows_per_step=r),
        grid=(n_blocks, N_CHUNKS),
        in_specs=[
            pl.BlockSpec((tm, D_MODEL), lambda i, j: (i, 0)),
            pl.BlockSpec((rs, D_MODEL), lambda i, j: (0, 0)),
            pl.BlockSpec((None, 1, D_MODEL), lambda i, j: (layer, 0, 0)),
            w_in_spec(0), w_in_spec(1), w_in_spec(2), w_in_spec(3),
            pl.BlockSpec((None, CONV_WIDTH, CHUNK), lambda i, j: (layer, 0, j)),
            live_last(rs), live_last(rs, N_CHUNKS),
        ],
        out_specs=[
            pl.BlockSpec((tm, CHUNK), lambda i, j: (i, j)),
            live_last(r), live_last(r),
            live_last(rs), live_last(rs), live_last(rs),
        ],
        out_shape=[
            jax.ShapeDtypeStruct((m, D_INNER), _BF16),
            jax.ShapeDtypeStruct((r, D_INNER), _F32),
            jax.ShapeDtypeStruct((r, D_INNER), _F32),
            jax.ShapeDtypeStruct((rs, D_INNER), _BF16),
            jax.ShapeDtypeStruct((rs, D_INNER), _F32),
            jax.ShapeDtypeStruct((rs, D_INNER), _F32),
        ],
        scratch_shapes=[
            pltpu.VMEM((tm, D_MODEL), _BF16),
            pltpu.VMEM((rs, D_MODEL), _BF16),
            pltpu.VMEM((tm + 2 * r, CHUNK), _F32),
            pltpu.VMEM((N_CHUNKS, 2 * r, CHUNK), _F32),
        ],
        compiler_params=_compiler_params(),
        name="conv_in",
    )(x, xs, norm_g, w_in, w_in, w_in, w_in, conv_w, state_s, state_s)
    return y, ys, jnp.stack([n2, n1], axis=1), jnp.stack([n2s, n1s], axis=1)


def _out_proj_kernel(y_ref, ys_ref, w_ref, x_ref, xs_ref, o_ref, os_ref, w16):
    i = pl.program_id(1)

    @pl.when(i == 0)
    def _():
        w16[...] = w_ref[...].astype(_BF16)

    o_ref[...] = x_ref[...] + _dot(y_ref[...], w16[...])

    @pl.when(i == pl.num_programs(1) - 1)
    def _():
        os_ref[...] = xs_ref[...] + _dot(ys_ref[...], w16[...])


def _stationary_weight_spec(k_dim, layer):
    return pl.BlockSpec((None, k_dim, WIDE), lambda n, i: (layer, 0, n),
                        pipeline_mode=pl.Buffered(1))


def _out_proj(y, ys, w_out, x, xs, layer):
    m = x.shape[0]
    rs = xs.shape[0]
    tm = TM_ROWWISE
    rows = lambda width: pl.BlockSpec((tm, width), lambda n, i: (i, n))
    sample_rows = lambda width: pl.BlockSpec((rs, width), lambda n, i: (0, n))
    return pl.pallas_call(
        _out_proj_kernel,
        grid=(D_MODEL // WIDE, m // tm),
        in_specs=[
            pl.BlockSpec((tm, D_INNER), lambda n, i: (i, 0)),
            pl.BlockSpec((rs, D_INNER), lambda n, i: (0, 0)),
            _stationary_weight_spec(D_INNER, layer),
            rows(WIDE), sample_rows(WIDE),
        ],
        out_specs=[rows(WIDE), sample_rows(WIDE)],
        out_shape=[
            jax.ShapeDtypeStruct((m, D_MODEL), _F32),
            jax.ShapeDtypeStruct((rs, D_MODEL), _F32),
        ],
        scratch_shapes=[pltpu.VMEM((D_INNER, WIDE), _BF16)],
        compiler_params=_compiler_params(),
        name="out_proj",
    )(y, ys, w_out, x, xs)


def _ssm_prep_kernel(are_ref, aim_ref, logdt_ref, btre_ref, btim_ref,
                     lre_ref, lim_ref, bbre_ref, bbim_ref):
    a_re = are_ref[...]
    a_im = aim_ref[...]
    dt = jnp.exp(logdt_ref[...])
    mag = jnp.exp(a_re * dt)
    l_re = mag * jnp.cos(a_im * dt)
    l_im = mag * jnp.sin(a_im * dt)
    lre_ref[...] = l_re
    lim_ref[...] = l_im
    n_re = l_re - 1.0
    den = a_re * a_re + a_im * a_im
    k_re = (n_re * a_re + l_im * a_im) / den
    k_im = (l_im * a_re - n_re * a_im) / den
    g = a_re.shape[0]
    expand = lambda k: jnp.broadcast_to(
        k[:, None, :], (g, GROUP_SIZE, STATE_DIM)).reshape(g * GROUP_SIZE, STATE_DIM)
    k_re = expand(k_re)
    k_im = expand(k_im)
    bt_re = btre_ref[...]
    bt_im = btim_ref[...]
    bbre_ref[...] = k_re * bt_re - k_im * bt_im
    bbim_ref[...] = k_re * bt_im + k_im * bt_re


def _ssm_prep(a_re, a_im, log_dt, b_re, b_im):
    gb = SUBLANES
    bt_re = b_re.transpose(0, 2, 1).reshape(N_GROUPS * GROUP_SIZE, STATE_DIM)
    bt_im = b_im.transpose(0, 2, 1).reshape(N_GROUPS * GROUP_SIZE, STATE_DIM)
    gp_spec = pl.BlockSpec((gb, STATE_DIM), lambda i: (i, 0))
    ghp_spec = pl.BlockSpec((gb * GROUP_SIZE, STATE_DIM), lambda i: (i, 0))
    return pl.pallas_call(
        _ssm_prep_kernel,
        grid=(N_GROUPS // gb,),
        in_specs=[gp_spec, gp_spec, pl.BlockSpec((gb, 1), lambda i: (i, 0)), ghp_spec, ghp_spec],
        out_specs=[gp_spec, gp_spec, ghp_spec, ghp_spec],
        out_shape=[
            jax.ShapeDtypeStruct((N_GROUPS, STATE_DIM), _F32),
            jax.ShapeDtypeStruct((N_GROUPS, STATE_DIM), _F32),
            jax.ShapeDtypeStruct((N_GROUPS * GROUP_SIZE, STATE_DIM), _F32),
            jax.ShapeDtypeStruct((N_GROUPS * GROUP_SIZE, STATE_DIM), _F32),
        ],
        name="ssm_prep",
    )(a_re, a_im, log_dt.reshape(N_GROUPS, 1), bt_re, bt_im)


def _group_eye():
    return jnp.eye(GROUPS_PER_HALF, dtype=bool)


def _block_diag_in(bb):
    v = bb.reshape(N_CHUNKS, 2, GROUPS_PER_HALF, GROUP_SIZE, 1, STATE_DIM)
    eye = _group_eye()[None, None, :, None, :, None]
    return jnp.where(eye, v, 0.0).reshape(N_CHUNKS, CHUNK, STATES_PER_HALF)


def _block_diag_out(c):
    v = c.reshape(N_CHUNKS, 2, GROUPS_PER_HALF, GROUP_SIZE, STATE_DIM)
    v = v.transpose(0, 4, 1, 2, 3)[:, None]
    eye = _group_eye()[None, :, None, None, :, None]
    return jnp.where(eye, v, 0.0).reshape(N_CHUNKS, STATES_PER_HALF, CHUNK)


def _lam_rows(lre_ref, lim_ref, rows_per_half):
    first_half = (lax.broadcasted_iota(jnp.int32, (2 * rows_per_half, STATES_PER_HALF), 0)
                  < rows_per_half)
    return (jnp.where(first_half, lre_ref[0:1, :], lre_ref[1:2, :]),
            jnp.where(first_half, lim_ref[0:1, :], lim_ref[1:2, :]))


def _readout(x_re, x_im, cre_ref, cim_ref):
    return _dot(x_re.astype(_BF16), cre_ref[...]) - _dot(x_im.astype(_BF16), cim_ref[...])


def _ssm_scan_kernel(x_ref, xs_ref, g_ref, wu_ref, wz_ref, bd_ref,
                     cre_a_ref, cim_a_ref, lre_a_ref, lim_a_ref, d_a_ref,
                     cre_b_ref, cim_b_ref, lre_b_ref, lim_b_ref, d_b_ref,
                     s0re_ref, s0im_ref,
                     y_ref, zs_ref, nre_ref, nim_ref, ys_ref, zss_ref, nres_ref, nims_ref,
                     h_scr, hs_scr, xre0, xim0, u0, z0, xre1, xim1, u1, z1, carry_re, carry_im,
                     *, rows_per_step, n_blocks):
    s = pl.program_id(0)
    n_steps = n_blocks * N_CHUNKS
    step_a = jnp.minimum(s, n_steps - 1)
    step_b = jnp.maximum(s - 1, 0)
    i_a, j_a = step_a // N_CHUNKS, step_a % N_CHUNKS
    i_b, j_b = step_b // N_CHUNKS, step_b % N_CHUNKS
    tm = x_ref.shape[0]
    r = rows_per_step
    rs = xs_ref.shape[0]
    sh = STATES_PER_HALF
    assert 2 * r == SUBLANES

    @pl.when(s == 0)
    def _():
        for ref in (xre1, xim1, u1, z1):
            ref[...] = jnp.zeros(ref.shape, _F32)

    @pl.when(j_a == 0)
    def _():
        h_scr[...] = _rmsnorm(x_ref[...], g_ref[...]).astype(_BF16)

    @pl.when(i_b == 0)
    def _():
        zeros = jnp.zeros((2 * r, sh), _F32)
        carry_re[j_b] = zeros
        carry_im[j_b] = zeros

    def stages(xre_a, xim_a, u_a, z_a, xre_b, xim_b, u_b, z_b):
        h = h_scr[...]
        u = _dot(h, wu_ref[...].astype(_BF16))
        z = _dot(h, wz_ref[...].astype(_BF16))
        u_a[...] = u
        z_a[...] = z
        first_lanes = lax.broadcasted_iota(jnp.int32, (tm, CHUNK), 1) < LANES
        low_rows = (lax.broadcasted_iota(jnp.int32, (tm, CHUNK), 0) & r) == 0
        u_first = jnp.where(first_lanes, u, 0.0)
        u_second = jnp.where(first_lanes, 0.0, u)
        u_down = pltpu.roll(u, r, 0)
        u_up = pltpu.roll(u, tm - r, 0)
        even = jnp.where(low_rows, u_first, jnp.where(first_lanes, 0.0, u_down))
        odd = jnp.where(low_rows, jnp.where(first_lanes, u_up, 0.0), u_second)
        bu = _dot(jnp.concatenate([even, odd], axis=0).astype(_BF16), bd_ref[...])
        xre_a[...] = bu[:, :sh]
        xim_a[...] = bu[:, sh:]

        l_re, l_im = _lam_rows(lre_b_ref, lim_b_ref, r)
        c_re = carry_re[j_b]
        c_im = carry_im[j_b]
        for k in range(tm // SUBLANES):
            for rows in (pl.ds(k * SUBLANES, SUBLANES), pl.ds(tm + k * SUBLANES, SUBLANES)):
                c_re, c_im = (l_re * c_re - l_im * c_im + xre_b[rows, :],
                              l_re * c_im + l_im * c_re + xim_b[rows, :])
                xre_b[rows, :] = c_re
                xim_b[rows, :] = c_im
        carry_re[j_b] = c_re
        carry_im[j_b] = c_im

        yy = _readout(xre_b[...], xim_b[...], cre_b_ref, cim_b_ref)
        low_rows = (lax.broadcasted_iota(jnp.int32, (tm, LANES), 0) & r) == 0
        y_even = yy[:tm]
        y_odd = yy[tm:]
        y_first = jnp.where(low_rows, y_even[:, :LANES], pltpu.roll(y_odd[:, :LANES], r, 0))
        y_second = jnp.where(low_rows, pltpu.roll(y_even[:, LANES:], tm - r, 0),
                             y_odd[:, LANES:])
        y = jnp.concatenate([y_first, y_second], axis=1)
        y_ref[...] = jax.nn.gelu(y + d_b_ref[...] * u_b[...]).astype(_BF16)
        zs_ref[...] = _silu(z_b[...]).astype(_BF16)

    @pl.when(s % 2 == 0)
    def _():
        stages(xre0, xim0, u0, z0, xre1, xim1, u1, z1)

    @pl.when(s % 2 == 1)
    def _():
        stages(xre1, xim1, u1, z1, xre0, xim0, u0, z0)

    @pl.when((i_b == n_blocks - 1) & (s > 0))
    def _():
        c_re = carry_re[j_b]
        c_im = carry_im[j_b]
        nre_ref[:, 0:sh] = c_re[0:r]
        nre_ref[:, sh:2 * sh] = c_re[r:2 * r]
        nim_ref[:, 0:sh] = c_im[0:r]
        nim_ref[:, sh:2 * sh] = c_im[r:2 * r]

    @pl.when((i_a == n_blocks - 1) & (s < n_steps))
    def _():
        @pl.when(j_a == 0)
        def _():
            hs_scr[...] = _rmsnorm(xs_ref[...], g_ref[...]).astype(_BF16)

        hs = hs_scr[...]
        us = _dot(hs, wu_ref[...].astype(_BF16))
        zz = _dot(hs, wz_ref[...].astype(_BF16))
        first = lax.broadcasted_iota(jnp.int32, (rs, CHUNK), 1) < LANES
        lhs_s = jnp.concatenate([jnp.where(first, us, 0.0), jnp.where(first, 0.0, us)], axis=0)
        bus = _dot(lhs_s.astype(_BF16), bd_ref[...])
        p_re = jnp.concatenate([s0re_ref[:, 0:sh], s0re_ref[:, sh:2 * sh]], axis=0)
        p_im = jnp.concatenate([s0im_ref[:, 0:sh], s0im_ref[:, sh:2 * sh]], axis=0)
        ls_re, ls_im = _lam_rows(lre_a_ref, lim_a_ref, rs)
        n_re = ls_re * p_re - ls_im * p_im + bus[:, :sh]
        n_im = ls_re * p_im + ls_im * p_re + bus[:, sh:]
        nres_ref[:, 0:sh] = n_re[0:rs]
        nres_ref[:, sh:2 * sh] = n_re[rs:2 * rs]
        nims_ref[:, 0:sh] = n_im[0:rs]
        nims_ref[:, sh:2 * sh] = n_im[rs:2 * rs]
        yys = _readout(n_re, n_im, cre_a_ref, cim_a_ref)
        y_s = jnp.concatenate([yys[:rs, :LANES], yys[rs:, LANES:]], axis=1)
        ys_ref[...] = jax.nn.gelu(y_s + d_a_ref[...] * us).astype(_BF16)
        zss_ref[...] = _silu(zz).astype(_BF16)


def _ssm_scan(x, xs, norm_g, w_in, bd, cd_re, cd_im, lam_re, lam_im, d_skip, s0_re, s0_im, layer,
              *, rows_per_step):
    m = x.shape[0]
    rs = xs.shape[0]
    r = rows_per_step
    tm = TM_SCAN
    n_blocks = m // tm
    n_steps = n_blocks * N_CHUNKS
    sh = STATES_PER_HALF

    def stage_a(s):
        step = jnp.minimum(s, n_steps - 1)
        return step // N_CHUNKS, step % N_CHUNKS

    def stage_b(s):
        step = jnp.maximum(s - 1, 0)
        return step // N_CHUNKS, step % N_CHUNKS

    def per_chunk(stage, a, b):
        return pl.BlockSpec((None, a, b), lambda s: (stage(s)[1], 0, 0))

    def skip_spec(stage):
        return pl.BlockSpec((None, 1, CHUNK), lambda s: (layer, 0, stage(s)[1]))

    def live_last(stage, rows, cols):
        def index_map(s):
            i, j = stage(s)
            return 0, jnp.where(i == n_blocks - 1, j, 0)
        return pl.BlockSpec((rows, cols), index_map)

    act_spec = pl.BlockSpec((tm, CHUNK), lambda s: stage_b(s))
    state_rows = lambda rows: pltpu.VMEM((rows, sh), _F32)
    chunk_rows = pltpu.VMEM((tm, CHUNK), _F32)
    return pl.pallas_call(
        functools.partial(_ssm_scan_kernel, rows_per_step=r, n_blocks=n_blocks),
        grid=(n_steps + 1,),
        in_specs=[
            pl.BlockSpec((tm, D_MODEL), lambda s: (stage_a(s)[0], 0)),
            pl.BlockSpec((rs, D_MODEL), lambda s: (0, 0)),
            pl.BlockSpec((None, 1, D_MODEL), lambda s: (layer, 0, 0)),
            pl.BlockSpec((None, D_MODEL, CHUNK), lambda s: (layer, 0, stage_a(s)[1])),
            pl.BlockSpec((None, D_MODEL, CHUNK), lambda s: (layer, 0, N_CHUNKS + stage_a(s)[1])),
            per_chunk(stage_a, CHUNK, 2 * sh),
            per_chunk(stage_a, sh, CHUNK), per_chunk(stage_a, sh, CHUNK),
            per_chunk(stage_a, 2, sh), per_chunk(stage_a, 2, sh), skip_spec(stage_a),
            per_chunk(stage_b, sh, CHUNK), per_chunk(stage_b, sh, CHUNK),
            per_chunk(stage_b, 2, sh), per_chunk(stage_b, 2, sh), skip_spec(stage_b),
            live_last(stage_a, rs, STATES_PER_CHUNK), live_last(stage_a, rs, STATES_PER_CHUNK),
        ],
        out_specs=[
            act_spec, act_spec,
            live_last(stage_b, r, STATES_PER_CHUNK), live_last(stage_b, r, STATES_PER_CHUNK),
            live_last(stage_a, rs, CHUNK), live_last(stage_a, rs, CHUNK),
            live_last(stage_a, rs, STATES_PER_CHUNK), live_last(stage_a, rs, STATES_PER_CHUNK),
        ],
        out_shape=[
            jax.ShapeDtypeStruct((m, D_INNER), _BF16),
            jax.ShapeDtypeStruct((m, D_INNER), _BF16),
            jax.ShapeDtypeStruct((r, N_STATES), _F32),
            jax.ShapeDtypeStruct((r, N_STATES), _F32),
            jax.ShapeDtypeStruct((rs, D_INNER), _BF16),
            jax.ShapeDtypeStruct((rs, D_INNER), _BF16),
            jax.ShapeDtypeStruct((rs, N_STATES), _F32),
            jax.ShapeDtypeStruct((rs, N_STATES), _F32),
        ],
        scratch_shapes=[
            pltpu.VMEM((tm, D_MODEL), _BF16),
            pltpu.VMEM((rs, D_MODEL), _BF16),
            state_rows(2 * tm), state_rows(2 * tm), chunk_rows, chunk_rows,
            state_rows(2 * tm), state_rows(2 * tm), chunk_rows, chunk_rows,
            pltpu.VMEM((N_CHUNKS, 2 * r, sh), _F32),
            pltpu.VMEM((N_CHUNKS, 2 * r, sh), _F32),
        ],
        compiler_params=_compiler_params(1),
        name="ssm_scan",
    )(x, xs, norm_g, w_in, w_in, bd, cd_re, cd_im, lam_re, lam_im, d_skip,
      cd_re, cd_im, lam_re, lam_im, d_skip, s0_re, s0_im)


def _glu_kernel(y_ref, yj_ref, zs_ref, ys_ref, yjs_ref, zss_ref, wg_ref, bg_ref, o_ref, os_ref,
                w16):
    i = pl.program_id(1)

    @pl.when(i == 0)
    def _():
        w16[...] = wg_ref[...].astype(_BF16)

    bias = bg_ref[...]

    def gated(y_all, y_chunk, z_chunk):
        gate = jax.nn.sigmoid(_dot(y_all, w16[...]) + bias)
        return (y_chunk.astype(_F32) * gate * z_chunk.astype(_F32)).astype(_BF16)

    o_ref[...] = gated(y_ref[...], yj_ref[...], zs_ref[...])

    @pl.when(i == pl.num_programs(1) - 1)
    def _():
        os_ref[...] = gated(ys_ref[...], yjs_ref[...], zss_ref[...])


def _glu(y, zs, ys, zss, w_glu, b_glu, layer):
    m = y.shape[0]
    rs = ys.shape[0]
    tm = TM_ROWWISE
    rows = pl.BlockSpec((tm, WIDE), lambda n, i: (i, n))
    sample_rows = pl.BlockSpec((rs, WIDE), lambda n, i: (0, n))
    return pl.pallas_call(
        _glu_kernel,
        grid=(D_INNER // WIDE, m // tm),
        in_specs=[
            pl.BlockSpec((tm, D_INNER), lambda n, i: (i, 0)),
            rows, rows,
            pl.BlockSpec((rs, D_INNER), lambda n, i: (0, 0)),
            sample_rows, sample_rows,
            _stationary_weight_spec(D_INNER, layer),
            pl.BlockSpec((None, 1, WIDE), lambda n, i: (layer, 0, n)),
        ],
        out_specs=[rows, sample_rows],
        out_shape=[
            jax.ShapeDtypeStruct((m, D_INNER), _BF16),
            jax.ShapeDtypeStruct((rs, D_INNER), _BF16),
        ],
        scratch_shapes=[pltpu.VMEM((D_INNER, WIDE), _BF16)],
        compiler_params=_compiler_params(),
        name="glu",
    )(y, y, zs, ys, ys, zss, w_glu, b_glu)


def _final_norm_kernel(x_ref, g_ref, o_ref):
    o_ref[...] = _rmsnorm(x_ref[...], g_ref[...])


def _final_norm(x, g):
    m = x.shape[0]
    tm = min(m, TM_CONV)
    row_spec = pl.BlockSpec((tm, D_MODEL), lambda i: (i, 0))
    return pl.pallas_call(
        _final_norm_kernel,
        grid=(m // tm,),
        in_specs=[row_spec, pl.BlockSpec((1, D_MODEL), lambda i: (0, 0))],
        out_specs=row_spec,
        out_shape=jax.ShapeDtypeStruct((m, D_MODEL), _F32),
        compiler_params=_compiler_params(1),
        name="final_norm",
    )(x, g)


def kernel(x_prompt, x_sample, state_conv, state_ssm_re, state_ssm_im,
           conv_norm, conv_w_in, conv_w, conv_w_out,
           ssm_norm, ssm_w_in, ssm_a_re, ssm_a_im, ssm_log_dt, ssm_b_re, ssm_b_im,
           ssm_c_re, ssm_c_im, ssm_d, ssm_w_glu, ssm_b_glu, ssm_w_out, final_norm):
    batch, seq, _ = x_prompt.shape
    dec_batch = x_sample.shape[0]
    n_conv = conv_w_in.shape[0]
    n_ssm = ssm_w_in.shape[0]
    depth = n_conv + n_ssm

    xp = x_prompt.transpose(1, 0, 2).reshape(seq * batch, D_MODEL)
    xs = x_sample.reshape(dec_batch, D_MODEL)
    conv_norm3 = conv_norm.reshape(n_conv, 1, D_MODEL)
    ssm_norm3 = ssm_norm.reshape(n_ssm, 1, D_MODEL)
    ssm_d3 = ssm_d.reshape(n_ssm, 1, D_INNER)
    ssm_b_glu3 = ssm_b_glu.reshape(n_ssm, 1, D_INNER)

    conv_p, conv_s, re_p, im_p, re_s, im_s = [], [], [], [], [], []
    for layer in range(depth):
        l = layer // 2
        if layer % 2 == 0:
            y, ys, ns_p, ns_s = _conv_in(
                xp, xs, conv_norm3, conv_w_in, conv_w,
                state_conv[l].reshape(dec_batch, (CONV_WIDTH - 1) * D_INNER), l,
                rows_per_step=batch)
            xp, xs = _out_proj(y, ys, conv_w_out, xp, xs, l)
            conv_p.append(ns_p)
            conv_s.append(ns_s)
        else:
            lam_re, lam_im, bb_re, bb_im = _ssm_prep(
                ssm_a_re[l], ssm_a_im[l], ssm_log_dt[l], ssm_b_re[l], ssm_b_im[l])
            bd = jnp.concatenate([_block_diag_in(bb_re), _block_diag_in(bb_im)],
                                 axis=-1).astype(_BF16)
            cd_re = _block_diag_out(ssm_c_re[l]).astype(_BF16)
            cd_im = _block_diag_out(ssm_c_im[l]).astype(_BF16)
            lam_re = lam_re.reshape(N_CHUNKS, 2, STATES_PER_HALF)
            lam_im = lam_im.reshape(N_CHUNKS, 2, STATES_PER_HALF)
            y, zs, hr_p, hi_p, ys, zss, hr_s, hi_s = _ssm_scan(
                xp, xs, ssm_norm3, ssm_w_in, bd, cd_re, cd_im, lam_re, lam_im, ssm_d3,
                state_ssm_re[l].reshape(dec_batch, N_STATES),
                state_ssm_im[l].reshape(dec_batch, N_STATES),
                l, rows_per_step=batch)
            yy, yys = _glu(y, zs, ys, zss, ssm_w_glu, ssm_b_glu3, l)
            xp, xs = _out_proj(yy, yys, ssm_w_out, xp, xs, l)
            re_p.append(hr_p.reshape(batch, N_GROUPS, STATE_DIM))
            im_p.append(hi_p.reshape(batch, N_GROUPS, STATE_DIM))
            re_s.append(hr_s.reshape(dec_batch, N_GROUPS, STATE_DIM))
            im_s.append(hi_s.reshape(dec_batch, N_GROUPS, STATE_DIM))

    final_g = final_norm.reshape(1, D_MODEL)
    xp = _final_norm(xp, final_g)
    xs = _final_norm(xs, final_g)
    y_prompt = xp.reshape(seq, batch, D_MODEL).transpose(1, 0, 2)
    y_sample = xs.reshape(dec_batch, 1, D_MODEL)
    return (y_prompt, y_sample, jnp.stack(conv_p), jnp.stack(conv_s),
            jnp.stack(re_p), jnp.stack(im_p), jnp.stack(re_s), jnp.stack(im_s))
```

```python
import functools

import jax
import jax.numpy as jnp
from jax import lax
from jax.experimental import pallas as pl
from jax.experimental.pallas import tpu as pltpu

D_MODEL = 2048
D_INNER = 2 * D_MODEL
CONV_WIDTH = 3
GROUP_SIZE = 16
N_GROUPS = D_INNER // GROUP_SIZE
STATE_DIM = 64
N_STATES = N_GROUPS * STATE_DIM
EPS = 1e-6

LANES = 128
SUBLANES = 8
CHUNK = 2 * LANES
N_CHUNKS = D_INNER // CHUNK
GROUPS_PER_HALF = LANES // GROUP_SIZE
STATES_PER_HALF = GROUPS_PER_HALF * STATE_DIM
STATES_PER_CHUNK = 2 * STATES_PER_HALF
VMEM_LIMIT_BYTES = 60000 * 1024

TM_ROWWISE = 512
TM_CONV = 1024
TM_SCAN = 512
WIDE = 1024
TT_REORDER = 128

_F32 = jnp.float32
_BF16 = jnp.bfloat16


def _dot(a, b):
    return jnp.dot(a, b, preferred_element_type=_F32)


def _rmsnorm(x, g):
    return x * lax.rsqrt(jnp.mean(x * x, axis=-1, keepdims=True) + EPS) * g


def _silu(z):
    return z * jax.nn.sigmoid(z)


def _compiler_params(n_axes=2):
    return pltpu.CompilerParams(
        dimension_semantics=("arbitrary",) * n_axes,
        vmem_limit_bytes=VMEM_LIMIT_BYTES)


def _last_block_cols(n_blocks, offset=0):
    return lambda i, j: (0, offset + jnp.where(i == n_blocks - 1, j, 0))


def _conv_in_kernel(x_ref, xs_ref, g_ref, wb_ref, wc_ref, wv_ref, wz_ref, cw_ref,
                    s2s_ref, s1s_ref,
                    y_ref, n2_ref, n1_ref, ys_ref, n2s_ref, n1s_ref,
                    h_scr, hs_scr, ubuf, carry, *, rows_per_step):
    i = pl.program_id(0)
    j = pl.program_id(1)
    last = i == pl.num_programs(0) - 1
    tm = x_ref.shape[0]
    r = rows_per_step

    @pl.when(j == 0)
    def _():
        h_scr[...] = _rmsnorm(x_ref[...], g_ref[...]).astype(_BF16)

    @pl.when(i == 0)
    def _():
        carry[j] = jnp.zeros(carry.shape[1:], _F32)

    w16 = [w[...].astype(_BF16) for w in (wb_ref, wc_ref, wv_ref, wz_ref)]
    cw = cw_ref[...]

    h = h_scr[...]
    b, c, v, z = [_dot(h, w) for w in w16]
    ubuf[0:2 * r, :] = carry[j]
    ubuf[2 * r:2 * r + tm, :] = c * v
    conv = (cw[0:1, :] * ubuf[0:tm, :] + cw[1:2, :] * ubuf[r:r + tm, :]
            + cw[2:3, :] * ubuf[2 * r:2 * r + tm, :])
    y_ref[...] = (b * conv * _silu(z)).astype(_BF16)
    carry[j] = ubuf[tm:tm + 2 * r, :]

    @pl.when(last)
    def _():
        n2_ref[...] = ubuf[tm:tm + r, :]
        n1_ref[...] = ubuf[tm + r:tm + 2 * r, :]

        @pl.when(j == 0)
        def _():
            hs_scr[...] = _rmsnorm(xs_ref[...], g_ref[...]).astype(_BF16)

        hs = hs_scr[...]
        bs, cs, vs, zs = [_dot(hs, w) for w in w16]
        us = cs * vs
        s1s = s1s_ref[...]
        convs = cw[0:1, :] * s2s_ref[...] + cw[1:2, :] * s1s + cw[2:3, :] * us
        ys_ref[...] = (bs * convs * _silu(zs)).astype(_BF16)
        n2s_ref[...] = s1s
        n1s_ref[...] = us


def _conv_in(x, xs, norm_g, w_in, conv_w, state_s, layer, *, rows_per_step):
    m = x.shape[0]
    rs = xs.shape[0]
    r = rows_per_step
    tm = TM_CONV
    n_blocks = m // tm
    w_in_spec = lambda part: pl.BlockSpec(
        (None, D_MODEL, CHUNK), lambda i, j: (layer, 0, part * N_CHUNKS + j))
    live_last = lambda rows, offset=0: pl.BlockSpec(
        (rows, CHUNK), _last_block_cols(n_blocks, offset))
    y, n2, n1, ys, n2s, n1s = pl.pallas_call(
        functools.partial(_conv_in_kernel, rows_per_step=r),
        grid=(n_blocks, N_CHUNKS),
        in_specs=[
            pl.BlockSpec((tm, D_MODEL), lambda i, j: (i, 0)),
            pl.BlockSpec((rs, D_MODEL), lambda i, j: (0, 0)),
            pl.BlockSpec((None, 1, D_MODEL), lambda i, j: (layer, 0, 0)),
            w_in_spec(0), w_in_spec(1), w_in_spec(2), w_in_spec(3),
            pl.BlockSpec((None, CONV_WIDTH, CHUNK), lambda i, j: (layer, 0, j)),
            live_last(rs), live_last(rs, N_CHUNKS),
        ],
        out_specs=[
            pl.BlockSpec((tm, CHUNK), lambda i, j: (i, j)),
            live_last(r), live_last(r),
            live_last(rs), live_last(rs), live_last(rs),
        ],
        out_shape=[
            jax.ShapeDtypeStruct((m, D_INNER), _BF16),
            jax.ShapeDtypeStruct((r, D_INNER), _F32),
            jax.ShapeDtypeStruct((r, D_INNER), _F32),
            jax.ShapeDtypeStruct((rs, D_INNER), _BF16),
            jax.ShapeDtypeStruct((rs, D_INNER), _F32),
            jax.ShapeDtypeStruct((rs, D_INNER), _F32),
        ],
        scratch_shapes=[
            pltpu.VMEM((tm, D_MODEL), _BF16),
            pltpu.VMEM((rs, D_MODEL), _BF16),
            pltpu.VMEM((tm + 2 * r, CHUNK), _F32),
            pltpu.VMEM((N_CHUNKS, 2 * r, CHUNK), _F32),
        ],
        compiler_params=_compiler_params(),
        name="conv_in",
    )(x, xs, norm_g, w_in, w_in, w_in, w_in, conv_w, state_s, state_s)
    return y, ys, jnp.stack([n2, n1], axis=1), jnp.stack([n2s, n1s], axis=1)


def _out_proj_kernel(y_ref, ys_ref, w_ref, x_ref, xs_ref, o_ref, os_ref, w16):
    i = pl.program_id(1)

    @pl.when(i == 0)
    def _():
        w16[...] = w_ref[...].astype(_BF16)

    o_ref[...] = x_ref[...] + _dot(y_ref[...], w16[...])

    @pl.when(i == pl.num_programs(1) - 1)
    def _():
        os_ref[...] = xs_ref[...] + _dot(ys_ref[...], w16[...])


def _stationary_weight_spec(k_dim, layer):
    return pl.BlockSpec((None, k_dim, WIDE), lambda n, i: (layer, 0, n),
                        pipeline_mode=pl.Buffered(1))


def _out_proj(y, ys, w_out, x, xs, layer):
    m = x.shape[0]
    rs = xs.shape[0]
    tm = TM_ROWWISE
    rows = lambda width: pl.BlockSpec((tm, width), lambda n, i: (i, n))
    sample_rows = lambda width: pl.BlockSpec((rs, width), lambda n, i: (0, n))
    return pl.pallas_call(
        _out_proj_kernel,
        grid=(D_MODEL // WIDE, m // tm),
        in_specs=[
            pl.BlockSpec((tm, D_INNER), lambda n, i: (i, 0)),
            pl.BlockSpec((rs, D_INNER), lambda n, i: (0, 0)),
            _stationary_weight_spec(D_INNER, layer),
            rows(WIDE), sample_rows(WIDE),
        ],
        out_specs=[rows(WIDE), sample_rows(WIDE)],
        out_shape=[
            jax.ShapeDtypeStruct((m, D_MODEL), _F32),
            jax.ShapeDtypeStruct((rs, D_MODEL), _F32),
        ],
        scratch_shapes=[pltpu.VMEM((D_INNER, WIDE), _BF16)],
        compiler_params=_compiler_params(),
        name="out_proj",
    )(y, ys, w_out, x, xs)


def _ssm_prep_kernel(are_ref, aim_ref, logdt_ref, btre_ref, btim_ref, cre_ref, cim_ref,
                     lre_ref, lim_ref, bd_ref, cdre_ref, cdim_ref):
    a_re = are_ref[...]
    a_im = aim_ref[...]
    dt = jnp.exp(logdt_ref[...])
    mag = jnp.exp(a_re * dt)
    l_re = mag * jnp.cos(a_im * dt)
    l_im = mag * jnp.sin(a_im * dt)
    lre_ref[...] = l_re
    lim_ref[...] = l_im
    n_re = l_re - 1.0
    den = a_re * a_re + a_im * a_im
    k_re = (n_re * a_re + l_im * a_im) / den
    k_im = (l_im * a_re - n_re * a_im) / den
    g = a_re.shape[0]
    expand = lambda k: jnp.broadcast_to(
        k[:, None, :], (g, GROUP_SIZE, STATE_DIM)).reshape(g * GROUP_SIZE, STATE_DIM)
    k_re = expand(k_re)
    k_im = expand(k_im)
    bt_re = btre_ref[...]
    bt_im = btim_ref[...]
    bb_re = k_re * bt_re - k_im * bt_im
    bb_im = k_re * bt_im + k_im * bt_re

    shape = (CHUNK, STATES_PER_HALF)
    log2 = lambda n: n.bit_length() - 1
    row_group = (jnp.right_shift(lax.broadcasted_iota(jnp.int32, shape, 0), log2(GROUP_SIZE))
                 & (GROUPS_PER_HALF - 1))
    col_group = jnp.right_shift(lax.broadcasted_iota(jnp.int32, shape, 1), log2(STATE_DIM))
    own = row_group == col_group
    spread = lambda v: jnp.where(own, jnp.concatenate([v] * GROUPS_PER_HALF, axis=1), 0.0)
    bd_ref[...] = jnp.concatenate([spread(bb_re), spread(bb_im)], axis=1).astype(_BF16)
    cdre_ref[...] = spread(cre_ref[...]).T.astype(_BF16)
    cdim_ref[...] = spread(cim_ref[...]).T.astype(_BF16)


def _ssm_prep(a_re, a_im, log_dt, b_re, b_im, c_re, c_im):
    gc = CHUNK // GROUP_SIZE
    rows = N_GROUPS * GROUP_SIZE
    bt_re = b_re.transpose(0, 2, 1).reshape(rows, STATE_DIM)
    bt_im = b_im.transpose(0, 2, 1).reshape(rows, STATE_DIM)
    gp_spec = pl.BlockSpec((gc, STATE_DIM), lambda j: (j, 0))
    ghp_spec = pl.BlockSpec((CHUNK, STATE_DIM), lambda j: (j, 0))
    per_chunk = lambda a, b: pl.BlockSpec((None, a, b), lambda j: (j, 0, 0))
    return pl.pallas_call(
        _ssm_prep_kernel,
        grid=(N_CHUNKS,),
        in_specs=[gp_spec, gp_spec, pl.BlockSpec((gc, 1), lambda j: (j, 0)),
                  ghp_spec, ghp_spec, ghp_spec, ghp_spec],
        out_specs=[gp_spec, gp_spec, per_chunk(CHUNK, 2 * STATES_PER_HALF),
                   per_chunk(STATES_PER_HALF, CHUNK), per_chunk(STATES_PER_HALF, CHUNK)],
        out_shape=[
            jax.ShapeDtypeStruct((N_GROUPS, STATE_DIM), _F32),
            jax.ShapeDtypeStruct((N_GROUPS, STATE_DIM), _F32),
            jax.ShapeDtypeStruct((N_CHUNKS, CHUNK, 2 * STATES_PER_HALF), _BF16),
            jax.ShapeDtypeStruct((N_CHUNKS, STATES_PER_HALF, CHUNK), _BF16),
            jax.ShapeDtypeStruct((N_CHUNKS, STATES_PER_HALF, CHUNK), _BF16),
        ],
        compiler_params=_compiler_params(1),
        name="ssm_prep",
    )(a_re, a_im, log_dt.reshape(N_GROUPS, 1), bt_re, bt_im,
      c_re.reshape(rows, STATE_DIM), c_im.reshape(rows, STATE_DIM))


def _lam_rows(lre_ref, lim_ref, rows_per_half):
    first_half = (lax.broadcasted_iota(jnp.int32, (2 * rows_per_half, STATES_PER_HALF), 0)
                  < rows_per_half)
    return (jnp.where(first_half, lre_ref[0:1, :], lre_ref[1:2, :]),
            jnp.where(first_half, lim_ref[0:1, :], lim_ref[1:2, :]))


def _readout(x_re, x_im, cre_ref, cim_ref):
    return _dot(x_re.astype(_BF16), cre_ref[...]) - _dot(x_im.astype(_BF16), cim_ref[...])


def _ssm_scan_kernel(x_ref, xs_ref, g_ref, wu_ref, wz_ref, bd_ref,
                     cre_a_ref, cim_a_ref, lre_a_ref, lim_a_ref, d_a_ref,
                     cre_b_ref, cim_b_ref, lre_b_ref, lim_b_ref, d_b_ref,
                     s0re_ref, s0im_ref,
                     y_ref, zs_ref, nre_ref, nim_ref, ys_ref, zss_ref, nres_ref, nims_ref,
                     h_scr, hs_scr, xre0, xim0, u0, z0, xre1, xim1, u1, z1, carry_re, carry_im,
                     *, rows_per_step, n_blocks):
    s = pl.program_id(0)
    n_steps = n_blocks * N_CHUNKS
    step_a = jnp.minimum(s, n_steps - 1)
    step_b = jnp.maximum(s - 1, 0)
    i_a, j_a = step_a // N_CHUNKS, step_a % N_CHUNKS
    i_b, j_b = step_b // N_CHUNKS, step_b % N_CHUNKS
    tm = x_ref.shape[0]
    r = rows_per_step
    rs = xs_ref.shape[0]
    sh = STATES_PER_HALF
    assert 2 * r == SUBLANES

    @pl.when(s == 0)
    def _():
        for ref in (xre1, xim1, u1, z1):
            ref[...] = jnp.zeros(ref.shape, _F32)

    @pl.when(j_a == 0)
    def _():
        h_scr[...] = _rmsnorm(x_ref[...], g_ref[...]).astype(_BF16)

    @pl.when(i_b == 0)
    def _():
        zeros = jnp.zeros((2 * r, sh), _F32)
        carry_re[j_b] = zeros
        carry_im[j_b] = zeros

    def stages(xre_a, xim_a, u_a, z_a, xre_b, xim_b, u_b, z_b):
        h = h_scr[...]
        u = _dot(h, wu_ref[...].astype(_BF16))
        z = _dot(h, wz_ref[...].astype(_BF16))
        u_a[...] = u
        z_a[...] = z
        first_lanes = lax.broadcasted_iota(jnp.int32, (tm, CHUNK), 1) < LANES
        low_rows = (lax.broadcasted_iota(jnp.int32, (tm, CHUNK), 0) & r) == 0
        u_first = jnp.where(first_lanes, u, 0.0)
        u_second = jnp.where(first_lanes, 0.0, u)
        u_down = pltpu.roll(u, r, 0)
        u_up = pltpu.roll(u, tm - r, 0)
        even = jnp.where(low_rows, u_first, jnp.where(first_lanes, 0.0, u_down))
        odd = jnp.where(low_rows, jnp.where(first_lanes, u_up, 0.0), u_second)
        bu = _dot(jnp.concatenate([even, odd], axis=0).astype(_BF16), bd_ref[...])
        xre_a[...] = bu[:, :sh]
        xim_a[...] = bu[:, sh:]

        l_re, l_im = _lam_rows(lre_b_ref, lim_b_ref, r)
        c_re = carry_re[j_b]
        c_im = carry_im[j_b]
        for k in range(tm // SUBLANES):
            for rows in (pl.ds(k * SUBLANES, SUBLANES), pl.ds(tm + k * SUBLANES, SUBLANES)):
                c_re, c_im = (l_re * c_re - l_im * c_im + xre_b[rows, :],
                              l_re * c_im + l_im * c_re + xim_b[rows, :])
                xre_b[rows, :] = c_re
                xim_b[rows, :] = c_im
        carry_re[j_b] = c_re
        carry_im[j_b] = c_im

        yy = _readout(xre_b[...], xim_b[...], cre_b_ref, cim_b_ref)
        low_rows = (lax.broadcasted_iota(jnp.int32, (tm, LANES), 0) & r) == 0
        y_even = yy[:tm]
        y_odd = yy[tm:]
        y_first = jnp.where(low_rows, y_even[:, :LANES], pltpu.roll(y_odd[:, :LANES], r, 0))
        y_second = jnp.where(low_rows, pltpu.roll(y_even[:, LANES:], tm - r, 0),
                             y_odd[:, LANES:])
        y = jnp.concatenate([y_first, y_second], axis=1)
        y_ref[...] = jax.nn.gelu(y + d_b_ref[...] * u_b[...]).astype(_BF16)
        zs_ref[...] = _silu(z_b[...]).astype(_BF16)

    @pl.when(s % 2 == 0)
    def _():
        stages(xre0, xim0, u0, z0, xre1, xim1, u1, z1)

    @pl.when(s % 2 == 1)
    def _():
        stages(xre1, xim1, u1, z1, xre0, xim0, u0, z0)

    @pl.when((i_b == n_blocks - 1) & (s > 0))
    def _():
        c_re = carry_re[j_b]
        c_im = carry_im[j_b]
        nre_ref[:, 0:sh] = c_re[0:r]
        nre_ref[:, sh:2 * sh] = c_re[r:2 * r]
        nim_ref[:, 0:sh] = c_im[0:r]
        nim_ref[:, sh:2 * sh] = c_im[r:2 * r]

    @pl.when((i_a == n_blocks - 1) & (s < n_steps))
    def _():
        @pl.when(j_a == 0)
        def _():
            hs_scr[...] = _rmsnorm(xs_ref[...], g_ref[...]).astype(_BF16)

        hs = hs_scr[...]
        us = _dot(hs, wu_ref[...].astype(_BF16))
        zz = _dot(hs, wz_ref[...].astype(_BF16))
        first = lax.broadcasted_iota(jnp.int32, (rs, CHUNK), 1) < LANES
        lhs_s = jnp.concatenate([jnp.where(first, us, 0.0), jnp.where(first, 0.0, us)], axis=0)
        bus = _dot(lhs_s.astype(_BF16), bd_ref[...])
        p_re = jnp.concatenate([s0re_ref[:, 0:sh], s0re_ref[:, sh:2 * sh]], axis=0)
        p_im = jnp.concatenate([s0im_ref[:, 0:sh], s0im_ref[:, sh:2 * sh]], axis=0)
        ls_re, ls_im = _lam_rows(lre_a_ref, lim_a_ref, rs)
        n_re = ls_re * p_re - ls_im * p_im + bus[:, :sh]
        n_im = ls_re * p_im + ls_im * p_re + bus[:, sh:]
        nres_ref[:, 0:sh] = n_re[0:rs]
        nres_ref[:, sh:2 * sh] = n_re[rs:2 * rs]
        nims_ref[:, 0:sh] = n_im[0:rs]
        nims_ref[:, sh:2 * sh] = n_im[rs:2 * rs]
        yys = _readout(n_re, n_im, cre_a_ref, cim_a_ref)
        y_s = jnp.concatenate([yys[:rs, :LANES], yys[rs:, LANES:]], axis=1)
        ys_ref[...] = jax.nn.gelu(y_s + d_a_ref[...] * us).astype(_BF16)
        zss_ref[...] = _silu(zz).astype(_BF16)


def _ssm_scan(x, xs, norm_g, w_in, bd, cd_re, cd_im, lam_re, lam_im, d_skip, s0_re, s0_im, layer,
              *, rows_per_step):
    m = x.shape[0]
    rs = xs.shape[0]
    r = rows_per_step
    tm = TM_SCAN
    n_blocks = m // tm
    n_steps = n_blocks * N_CHUNKS
    sh = STATES_PER_HALF

    def stage_a(s):
        step = jnp.minimum(s, n_steps - 1)
        return step // N_CHUNKS, step % N_CHUNKS

    def stage_b(s):
        step = jnp.maximum(s - 1, 0)
        return step // N_CHUNKS, step % N_CHUNKS

    def per_chunk(stage, a, b):
        return pl.BlockSpec((None, a, b), lambda s: (stage(s)[1], 0, 0))

    def skip_spec(stage):
        return pl.BlockSpec((None, 1, CHUNK), lambda s: (layer, 0, stage(s)[1]))

    def live_last(stage, rows, cols):
        def index_map(s):
            i, j = stage(s)
            return 0, jnp.where(i == n_blocks - 1, j, 0)
        return pl.BlockSpec((rows, cols), index_map)

    act_spec = pl.BlockSpec((tm, CHUNK), lambda s: stage_b(s))
    state_rows = lambda rows: pltpu.VMEM((rows, sh), _F32)
    chunk_rows = pltpu.VMEM((tm, CHUNK), _F32)
    return pl.pallas_call(
        functools.partial(_ssm_scan_kernel, rows_per_step=r, n_blocks=n_blocks),
        grid=(n_steps + 1,),
        in_specs=[
            pl.BlockSpec((tm, D_MODEL), lambda s: (stage_a(s)[0], 0)),
            pl.BlockSpec((rs, D_MODEL), lambda s: (0, 0)),
            pl.BlockSpec((None, 1, D_MODEL), lambda s: (layer, 0, 0)),
            pl.BlockSpec((None, D_MODEL, CHUNK), lambda s: (layer, 0, stage_a(s)[1])),
            pl.BlockSpec((None, D_MODEL, CHUNK), lambda s: (layer, 0, N_CHUNKS + stage_a(s)[1])),
            per_chunk(stage_a, CHUNK, 2 * sh),
            per_chunk(stage_a, sh, CHUNK), per_chunk(stage_a, sh, CHUNK),
            per_chunk(stage_a, 2, sh), per_chunk(stage_a, 2, sh), skip_spec(stage_a),
            per_chunk(stage_b, sh, CHUNK), per_chunk(stage_b, sh, CHUNK),
            per_chunk(stage_b, 2, sh), per_chunk(stage_b, 2, sh), skip_spec(stage_b),
            live_last(stage_a, rs, STATES_PER_CHUNK), live_last(stage_a, rs, STATES_PER_CHUNK),
        ],
        out_specs=[
            act_spec, act_spec,
            live_last(stage_b, r, STATES_PER_CHUNK), live_last(stage_b, r, STATES_PER_CHUNK),
            live_last(stage_a, rs, CHUNK), live_last(stage_a, rs, CHUNK),
            live_last(stage_a, rs, STATES_PER_CHUNK), live_last(stage_a, rs, STATES_PER_CHUNK),
        ],
        out_shape=[
            jax.ShapeDtypeStruct((m, D_INNER), _BF16),
            jax.ShapeDtypeStruct((m, D_INNER), _BF16),
            jax.ShapeDtypeStruct((r, N_STATES), _F32),
            jax.ShapeDtypeStruct((r, N_STATES), _F32),
            jax.ShapeDtypeStruct((rs, D_INNER), _BF16),
            jax.ShapeDtypeStruct((rs, D_INNER), _BF16),
            jax.ShapeDtypeStruct((rs, N_STATES), _F32),
            jax.ShapeDtypeStruct((rs, N_STATES), _F32),
        ],
        scratch_shapes=[
            pltpu.VMEM((tm, D_MODEL), _BF16),
            pltpu.VMEM((rs, D_MODEL), _BF16),
            state_rows(2 * tm), state_rows(2 * tm), chunk_rows, chunk_rows,
            state_rows(2 * tm), state_rows(2 * tm), chunk_rows, chunk_rows,
            pltpu.VMEM((N_CHUNKS, 2 * r, sh), _F32),
            pltpu.VMEM((N_CHUNKS, 2 * r, sh), _F32),
        ],
        compiler_params=_compiler_params(1),
        name="ssm_scan",
    )(x, xs, norm_g, w_in, w_in, bd, cd_re, cd_im, lam_re, lam_im, d_skip,
      cd_re, cd_im, lam_re, lam_im, d_skip, s0_re, s0_im)


def _glu_kernel(y_ref, yj_ref, zs_ref, ys_ref, yjs_ref, zss_ref, wg_ref, bg_ref, o_ref, os_ref,
                w16):
    i = pl.program_id(1)

    @pl.when(i == 0)
    def _():
        w16[...] = wg_ref[...].astype(_BF16)

    bias = bg_ref[...]

    def gated(y_all, y_chunk, z_chunk):
        gate = jax.nn.sigmoid(_dot(y_all, w16[...]) + bias)
        return (y_chunk.astype(_F32) * gate * z_chunk.astype(_F32)).astype(_BF16)

    o_ref[...] = gated(y_ref[...], yj_ref[...], zs_ref[...])

    @pl.when(i == pl.num_programs(1) - 1)
    def _():
        os_ref[...] = gated(ys_ref[...], yjs_ref[...], zss_ref[...])


def _glu(y, zs, ys, zss, w_glu, b_glu, layer):
    m = y.shape[0]
    rs = ys.shape[0]
    tm = TM_ROWWISE
    rows = pl.BlockSpec((tm, WIDE), lambda n, i: (i, n))
    sample_rows = pl.BlockSpec((rs, WIDE), lambda n, i: (0, n))
    return pl.pallas_call(
        _glu_kernel,
        grid=(D_INNER // WIDE, m // tm),
        in_specs=[
            pl.BlockSpec((tm, D_INNER), lambda n, i: (i, 0)),
            rows, rows,
            pl.BlockSpec((rs, D_INNER), lambda n, i: (0, 0)),
            sample_rows, sample_rows,
            _stationary_weight_spec(D_INNER, layer),
            pl.BlockSpec((None, 1, WIDE), lambda n, i: (layer, 0, n)),
        ],
        out_specs=[rows, sample_rows],
        out_shape=[
            jax.ShapeDtypeStruct((m, D_INNER), _BF16),
            jax.ShapeDtypeStruct((rs, D_INNER), _BF16),
        ],
        scratch_shapes=[pltpu.VMEM((D_INNER, WIDE), _BF16)],
        compiler_params=_compiler_params(),
        name="glu",
    )(y, y, zs, ys, ys, zss, w_glu, b_glu)


def _to_time_major_kernel(x_ref, o_ref, slab):
    nb, tt, _ = x_ref.shape
    for k in range(D_MODEL // LANES):
        lanes = slice(k * LANES, (k + 1) * LANES)
        for b in range(nb):
            slab[k, pl.ds(b, tt, stride=nb), :] = x_ref[b, :, lanes]
        o_ref[:, lanes] = slab[k]


def _to_time_major(x):
    nb, seq, _ = x.shape
    tt = TT_REORDER
    return pl.pallas_call(
        _to_time_major_kernel,
        grid=(seq // tt,),
        in_specs=[pl.BlockSpec((nb, tt, D_MODEL), lambda i: (0, i, 0))],
        out_specs=pl.BlockSpec((nb * tt, D_MODEL), lambda i: (i, 0)),
        out_shape=jax.ShapeDtypeStruct((seq * nb, D_MODEL), _F32),
        scratch_shapes=[pltpu.VMEM((D_MODEL // LANES, nb * tt, LANES), _F32)],
        compiler_params=_compiler_params(1),
        name="to_time_major",
    )(x)


def _final_norm_batch_major_kernel(x_ref, g_ref, o_ref, slab):
    nb, tt, _ = o_ref.shape
    xn = _rmsnorm(x_ref[...], g_ref[...])
    for k in range(D_MODEL // LANES):
        lanes = slice(k * LANES, (k + 1) * LANES)
        slab[k] = xn[:, lanes]
        for b in range(nb):
            o_ref[b, :, lanes] = slab[k, pl.ds(b, tt, stride=nb), :]


def _final_norm_batch_major(x, g, nb):
    seq = x.shape[0] // nb
    tt = TT_REORDER
    return pl.pallas_call(
        _final_norm_batch_major_kernel,
        grid=(seq // tt,),
        in_specs=[pl.BlockSpec((nb * tt, D_MODEL), lambda i: (i, 0)),
                  pl.BlockSpec((1, D_MODEL), lambda i: (0, 0))],
        out_specs=pl.BlockSpec((nb, tt, D_MODEL), lambda i: (0, i, 0)),
        out_shape=jax.ShapeDtypeStruct((nb, seq, D_MODEL), _F32),
        scratch_shapes=[pltpu.VMEM((D_MODEL // LANES, nb * tt, LANES), _F32)],
        compiler_params=_compiler_params(1),
        name="final_norm_batch_major",
    )(x, g)


def _final_norm_kernel(x_ref, g_ref, o_ref):
    o_ref[...] = _rmsnorm(x_ref[...], g_ref[...])


def _final_norm(x, g):
    m = x.shape[0]
    tm = min(m, TM_CONV)
    row_spec = pl.BlockSpec((tm, D_MODEL), lambda i: (i, 0))
    return pl.pallas_call(
        _final_norm_kernel,
        grid=(m // tm,),
        in_specs=[row_spec, pl.BlockSpec((1, D_MODEL), lambda i: (0, 0))],
        out_specs=row_spec,
        out_shape=jax.ShapeDtypeStruct((m, D_MODEL), _F32),
        compiler_params=_compiler_params(1),
        name="final_norm",
    )(x, g)


def kernel(x_prompt, x_sample, state_conv, state_ssm_re, state_ssm_im,
           conv_norm, conv_w_in, conv_w, conv_w_out,
           ssm_norm, ssm_w_in, ssm_a_re, ssm_a_im, ssm_log_dt, ssm_b_re, ssm_b_im,
           ssm_c_re, ssm_c_im, ssm_d, ssm_w_glu, ssm_b_glu, ssm_w_out, final_norm):
    batch, seq, _ = x_prompt.shape
    dec_batch = x_sample.shape[0]
    n_conv = conv_w_in.shape[0]
    n_ssm = ssm_w_in.shape[0]
    depth = n_conv + n_ssm

    xp = _to_time_major(x_prompt)
    xs = x_sample.reshape(dec_batch, D_MODEL)
    conv_norm3 = conv_norm.reshape(n_conv, 1, D_MODEL)
    ssm_norm3 = ssm_norm.reshape(n_ssm, 1, D_MODEL)
    ssm_d3 = ssm_d.reshape(n_ssm, 1, D_INNER)
    ssm_b_glu3 = ssm_b_glu.reshape(n_ssm, 1, D_INNER)

    conv_p, conv_s, re_p, im_p, re_s, im_s = [], [], [], [], [], []
    for layer in range(depth):
        l = layer // 2
        if layer % 2 == 0:
            y, ys, ns_p, ns_s = _conv_in(
                xp, xs, conv_norm3, conv_w_in, conv_w,
                state_conv[l].reshape(dec_batch, (CONV_WIDTH - 1) * D_INNER), l,
                rows_per_step=batch)
            xp, xs = _out_proj(y, ys, conv_w_out, xp, xs, l)
            conv_p.append(ns_p)
            conv_s.append(ns_s)
        else:
            lam_re, lam_im, bd, cd_re, cd_im = _ssm_prep(
                ssm_a_re[l], ssm_a_im[l], ssm_log_dt[l], ssm_b_re[l], ssm_b_im[l],
                ssm_c_re[l], ssm_c_im[l])
            lam_re = lam_re.reshape(N_CHUNKS, 2, STATES_PER_HALF)
            lam_im = lam_im.reshape(N_CHUNKS, 2, STATES_PER_HALF)
            y, zs, hr_p, hi_p, ys, zss, hr_s, hi_s = _ssm_scan(
                xp, xs, ssm_norm3, ssm_w_in, bd, cd_re, cd_im, lam_re, lam_im, ssm_d3,
                state_ssm_re[l].reshape(dec_batch, N_STATES),
                state_ssm_im[l].reshape(dec_batch, N_STATES),
                l, rows_per_step=batch)
            yy, yys = _glu(y, zs, ys, zss, ssm_w_glu, ssm_b_glu3, l)
            xp, xs = _out_proj(yy, yys, ssm_w_out, xp, xs, l)
            re_p.append(hr_p.reshape(batch, N_GROUPS, STATE_DIM))
            im_p.append(hi_p.reshape(batch, N_GROUPS, STATE_DIM))
            re_s.append(hr_s.reshape(dec_batch, N_GROUPS, STATE_DIM))
            im_s.append(hi_s.reshape(dec_batch, N_GROUPS, STATE_DIM))

    final_g = final_norm.reshape(1, D_MODEL)
    y_prompt = _final_norm_batch_major(xp, final_g, batch)
    xs = _final_norm(xs, final_g)
    y_sample = xs.reshape(dec_batch, 1, D_MODEL)
    return (y_prompt, y_sample, jnp.stack(conv_p), jnp.stack(conv_s),
            jnp.stack(re_p), jnp.stack(im_p), jnp.stack(re_s), jnp.stack(im_s))
```

```python
import functools

import jax
import jax.numpy as jnp
from jax import lax
from jax.experimental import pallas as pl
from jax.experimental.pallas import tpu as pltpu

D_MODEL = 2048
D_INNER = 2 * D_MODEL
CONV_WIDTH = 3
GROUP_SIZE = 16
N_GROUPS = D_INNER // GROUP_SIZE
STATE_DIM = 64
N_STATES = N_GROUPS * STATE_DIM
EPS = 1e-6

LANES = 128
SUBLANES = 8
CHUNK = 2 * LANES
N_CHUNKS = D_INNER // CHUNK
GROUPS_PER_HALF = LANES // GROUP_SIZE
STATES_PER_HALF = GROUPS_PER_HALF * STATE_DIM
STATES_PER_CHUNK = 2 * STATES_PER_HALF
VMEM_LIMIT_BYTES = 60000 * 1024

TM_ROWWISE = 512
TM_CONV = 1024
TM_SCAN = 512
WIDE = 1024
TT_REORDER = 128
W_PIECES = 4

_F32 = jnp.float32
_BF16 = jnp.bfloat16


def _dot(a, b):
    return jnp.dot(a, b, preferred_element_type=_F32)


def _rmsnorm(x, g):
    return x * lax.rsqrt(jnp.mean(x * x, axis=-1, keepdims=True) + EPS) * g


def _silu(z):
    return z * jax.nn.sigmoid(z)


def _compiler_params(n_axes=2):
    return pltpu.CompilerParams(
        dimension_semantics=("arbitrary",) * n_axes,
        vmem_limit_bytes=VMEM_LIMIT_BYTES)


def _last_block_cols(n_blocks, offset=0):
    return lambda i, j: (0, offset + jnp.where(i == n_blocks - 1, j, 0))


def _conv_in_kernel(x_ref, xs_ref, g_ref, wb_ref, wc_ref, wv_ref, wz_ref, cw_ref,
                    s2s_ref, s1s_ref,
                    y_ref, n2_ref, n1_ref, ys_ref, n2s_ref, n1s_ref,
                    h_scr, hs_scr, ubuf, carry, *, rows_per_step):
    i = pl.program_id(0)
    j = pl.program_id(1)
    last = i == pl.num_programs(0) - 1
    tm = x_ref.shape[0]
    r = rows_per_step

    @pl.when(j == 0)
    def _():
        h_scr[...] = _rmsnorm(x_ref[...], g_ref[...]).astype(_BF16)

    @pl.when(i == 0)
    def _():
        carry[j] = jnp.zeros(carry.shape[1:], _F32)

    w16 = [w[...].astype(_BF16) for w in (wb_ref, wc_ref, wv_ref, wz_ref)]
    cw = cw_ref[...]

    h = h_scr[...]
    b, c, v, z = [_dot(h, w) for w in w16]
    ubuf[0:2 * r, :] = carry[j]
    ubuf[2 * r:2 * r + tm, :] = c * v
    conv = (cw[0:1, :] * ubuf[0:tm, :] + cw[1:2, :] * ubuf[r:r + tm, :]
            + cw[2:3, :] * ubuf[2 * r:2 * r + tm, :])
    y_ref[...] = (b * conv * _silu(z)).astype(_BF16)
    carry[j] = ubuf[tm:tm + 2 * r, :]

    @pl.when(last)
    def _():
        n2_ref[...] = ubuf[tm:tm + r, :]
        n1_ref[...] = ubuf[tm + r:tm + 2 * r, :]

        @pl.when(j == 0)
        def _():
            hs_scr[...] = _rmsnorm(xs_ref[...], g_ref[...]).astype(_BF16)

        hs = hs_scr[...]
        bs, cs, vs, zs = [_dot(hs, w) for w in w16]
        us = cs * vs
        s1s = s1s_ref[...]
        convs = cw[0:1, :] * s2s_ref[...] + cw[1:2, :] * s1s + cw[2:3, :] * us
        ys_ref[...] = (bs * convs * _silu(zs)).astype(_BF16)
        n2s_ref[...] = s1s
        n1s_ref[...] = us


def _conv_in(x, xs, norm_g, w_in, conv_w, state_s, layer, *, rows_per_step):
    m = x.shape[0]
    rs = xs.shape[0]
    r = rows_per_step
    tm = TM_CONV
    n_blocks = m // tm
    w_in_spec = lambda part: pl.BlockSpec(
        (None, D_MODEL, CHUNK), lambda i, j: (layer, 0, part * N_CHUNKS + j))
    live_last = lambda rows, offset=0: pl.BlockSpec(
        (rows, CHUNK), _last_block_cols(n_blocks, offset))
    y, n2, n1, ys, n2s, n1s = pl.pallas_call(
        functools.partial(_conv_in_kernel, rows_per_step=r),
        grid=(n_blocks, N_CHUNKS),
        in_specs=[
            pl.BlockSpec((tm, D_MODEL), lambda i, j: (i, 0)),
            pl.BlockSpec((rs, D_MODEL), lambda i, j: (0, 0)),
            pl.BlockSpec((None, 1, D_MODEL), lambda i, j: (layer, 0, 0)),
            w_in_spec(0), w_in_spec(1), w_in_spec(2), w_in_spec(3),
            pl.BlockSpec((None, CONV_WIDTH, CHUNK), lambda i, j: (layer, 0, j)),
            live_last(rs), live_last(rs, N_CHUNKS),
        ],
        out_specs=[
            pl.BlockSpec((tm, CHUNK), lambda i, j: (i, j)),
            live_last(r), live_last(r),
            live_last(rs), live_last(rs), live_last(rs),
        ],
        out_shape=[
            jax.ShapeDtypeStruct((m, D_INNER), _BF16),
            jax.ShapeDtypeStruct((r, D_INNER), _F32),
            jax.ShapeDtypeStruct((r, D_INNER), _F32),
            jax.ShapeDtypeStruct((rs, D_INNER), _BF16),
            jax.ShapeDtypeStruct((rs, D_INNER), _F32),
            jax.ShapeDtypeStruct((rs, D_INNER), _F32),
        ],
        scratch_shapes=[
            pltpu.VMEM((tm, D_MODEL), _BF16),
            pltpu.VMEM((rs, D_MODEL), _BF16),
            pltpu.VMEM((tm + 2 * r, CHUNK), _F32),
            pltpu.VMEM((N_CHUNKS, 2 * r, CHUNK), _F32),
        ],
        compiler_params=_compiler_params(),
        name="conv_in",
    )(x, xs, norm_g, w_in, w_in, w_in, w_in, conv_w, state_s, state_s)
    return y, ys, jnp.stack([n2, n1], axis=1), jnp.stack([n2s, n1s], axis=1)


class _StationaryWeights:
    def __init__(self, k_dim, n_cols, n_rows, layer):
        self.k_dim, self.n_cols, self.n_rows, self.layer = k_dim, n_cols, n_rows, layer
        self.piece = k_dim // W_PIECES
        self.n_steps = W_PIECES + n_cols * n_rows

    def col_row(self, s):
        t = jnp.maximum(s - W_PIECES, 0)
        return t // self.n_rows, t % self.n_rows

    def _piece_and_block(self, s):
        n, i = self.col_row(s)
        ahead = i - (self.n_rows - W_PIECES)
        piece = jnp.where(s < W_PIECES, s, jnp.where(ahead >= 0, ahead, W_PIECES - 1))
        block = jnp.where((s >= W_PIECES) & (ahead >= 0),
                          jnp.minimum(n + 1, self.n_cols - 1), n)
        return piece, block

    def weight_spec(self):
        def index_map(s):
            piece, block = self._piece_and_block(s)
            return self.layer, piece, block
        return pl.BlockSpec((None, self.piece, WIDE), index_map)

    def scratch(self):
        return pltpu.VMEM((2, self.k_dim, WIDE), _BF16)

    def rows_spec(self, tm, width_is_wide=True):
        if width_is_wide:
            return pl.BlockSpec((tm, WIDE), lambda s: self.col_row(s)[::-1])
        return pl.BlockSpec((tm, self.k_dim), lambda s: (self.col_row(s)[1], 0))

    def fixed_rows_spec(self, rows):
        return pl.BlockSpec((rows, WIDE), lambda s: (0, self.col_row(s)[0]))

    def stage(self, s, w_ref, w16):
        n, i = self.col_row(s)
        piece, _ = self._piece_and_block(s)
        active = s >= W_PIECES
        ahead = i - (self.n_rows - W_PIECES)
        dst = jnp.where(active, 1 - n % 2, 0)

        @pl.when(jnp.logical_not(active) | (ahead >= 0))
        def _():
            rows = pl.ds(pl.multiple_of(piece * self.piece, self.piece), self.piece)
            w16[dst, rows, :] = w_ref[...].astype(_BF16)

        return active, i, n % 2


def _out_proj_kernel(y_ref, ys_ref, w_ref, x_ref, xs_ref, o_ref, os_ref, w16, *, plan):
    active, i, cur = plan.stage(pl.program_id(0), w_ref, w16)

    @pl.when(active)
    def _():
        o_ref[...] = x_ref[...] + _dot(y_ref[...], w16[cur])

        @pl.when(i == plan.n_rows - 1)
        def _():
            os_ref[...] = xs_ref[...] + _dot(ys_ref[...], w16[cur])


def _out_proj(y, ys, w_out, x, xs, layer):
    m = x.shape[0]
    rs = xs.shape[0]
    tm = TM_ROWWISE
    plan = _StationaryWeights(D_INNER, D_MODEL // WIDE, m // tm, layer)
    return pl.pallas_call(
        functools.partial(_out_proj_kernel, plan=plan),
        grid=(plan.n_steps,),
        in_specs=[
            plan.rows_spec(tm, width_is_wide=False),
            pl.BlockSpec((rs, D_INNER), lambda s: (0, 0)),
            plan.weight_spec(),
            plan.rows_spec(tm), plan.fixed_rows_spec(rs),
        ],
        out_specs=[plan.rows_spec(tm), plan.fixed_rows_spec(rs)],
        out_shape=[
            jax.ShapeDtypeStruct((m, D_MODEL), _F32),
            jax.ShapeDtypeStruct((rs, D_MODEL), _F32),
        ],
        scratch_shapes=[plan.scratch()],
        compiler_params=_compiler_params(1),
        name="out_proj",
    )(y, ys, w_out, x, xs)


def _ssm_prep_kernel(are_ref, aim_ref, logdt_ref, btre_ref, btim_ref, cre_ref, cim_ref,
                     lre_ref, lim_ref, bd_ref, cd_ref):
    a_re = are_ref[...]
    a_im = aim_ref[...]
    dt = jnp.exp(logdt_ref[...])
    mag = jnp.exp(a_re * dt)
    l_re = mag * jnp.cos(a_im * dt)
    l_im = mag * jnp.sin(a_im * dt)
    lre_ref[...] = l_re
    lim_ref[...] = l_im
    n_re = l_re - 1.0
    den = a_re * a_re + a_im * a_im
    k_re = (n_re * a_re + l_im * a_im) / den
    k_im = (l_im * a_re - n_re * a_im) / den
    g = a_re.shape[0]
    expand = lambda k: jnp.broadcast_to(
        k[:, None, :], (g, GROUP_SIZE, STATE_DIM)).reshape(g * GROUP_SIZE, STATE_DIM)
    k_re = expand(k_re)
    k_im = expand(k_im)
    bt_re = btre_ref[...]
    bt_im = btim_ref[...]
    bb_re = k_re * bt_re - k_im * bt_im
    bb_im = k_re * bt_im + k_im * bt_re

    shape = (CHUNK, STATES_PER_HALF)
    log2 = lambda n: n.bit_length() - 1
    row_group = (jnp.right_shift(lax.broadcasted_iota(jnp.int32, shape, 0), log2(GROUP_SIZE))
                 & (GROUPS_PER_HALF - 1))
    col_group = jnp.right_shift(lax.broadcasted_iota(jnp.int32, shape, 1), log2(STATE_DIM))
    own = row_group == col_group
    spread = lambda v: jnp.where(own, jnp.concatenate([v] * GROUPS_PER_HALF, axis=1), 0.0)
    bd_ref[...] = jnp.concatenate([spread(bb_re), spread(bb_im)], axis=1).astype(_BF16)
    cd_ref[...] = jnp.concatenate(
        [spread(cre_ref[...]).T, -spread(cim_ref[...]).T], axis=0).astype(_BF16)


def _ssm_prep(a_re, a_im, log_dt, b_re, b_im, c_re, c_im):
    gc = CHUNK // GROUP_SIZE
    rows = N_GROUPS * GROUP_SIZE
    bt_re = b_re.transpose(0, 2, 1).reshape(rows, STATE_DIM)
    bt_im = b_im.transpose(0, 2, 1).reshape(rows, STATE_DIM)
    gp_spec = pl.BlockSpec((gc, STATE_DIM), lambda j: (j, 0))
    ghp_spec = pl.BlockSpec((CHUNK, STATE_DIM), lambda j: (j, 0))
    per_chunk = lambda a, b: pl.BlockSpec((None, a, b), lambda j: (j, 0, 0))
    return pl.pallas_call(
        _ssm_prep_kernel,
        grid=(N_CHUNKS,),
        in_specs=[gp_spec, gp_spec, pl.BlockSpec((gc, 1), lambda j: (j, 0)),
                  ghp_spec, ghp_spec, ghp_spec, ghp_spec],
        out_specs=[gp_spec, gp_spec, per_chunk(CHUNK, 2 * STATES_PER_HALF),
                   per_chunk(2 * STATES_PER_HALF, CHUNK)],
        out_shape=[
            jax.ShapeDtypeStruct((N_GROUPS, STATE_DIM), _F32),
            jax.ShapeDtypeStruct((N_GROUPS, STATE_DIM), _F32),
            jax.ShapeDtypeStruct((N_CHUNKS, CHUNK, 2 * STATES_PER_HALF), _BF16),
            jax.ShapeDtypeStruct((N_CHUNKS, 2 * STATES_PER_HALF, CHUNK), _BF16),
        ],
        compiler_params=_compiler_params(1),
        name="ssm_prep",
    )(a_re, a_im, log_dt.reshape(N_GROUPS, 1), bt_re, bt_im,
      c_re.reshape(rows, STATE_DIM), c_im.reshape(rows, STATE_DIM))


def _lam_rows(lre_ref, lim_ref, rows_per_half):
    first_half = (lax.broadcasted_iota(jnp.int32, (2 * rows_per_half, STATES_PER_HALF), 0)
                  < rows_per_half)
    return (jnp.where(first_half, lre_ref[0:1, :], lre_ref[1:2, :]),
            jnp.where(first_half, lim_ref[0:1, :], lim_ref[1:2, :]))


def _readout(x, cd_ref):
    return _dot(x.astype(_BF16), cd_ref[...])


def _ssm_scan_kernel(x_ref, xs_ref, g_ref, wu_ref, wz_ref, bd_ref,
                     cd_a_ref, lre_a_ref, lim_a_ref, d_a_ref,
                     cd_b_ref, lre_b_ref, lim_b_ref, d_b_ref,
                     s0re_ref, s0im_ref,
                     y_ref, zs_ref, nre_ref, nim_ref, ys_ref, zss_ref, nres_ref, nims_ref,
                     h_scr, hs_scr, x0, u0, z0, x1, u1, z1, carry_re, carry_im,
                     *, rows_per_step, n_blocks):
    s = pl.program_id(0)
    n_steps = n_blocks * N_CHUNKS
    step_a = jnp.minimum(s, n_steps - 1)
    step_b = jnp.maximum(s - 1, 0)
    i_a, j_a = step_a // N_CHUNKS, step_a % N_CHUNKS
    i_b, j_b = step_b // N_CHUNKS, step_b % N_CHUNKS
    tm = x_ref.shape[0]
    r = rows_per_step
    rs = xs_ref.shape[0]
    sh = STATES_PER_HALF
    assert 2 * r == SUBLANES

    @pl.when(s == 0)
    def _():
        for ref in (x1, u1, z1):
            ref[...] = jnp.zeros(ref.shape, _F32)

    @pl.when(j_a == 0)
    def _():
        h_scr[...] = _rmsnorm(x_ref[...], g_ref[...]).astype(_BF16)

    @pl.when(i_b == 0)
    def _():
        zeros = jnp.zeros((2 * r, sh), _F32)
        carry_re[j_b] = zeros
        carry_im[j_b] = zeros

    def stages(x_a, u_a, z_a, x_b, u_b, z_b):
        h = h_scr[...]
        u = _dot(h, wu_ref[...].astype(_BF16))
        z = _dot(h, wz_ref[...].astype(_BF16))
        u_a[...] = u
        z_a[...] = z
        first_lanes = lax.broadcasted_iota(jnp.int32, (tm, CHUNK), 1) < LANES
        low_rows = (lax.broadcasted_iota(jnp.int32, (tm, CHUNK), 0) & r) == 0
        u_first = jnp.where(first_lanes, u, 0.0)
        u_second = jnp.where(first_lanes, 0.0, u)
        u_down = pltpu.roll(u, r, 0)
        u_up = pltpu.roll(u, tm - r, 0)
        even = jnp.where(low_rows, u_first, jnp.where(first_lanes, 0.0, u_down))
        odd = jnp.where(low_rows, jnp.where(first_lanes, u_up, 0.0), u_second)
        x_a[...] = _dot(jnp.concatenate([even, odd], axis=0).astype(_BF16), bd_ref[...])

        l_re, l_im = _lam_rows(lre_b_ref, lim_b_ref, r)
        c_re = carry_re[j_b]
        c_im = carry_im[j_b]
        for k in range(tm // SUBLANES):
            for rows in (pl.ds(k * SUBLANES, SUBLANES), pl.ds(tm + k * SUBLANES, SUBLANES)):
                c_re, c_im = (l_re * c_re - l_im * c_im + x_b[rows, 0:sh],
                              l_re * c_im + l_im * c_re + x_b[rows, sh:2 * sh])
                x_b[rows, 0:sh] = c_re
                x_b[rows, sh:2 * sh] = c_im
        carry_re[j_b] = c_re
        carry_im[j_b] = c_im

        yy = _readout(x_b[...], cd_b_ref)
        low_rows = (lax.broadcasted_iota(jnp.int32, (tm, LANES), 0) & r) == 0
        y_even = yy[:tm]
        y_odd = yy[tm:]
        y_first = jnp.where(low_rows, y_even[:, :LANES], pltpu.roll(y_odd[:, :LANES], r, 0))
        y_second = jnp.where(low_rows, pltpu.roll(y_even[:, LANES:], tm - r, 0),
                             y_odd[:, LANES:])
        y = jnp.concatenate([y_first, y_second], axis=1)
        y_ref[...] = jax.nn.gelu(y + d_b_ref[...] * u_b[...]).astype(_BF16)
        zs_ref[...] = _silu(z_b[...]).astype(_BF16)

    @pl.when(s % 2 == 0)
    def _():
        stages(x0, u0, z0, x1, u1, z1)

    @pl.when(s % 2 == 1)
    def _():
        stages(x1, u1, z1, x0, u0, z0)

    @pl.when((i_b == n_blocks - 1) & (s > 0))
    def _():
        c_re = carry_re[j_b]
        c_im = carry_im[j_b]
        nre_ref[:, 0:sh] = c_re[0:r]
        nre_ref[:, sh:2 * sh] = c_re[r:2 * r]
        nim_ref[:, 0:sh] = c_im[0:r]
        nim_ref[:, sh:2 * sh] = c_im[r:2 * r]

    @pl.when((i_a == n_blocks - 1) & (s < n_steps))
    def _():
        @pl.when(j_a == 0)
        def _():
            hs_scr[...] = _rmsnorm(xs_ref[...], g_ref[...]).astype(_BF16)

        hs = hs_scr[...]
        us = _dot(hs, wu_ref[...].astype(_BF16))
        zz = _dot(hs, wz_ref[...].astype(_BF16))
        first = lax.broadcasted_iota(jnp.int32, (rs, CHUNK), 1) < LANES
        lhs_s = jnp.concatenate([jnp.where(first, us, 0.0), jnp.where(first, 0.0, us)], axis=0)
        bus = _dot(lhs_s.astype(_BF16), bd_ref[...])
        p_re = jnp.concatenate([s0re_ref[:, 0:sh], s0re_ref[:, sh:2 * sh]], axis=0)
        p_im = jnp.concatenate([s0im_ref[:, 0:sh], s0im_ref[:, sh:2 * sh]], axis=0)
        ls_re, ls_im = _lam_rows(lre_a_ref, lim_a_ref, rs)
        n_re = ls_re * p_re - ls_im * p_im + bus[:, :sh]
        n_im = ls_re * p_im + ls_im * p_re + bus[:, sh:]
        nres_ref[:, 0:sh] = n_re[0:rs]
        nres_ref[:, sh:2 * sh] = n_re[rs:2 * rs]
        nims_ref[:, 0:sh] = n_im[0:rs]
        nims_ref[:, sh:2 * sh] = n_im[rs:2 * rs]
        yys = _readout(jnp.concatenate([n_re, n_im], axis=1), cd_a_ref)
        y_s = jnp.concatenate([yys[:rs, :LANES], yys[rs:, LANES:]], axis=1)
        ys_ref[...] = jax.nn.gelu(y_s + d_a_ref[...] * us).astype(_BF16)
        zss_ref[...] = _silu(zz).astype(_BF16)


def _ssm_scan(x, xs, norm_g, w_in, bd, cd, lam_re, lam_im, d_skip, s0_re, s0_im, layer,
              *, rows_per_step):
    m = x.shape[0]
    rs = xs.shape[0]
    r = rows_per_step
    tm = TM_SCAN
    n_blocks = m // tm
    n_steps = n_blocks * N_CHUNKS
    sh = STATES_PER_HALF

    def stage_a(s):
        step = jnp.minimum(s, n_steps - 1)
        return step // N_CHUNKS, step % N_CHUNKS

    def stage_b(s):
        step = jnp.maximum(s - 1, 0)
        return step // N_CHUNKS, step % N_CHUNKS

    def per_chunk(stage, a, b):
        return pl.BlockSpec((None, a, b), lambda s: (stage(s)[1], 0, 0))

    def skip_spec(stage):
        return pl.BlockSpec((None, 1, CHUNK), lambda s: (layer, 0, stage(s)[1]))

    def live_last(stage, rows, cols):
        def index_map(s):
            i, j = stage(s)
            return 0, jnp.where(i == n_blocks - 1, j, 0)
        return pl.BlockSpec((rows, cols), index_map)

    act_spec = pl.BlockSpec((tm, CHUNK), lambda s: stage_b(s))
    state_rows = pltpu.VMEM((2 * tm, 2 * sh), _F32)
    chunk_rows = pltpu.VMEM((tm, CHUNK), _F32)
    return pl.pallas_call(
        functools.partial(_ssm_scan_kernel, rows_per_step=r, n_blocks=n_blocks),
        grid=(n_steps + 1,),
        in_specs=[
            pl.BlockSpec((tm, D_MODEL), lambda s: (stage_a(s)[0], 0)),
            pl.BlockSpec((rs, D_MODEL), lambda s: (0, 0)),
            pl.BlockSpec((None, 1, D_MODEL), lambda s: (layer, 0, 0)),
            pl.BlockSpec((None, D_MODEL, CHUNK), lambda s: (layer, 0, stage_a(s)[1])),
            pl.BlockSpec((None, D_MODEL, CHUNK), lambda s: (layer, 0, N_CHUNKS + stage_a(s)[1])),
            per_chunk(stage_a, CHUNK, 2 * sh),
            per_chunk(stage_a, 2 * sh, CHUNK),
            per_chunk(stage_a, 2, sh), per_chunk(stage_a, 2, sh), skip_spec(stage_a),
            per_chunk(stage_b, 2 * sh, CHUNK),
            per_chunk(stage_b, 2, sh), per_chunk(stage_b, 2, sh), skip_spec(stage_b),
            live_last(stage_a, rs, STATES_PER_CHUNK), live_last(stage_a, rs, STATES_PER_CHUNK),
        ],
        out_specs=[
            act_spec, act_spec,
            live_last(stage_b, r, STATES_PER_CHUNK), live_last(stage_b, r, STATES_PER_CHUNK),
            live_last(stage_a, rs, CHUNK), live_last(stage_a, rs, CHUNK),
            live_last(stage_a, rs, STATES_PER_CHUNK), live_last(stage_a, rs, STATES_PER_CHUNK),
        ],
        out_shape=[
            jax.ShapeDtypeStruct((m, D_INNER), _BF16),
            jax.ShapeDtypeStruct((m, D_INNER), _BF16),
            jax.ShapeDtypeStruct((r, N_STATES), _F32),
            jax.ShapeDtypeStruct((r, N_STATES), _F32),
            jax.ShapeDtypeStruct((rs, D_INNER), _BF16),
            jax.ShapeDtypeStruct((rs, D_INNER), _BF16),
            jax.ShapeDtypeStruct((rs, N_STATES), _F32),
            jax.ShapeDtypeStruct((rs, N_STATES), _F32),
        ],
        scratch_shapes=[
            pltpu.VMEM((tm, D_MODEL), _BF16),
            pltpu.VMEM((rs, D_MODEL), _BF16),
            state_rows, chunk_rows, chunk_rows,
            state_rows, chunk_rows, chunk_rows,
            pltpu.VMEM((N_CHUNKS, 2 * r, sh), _F32),
            pltpu.VMEM((N_CHUNKS, 2 * r, sh), _F32),
        ],
        compiler_params=_compiler_params(1),
        name="ssm_scan",
    )(x, xs, norm_g, w_in, w_in, bd, cd, lam_re, lam_im, d_skip,
      cd, lam_re, lam_im, d_skip, s0_re, s0_im)


def _glu_kernel(y_ref, yj_ref, zs_ref, ys_ref, yjs_ref, zss_ref, wg_ref, bg_ref, o_ref, os_ref,
                w16, *, plan):
    active, i, cur = plan.stage(pl.program_id(0), wg_ref, w16)

    def gated(y_all, y_chunk, z_chunk):
        gate = jax.nn.sigmoid(_dot(y_all, w16[cur]) + bg_ref[...])
        return (y_chunk.astype(_F32) * gate * z_chunk.astype(_F32)).astype(_BF16)

    @pl.when(active)
    def _():
        o_ref[...] = gated(y_ref[...], yj_ref[...], zs_ref[...])

        @pl.when(i == plan.n_rows - 1)
        def _():
            os_ref[...] = gated(ys_ref[...], yjs_ref[...], zss_ref[...])


def _glu(y, zs, ys, zss, w_glu, b_glu, layer):
    m = y.shape[0]
    rs = ys.shape[0]
    tm = TM_ROWWISE
    plan = _StationaryWeights(D_INNER, D_INNER // WIDE, m // tm, layer)
    rows = plan.rows_spec(tm)
    sample_rows = plan.fixed_rows_spec(rs)
    return pl.pallas_call(
        functools.partial(_glu_kernel, plan=plan),
        grid=(plan.n_steps,),
        in_specs=[
            plan.rows_spec(tm, width_is_wide=False),
            rows, rows,
            pl.BlockSpec((rs, D_INNER), lambda s: (0, 0)),
            sample_rows, sample_rows,
            plan.weight_spec(),
            pl.BlockSpec((None, 1, WIDE), lambda s: (layer, 0, plan.col_row(s)[0])),
        ],
        out_specs=[rows, sample_rows],
        out_shape=[
            jax.ShapeDtypeStruct((m, D_INNER), _BF16),
            jax.ShapeDtypeStruct((rs, D_INNER), _BF16),
        ],
        scratch_shapes=[plan.scratch()],
        compiler_params=_compiler_params(1),
        name="glu",
    )(y, y, zs, ys, ys, zss, w_glu, b_glu)


def _to_time_major_kernel(x_ref, o_ref, slab):
    nb, tt, _ = x_ref.shape
    for k in range(D_MODEL // LANES):
        lanes = slice(k * LANES, (k + 1) * LANES)
        for b in range(nb):
            slab[k, pl.ds(b, tt, stride=nb), :] = x_ref[b, :, lanes]
        o_ref[:, lanes] = slab[k]


def _to_time_major(x):
    nb, seq, _ = x.shape
    tt = TT_REORDER
    return pl.pallas_call(
        _to_time_major_kernel,
        grid=(seq // tt,),
        in_specs=[pl.BlockSpec((nb, tt, D_MODEL), lambda i: (0, i, 0))],
        out_specs=pl.BlockSpec((nb * tt, D_MODEL), lambda i: (i, 0)),
        out_shape=jax.ShapeDtypeStruct((seq * nb, D_MODEL), _F32),
        scratch_shapes=[pltpu.VMEM((D_MODEL // LANES, nb * tt, LANES), _F32)],
        compiler_params=_compiler_params(1),
        name="to_time_major",
    )(x)


def _final_norm_batch_major_kernel(x_ref, g_ref, o_ref, slab):
    nb, tt, _ = o_ref.shape
    xn = _rmsnorm(x_ref[...], g_ref[...])
    for k in range(D_MODEL // LANES):
        lanes = slice(k * LANES, (k + 1) * LANES)
        slab[k] = xn[:, lanes]
        for b in range(nb):
            o_ref[b, :, lanes] = slab[k, pl.ds(b, tt, stride=nb), :]


def _final_norm_batch_major(x, g, nb):
    seq = x.shape[0] // nb
    tt = TT_REORDER
    return pl.pallas_call(
        _final_norm_batch_major_kernel,
        grid=(seq // tt,),
        in_specs=[pl.BlockSpec((nb * tt, D_MODEL), lambda i: (i, 0)),
                  pl.BlockSpec((1, D_MODEL), lambda i: (0, 0))],
        out_specs=pl.BlockSpec((nb, tt, D_MODEL), lambda i: (0, i, 0)),
        out_shape=jax.ShapeDtypeStruct((nb, seq, D_MODEL), _F32),
        scratch_shapes=[pltpu.VMEM((D_MODEL // LANES, nb * tt, LANES), _F32)],
        compiler_params=_compiler_params(1),
        name="final_norm_batch_major",
    )(x, g)


def _final_norm_kernel(x_ref, g_ref, o_ref):
    o_ref[...] = _rmsnorm(x_ref[...], g_ref[...])


def _final_norm(x, g):
    m = x.shape[0]
    tm = min(m, TM_CONV)
    row_spec = pl.BlockSpec((tm, D_MODEL), lambda i: (i, 0))
    return pl.pallas_call(
        _final_norm_kernel,
        grid=(m // tm,),
        in_specs=[row_spec, pl.BlockSpec((1, D_MODEL), lambda i: (0, 0))],
        out_specs=row_spec,
        out_shape=jax.ShapeDtypeStruct((m, D_MODEL), _F32),
        compiler_params=_compiler_params(1),
        name="final_norm",
    )(x, g)


def kernel(x_prompt, x_sample, state_conv, state_ssm_re, state_ssm_im,
           conv_norm, conv_w_in, conv_w, conv_w_out,
           ssm_norm, ssm_w_in, ssm_a_re, ssm_a_im, ssm_log_dt, ssm_b_re, ssm_b_im,
           ssm_c_re, ssm_c_im, ssm_d, ssm_w_glu, ssm_b_glu, ssm_w_out, final_norm):
    batch, seq, _ = x_prompt.shape
    dec_batch = x_sample.shape[0]
    n_conv = conv_w_in.shape[0]
    n_ssm = ssm_w_in.shape[0]
    depth = n_conv + n_ssm

    xp = _to_time_major(x_prompt)
    xs = x_sample.reshape(dec_batch, D_MODEL)
    conv_norm3 = conv_norm.reshape(n_conv, 1, D_MODEL)
    ssm_norm3 = ssm_norm.reshape(n_ssm, 1, D_MODEL)
    ssm_d3 = ssm_d.reshape(n_ssm, 1, D_INNER)
    ssm_b_glu3 = ssm_b_glu.reshape(n_ssm, 1, D_INNER)

    conv_p, conv_s, re_p, im_p, re_s, im_s = [], [], [], [], [], []
    for layer in range(depth):
        l = layer // 2
        if layer % 2 == 0:
            y, ys, ns_p, ns_s = _conv_in(
                xp, xs, conv_norm3, conv_w_in, conv_w,
                state_conv[l].reshape(dec_batch, (CONV_WIDTH - 1) * D_INNER), l,
                rows_per_step=batch)
            xp, xs = _out_proj(y, ys, conv_w_out, xp, xs, l)
            conv_p.append(ns_p)
            conv_s.append(ns_s)
        else:
            lam_re, lam_im, bd, cd = _ssm_prep(
                ssm_a_re[l], ssm_a_im[l], ssm_log_dt[l], ssm_b_re[l], ssm_b_im[l],
                ssm_c_re[l], ssm_c_im[l])
            lam_re = lam_re.reshape(N_CHUNKS, 2, STATES_PER_HALF)
            lam_im = lam_im.reshape(N_CHUNKS, 2, STATES_PER_HALF)
            y, zs, hr_p, hi_p, ys, zss, hr_s, hi_s = _ssm_scan(
                xp, xs, ssm_norm3, ssm_w_in, bd, cd, lam_re, lam_im, ssm_d3,
                state_ssm_re[l].reshape(dec_batch, N_STATES),
                state_ssm_im[l].reshape(dec_batch, N_STATES),
                l, rows_per_step=batch)
            yy, yys = _glu(y, zs, ys, zss, ssm_w_glu, ssm_b_glu3, l)
            xp, xs = _out_proj(yy, yys, ssm_w_out, xp, xs, l)
            re_p.append(hr_p.reshape(batch, N_GROUPS, STATE_DIM))
            im_p.append(hi_p.reshape(batch, N_GROUPS, STATE_DIM))
            re_s.append(hr_s.reshape(dec_batch, N_GROUPS, STATE_DIM))
            im_s.append(hi_s.reshape(dec_batch, N_GROUPS, STATE_DIM))

    final_g = final_norm.reshape(1, D_MODEL)
    y_prompt = _final_norm_batch_major(xp, final_g, batch)
    xs = _final_norm(xs, final_g)
    y_sample = xs.reshape(dec_batch, 1, D_MODEL)
    return (y_prompt, y_sample, jnp.stack(conv_p), jnp.stack(conv_s),
            jnp.stack(re_p), jnp.stack(im_p), jnp.stack(re_s), jnp.stack(im_s))
```

```python
import functools

import jax
import jax.numpy as jnp
from jax import lax
from jax.experimental import pallas as pl
from jax.experimental.pallas import tpu as pltpu

D_MODEL = 2048
D_INNER = 2 * D_MODEL
CONV_WIDTH = 3
GROUP_SIZE = 16
N_GROUPS = D_INNER // GROUP_SIZE
STATE_DIM = 64
N_STATES = N_GROUPS * STATE_DIM
EPS = 1e-6

LANES = 128
SUBLANES = 8
CHUNK = 2 * LANES
N_CHUNKS = D_INNER // CHUNK
GROUPS_PER_HALF = LANES // GROUP_SIZE
STATES_PER_HALF = GROUPS_PER_HALF * STATE_DIM
STATES_PER_CHUNK = 2 * STATES_PER_HALF
VMEM_LIMIT_BYTES = 60000 * 1024

TM_ROWWISE = 512
TM_CONV = 1024
TM_SCAN = 1024
WIDE = 1024
TT_REORDER = 256
W_PIECES = 4

_F32 = jnp.float32
_BF16 = jnp.bfloat16


def _dot(a, b):
    return jnp.dot(a, b, preferred_element_type=_F32)


def _rmsnorm(x, g):
    return x * lax.rsqrt(jnp.mean(x * x, axis=-1, keepdims=True) + EPS) * g


def _silu(z):
    return z * jax.nn.sigmoid(z)


def _divmod_pow2(x, n):
    assert n > 0 and n & (n - 1) == 0
    return jnp.right_shift(x, n.bit_length() - 1), x & (n - 1)


def _compiler_params(n_axes=2):
    return pltpu.CompilerParams(
        dimension_semantics=("arbitrary",) * n_axes,
        vmem_limit_bytes=VMEM_LIMIT_BYTES)


def _last_block_cols(n_blocks, offset=0):
    return lambda i, j: (0, offset + jnp.where(i == n_blocks - 1, j, 0))


def _conv_in_kernel(x_ref, xs_ref, g_ref, wb_ref, wc_ref, wv_ref, wz_ref, cw_ref,
                    s2s_ref, s1s_ref,
                    y_ref, n2_ref, n1_ref, ys_ref, n2s_ref, n1s_ref,
                    h_scr, hs_scr, ubuf, carry, *, rows_per_step):
    i = pl.program_id(0)
    j = pl.program_id(1)
    last = i == pl.num_programs(0) - 1
    tm = x_ref.shape[0]
    r = rows_per_step

    @pl.when(j == 0)
    def _():
        h_scr[...] = _rmsnorm(x_ref[...], g_ref[...]).astype(_BF16)

    @pl.when(i == 0)
    def _():
        carry[j] = jnp.zeros(carry.shape[1:], _F32)

    w16 = [w[...].astype(_BF16) for w in (wb_ref, wc_ref, wv_ref, wz_ref)]
    cw = cw_ref[...]

    h = h_scr[...]
    b, c, v, z = [_dot(h, w) for w in w16]
    ubuf[0:2 * r, :] = carry[j]
    ubuf[2 * r:2 * r + tm, :] = c * v
    conv = (cw[0:1, :] * ubuf[0:tm, :] + cw[1:2, :] * ubuf[r:r + tm, :]
            + cw[2:3, :] * ubuf[2 * r:2 * r + tm, :])
    y_ref[...] = (b * conv * _silu(z)).astype(_BF16)
    carry[j] = ubuf[tm:tm + 2 * r, :]

    @pl.when(last)
    def _():
        n2_ref[...] = ubuf[tm:tm + r, :]
        n1_ref[...] = ubuf[tm + r:tm + 2 * r, :]

        @pl.when(j == 0)
        def _():
            hs_scr[...] = _rmsnorm(xs_ref[...], g_ref[...]).astype(_BF16)

        hs = hs_scr[...]
        bs, cs, vs, zs = [_dot(hs, w) for w in w16]
        us = cs * vs
        s1s = s1s_ref[...]
        convs = cw[0:1, :] * s2s_ref[...] + cw[1:2, :] * s1s + cw[2:3, :] * us
        ys_ref[...] = (bs * convs * _silu(zs)).astype(_BF16)
        n2s_ref[...] = s1s
        n1s_ref[...] = us


def _conv_in(x, xs, norm_g, w_in, conv_w, state_s, layer, *, rows_per_step):
    m = x.shape[0]
    rs = xs.shape[0]
    r = rows_per_step
    tm = TM_CONV
    n_blocks = m // tm
    w_in_spec = lambda part: pl.BlockSpec(
        (None, D_MODEL, CHUNK), lambda i, j: (layer, 0, part * N_CHUNKS + j))
    live_last = lambda rows, offset=0: pl.BlockSpec(
        (rows, CHUNK), _last_block_cols(n_blocks, offset))
    y, n2, n1, ys, n2s, n1s = pl.pallas_call(
        functools.partial(_conv_in_kernel, rows_per_step=r),
        grid=(n_blocks, N_CHUNKS),
        in_specs=[
            pl.BlockSpec((tm, D_MODEL), lambda i, j: (i, 0)),
            pl.BlockSpec((rs, D_MODEL), lambda i, j: (0, 0)),
            pl.BlockSpec((None, 1, D_MODEL), lambda i, j: (layer, 0, 0)),
            w_in_spec(0), w_in_spec(1), w_in_spec(2), w_in_spec(3),
            pl.BlockSpec((None, CONV_WIDTH, CHUNK), lambda i, j: (layer, 0, j)),
            live_last(rs), live_last(rs, N_CHUNKS),
        ],
        out_specs=[
            pl.BlockSpec((tm, CHUNK), lambda i, j: (i, j)),
            live_last(r), live_last(r),
            live_last(rs), live_last(rs), live_last(rs),
        ],
        out_shape=[
            jax.ShapeDtypeStruct((m, D_INNER), _BF16),
            jax.ShapeDtypeStruct((r, D_INNER), _F32),
            jax.ShapeDtypeStruct((r, D_INNER), _F32),
            jax.ShapeDtypeStruct((rs, D_INNER), _BF16),
            jax.ShapeDtypeStruct((rs, D_INNER), _F32),
            jax.ShapeDtypeStruct((rs, D_INNER), _F32),
        ],
        scratch_shapes=[
            pltpu.VMEM((tm, D_MODEL), _BF16),
            pltpu.VMEM((rs, D_MODEL), _BF16),
            pltpu.VMEM((tm + 2 * r, CHUNK), _F32),
            pltpu.VMEM((N_CHUNKS, 2 * r, CHUNK), _F32),
        ],
        compiler_params=_compiler_params(),
        name="conv_in",
    )(x, xs, norm_g, w_in, w_in, w_in, w_in, conv_w, state_s, state_s)
    return y, ys, jnp.stack([n2, n1], axis=1), jnp.stack([n2s, n1s], axis=1)


class _StationaryWeights:
    def __init__(self, k_dim, n_cols, n_rows, layer):
        self.k_dim, self.n_cols, self.n_rows, self.layer = k_dim, n_cols, n_rows, layer
        self.piece = k_dim // W_PIECES
        self.n_steps = W_PIECES + n_cols * n_rows

    def col_row(self, s):
        t = jnp.maximum(s - W_PIECES, 0)
        return _divmod_pow2(t, self.n_rows)

    def _piece_and_block(self, s):
        n, i = self.col_row(s)
        ahead = i - (self.n_rows - W_PIECES)
        piece = jnp.where(s < W_PIECES, s, jnp.where(ahead >= 0, ahead, W_PIECES - 1))
        block = jnp.where((s >= W_PIECES) & (ahead >= 0),
                          jnp.minimum(n + 1, self.n_cols - 1), n)
        return piece, block

    def weight_spec(self):
        def index_map(s):
            piece, block = self._piece_and_block(s)
            return self.layer, piece, block
        return pl.BlockSpec((None, self.piece, WIDE), index_map)

    def scratch(self):
        return pltpu.VMEM((2, self.k_dim, WIDE), _BF16)

    def rows_spec(self, tm, width_is_wide=True):
        if width_is_wide:
            return pl.BlockSpec((tm, WIDE), lambda s: self.col_row(s)[::-1])
        return pl.BlockSpec((tm, self.k_dim), lambda s: (self.col_row(s)[1], 0))

    def fixed_rows_spec(self, rows):
        return pl.BlockSpec((rows, WIDE), lambda s: (0, self.col_row(s)[0]))

    def stage(self, s, w_ref, w16):
        n, i = self.col_row(s)
        piece, _ = self._piece_and_block(s)
        active = s >= W_PIECES
        ahead = i - (self.n_rows - W_PIECES)
        dst = jnp.where(active, 1 - (n & 1), 0)

        @pl.when(jnp.logical_not(active) | (ahead >= 0))
        def _():
            rows = pl.ds(pl.multiple_of(piece * self.piece, self.piece), self.piece)
            w16[dst, rows, :] = w_ref[...].astype(_BF16)

        return active, i, n & 1


def _out_proj_kernel(y_ref, ys_ref, w_ref, x_ref, xs_ref, o_ref, os_ref, w16, *, plan):
    active, i, cur = plan.stage(pl.program_id(0), w_ref, w16)

    @pl.when(active)
    def _():
        o_ref[...] = x_ref[...] + _dot(y_ref[...], w16[cur])

        @pl.when(i == plan.n_rows - 1)
        def _():
            os_ref[...] = xs_ref[...] + _dot(ys_ref[...], w16[cur])


def _out_proj(y, ys, w_out, x, xs, layer):
    m = x.shape[0]
    rs = xs.shape[0]
    tm = TM_ROWWISE
    plan = _StationaryWeights(D_INNER, D_MODEL // WIDE, m // tm, layer)
    return pl.pallas_call(
        functools.partial(_out_proj_kernel, plan=plan),
        grid=(plan.n_steps,),
        in_specs=[
            plan.rows_spec(tm, width_is_wide=False),
            pl.BlockSpec((rs, D_INNER), lambda s: (0, 0)),
            plan.weight_spec(),
            plan.rows_spec(tm), plan.fixed_rows_spec(rs),
        ],
        out_specs=[plan.rows_spec(tm), plan.fixed_rows_spec(rs)],
        out_shape=[
            jax.ShapeDtypeStruct((m, D_MODEL), _F32),
            jax.ShapeDtypeStruct((rs, D_MODEL), _F32),
        ],
        scratch_shapes=[plan.scratch()],
        compiler_params=_compiler_params(1),
        name="out_proj",
    )(y, ys, w_out, x, xs)


def _ssm_prep_kernel(are_ref, aim_ref, logdt_ref, btre_ref, btim_ref, cre_ref, cim_ref,
                     lre_ref, lim_ref, bd_ref, cd_ref):
    a_re = are_ref[...]
    a_im = aim_ref[...]
    dt = jnp.exp(logdt_ref[...])
    mag = jnp.exp(a_re * dt)
    l_re = mag * jnp.cos(a_im * dt)
    l_im = mag * jnp.sin(a_im * dt)
    lre_ref[...] = l_re
    lim_ref[...] = l_im
    n_re = l_re - 1.0
    den = a_re * a_re + a_im * a_im
    k_re = (n_re * a_re + l_im * a_im) / den
    k_im = (l_im * a_re - n_re * a_im) / den
    g = a_re.shape[0]
    expand = lambda k: jnp.broadcast_to(
        k[:, None, :], (g, GROUP_SIZE, STATE_DIM)).reshape(g * GROUP_SIZE, STATE_DIM)
    k_re = expand(k_re)
    k_im = expand(k_im)
    bt_re = btre_ref[...]
    bt_im = btim_ref[...]
    bb_re = k_re * bt_re - k_im * bt_im
    bb_im = k_re * bt_im + k_im * bt_re

    shape = (CHUNK, STATES_PER_HALF)
    log2 = lambda n: n.bit_length() - 1
    row_group = (jnp.right_shift(lax.broadcasted_iota(jnp.int32, shape, 0), log2(GROUP_SIZE))
                 & (GROUPS_PER_HALF - 1))
    col_group = jnp.right_shift(lax.broadcasted_iota(jnp.int32, shape, 1), log2(STATE_DIM))
    own = row_group == col_group
    spread = lambda v: jnp.where(own, jnp.concatenate([v] * GROUPS_PER_HALF, axis=1), 0.0)
    bd_ref[...] = jnp.concatenate([spread(bb_re), spread(bb_im)], axis=1).astype(_BF16)
    cd_ref[...] = jnp.concatenate(
        [spread(cre_ref[...]).T, -spread(cim_ref[...]).T], axis=0).astype(_BF16)


def _ssm_prep(a_re, a_im, log_dt, b_re, b_im, c_re, c_im):
    gc = CHUNK // GROUP_SIZE
    rows = N_GROUPS * GROUP_SIZE
    bt_re = b_re.transpose(0, 2, 1).reshape(rows, STATE_DIM)
    bt_im = b_im.transpose(0, 2, 1).reshape(rows, STATE_DIM)
    gp_spec = pl.BlockSpec((gc, STATE_DIM), lambda j: (j, 0))
    ghp_spec = pl.BlockSpec((CHUNK, STATE_DIM), lambda j: (j, 0))
    per_chunk = lambda a, b: pl.BlockSpec((None, a, b), lambda j: (j, 0, 0))
    return pl.pallas_call(
        _ssm_prep_kernel,
        grid=(N_CHUNKS,),
        in_specs=[gp_spec, gp_spec, pl.BlockSpec((gc, 1), lambda j: (j, 0)),
                  ghp_spec, ghp_spec, ghp_spec, ghp_spec],
        out_specs=[gp_spec, gp_spec, per_chunk(CHUNK, 2 * STATES_PER_HALF),
                   per_chunk(2 * STATES_PER_HALF, CHUNK)],
        out_shape=[
            jax.ShapeDtypeStruct((N_GROUPS, STATE_DIM), _F32),
            jax.ShapeDtypeStruct((N_GROUPS, STATE_DIM), _F32),
            jax.ShapeDtypeStruct((N_CHUNKS, CHUNK, 2 * STATES_PER_HALF), _BF16),
            jax.ShapeDtypeStruct((N_CHUNKS, 2 * STATES_PER_HALF, CHUNK), _BF16),
        ],
        compiler_params=_compiler_params(1),
        name="ssm_prep",
    )(a_re, a_im, log_dt.reshape(N_GROUPS, 1), bt_re, bt_im,
      c_re.reshape(rows, STATE_DIM), c_im.reshape(rows, STATE_DIM))


def _lam_rows(lre_ref, lim_ref, rows_per_half):
    first_half = (lax.broadcasted_iota(jnp.int32, (2 * rows_per_half, STATES_PER_HALF), 0)
                  < rows_per_half)
    return (jnp.where(first_half, lre_ref[0:1, :], lre_ref[1:2, :]),
            jnp.where(first_half, lim_ref[0:1, :], lim_ref[1:2, :]))


def _readout(x, cd_ref):
    return _dot(x.astype(_BF16), cd_ref[...])


def _ssm_scan_kernel(x_ref, xs_ref, g_ref, wu_ref, wz_ref, bd_ref,
                     cd_a_ref, lre_a_ref, lim_a_ref, d_a_ref,
                     cd_b_ref, lre_b_ref, lim_b_ref, d_b_ref,
                     s0re_ref, s0im_ref,
                     y_ref, zs_ref, nre_ref, nim_ref, ys_ref, zss_ref, nres_ref, nims_ref,
                     h_scr, hs_scr, x0, u0, z0, x1, u1, z1, carry_re, carry_im,
                     *, rows_per_step, n_blocks):
    s = pl.program_id(0)
    n_steps = n_blocks * N_CHUNKS
    step_a = jnp.minimum(s, n_steps - 1)
    step_b = jnp.maximum(s - 1, 0)
    i_a, j_a = _divmod_pow2(step_a, N_CHUNKS)
    i_b, j_b = _divmod_pow2(step_b, N_CHUNKS)
    tm = x_ref.shape[0]
    r = rows_per_step
    rs = xs_ref.shape[0]
    sh = STATES_PER_HALF
    assert 2 * r == SUBLANES

    @pl.when(s == 0)
    def _():
        for ref in (x1, u1, z1):
            ref[...] = jnp.zeros(ref.shape, _F32)

    @pl.when(j_a == 0)
    def _():
        h_scr[...] = _rmsnorm(x_ref[...], g_ref[...]).astype(_BF16)

    @pl.when(i_b == 0)
    def _():
        zeros = jnp.zeros((2 * r, sh), _F32)
        carry_re[j_b] = zeros
        carry_im[j_b] = zeros

    def stages(x_a, u_a, z_a, x_b, u_b, z_b):
        h = h_scr[...]
        u = _dot(h, wu_ref[...].astype(_BF16))
        z = _dot(h, wz_ref[...].astype(_BF16))
        u_a[...] = u
        z_a[...] = z
        first_lanes = lax.broadcasted_iota(jnp.int32, (tm, CHUNK), 1) < LANES
        low_rows = (lax.broadcasted_iota(jnp.int32, (tm, CHUNK), 0) & r) == 0
        u_first = jnp.where(first_lanes, u, 0.0)
        u_second = jnp.where(first_lanes, 0.0, u)
        u_down = pltpu.roll(u, r, 0)
        u_up = pltpu.roll(u, tm - r, 0)
        even = jnp.where(low_rows, u_first, jnp.where(first_lanes, 0.0, u_down))
        odd = jnp.where(low_rows, jnp.where(first_lanes, u_up, 0.0), u_second)
        x_a[...] = _dot(jnp.concatenate([even, odd], axis=0).astype(_BF16), bd_ref[...])

        l_re, l_im = _lam_rows(lre_b_ref, lim_b_ref, r)
        c_re = carry_re[j_b]
        c_im = carry_im[j_b]
        for k in range(tm // SUBLANES):
            for rows in (pl.ds(k * SUBLANES, SUBLANES), pl.ds(tm + k * SUBLANES, SUBLANES)):
                c_re, c_im = (l_re * c_re - l_im * c_im + x_b[rows, 0:sh],
                              l_re * c_im + l_im * c_re + x_b[rows, sh:2 * sh])
                x_b[rows, 0:sh] = c_re
                x_b[rows, sh:2 * sh] = c_im
        carry_re[j_b] = c_re
        carry_im[j_b] = c_im

        yy = _readout(x_b[...], cd_b_ref)
        low_rows = (lax.broadcasted_iota(jnp.int32, (tm, LANES), 0) & r) == 0
        y_even = yy[:tm]
        y_odd = yy[tm:]
        y_first = jnp.where(low_rows, y_even[:, :LANES], pltpu.roll(y_odd[:, :LANES], r, 0))
        y_second = jnp.where(low_rows, pltpu.roll(y_even[:, LANES:], tm - r, 0),
                             y_odd[:, LANES:])
        y = jnp.concatenate([y_first, y_second], axis=1)
        y_ref[...] = jax.nn.gelu(y + d_b_ref[...] * u_b[...]).astype(_BF16)
        zs_ref[...] = _silu(z_b[...]).astype(_BF16)

    @pl.when(s & 1 == 0)
    def _():
        stages(x0, u0, z0, x1, u1, z1)

    @pl.when(s & 1 == 1)
    def _():
        stages(x1, u1, z1, x0, u0, z0)

    @pl.when((i_b == n_blocks - 1) & (s > 0))
    def _():
        c_re = carry_re[j_b]
        c_im = carry_im[j_b]
        nre_ref[:, 0:sh] = c_re[0:r]
        nre_ref[:, sh:2 * sh] = c_re[r:2 * r]
        nim_ref[:, 0:sh] = c_im[0:r]
        nim_ref[:, sh:2 * sh] = c_im[r:2 * r]

    @pl.when((i_a == n_blocks - 1) & (s < n_steps))
    def _():
        @pl.when(j_a == 0)
        def _():
            hs_scr[...] = _rmsnorm(xs_ref[...], g_ref[...]).astype(_BF16)

        hs = hs_scr[...]
        us = _dot(hs, wu_ref[...].astype(_BF16))
        zz = _dot(hs, wz_ref[...].astype(_BF16))
        first = lax.broadcasted_iota(jnp.int32, (rs, CHUNK), 1) < LANES
        lhs_s = jnp.concatenate([jnp.where(first, us, 0.0), jnp.where(first, 0.0, us)], axis=0)
        bus = _dot(lhs_s.astype(_BF16), bd_ref[...])
        p_re = jnp.concatenate([s0re_ref[:, 0:sh], s0re_ref[:, sh:2 * sh]], axis=0)
        p_im = jnp.concatenate([s0im_ref[:, 0:sh], s0im_ref[:, sh:2 * sh]], axis=0)
        ls_re, ls_im = _lam_rows(lre_a_ref, lim_a_ref, rs)
        n_re = ls_re * p_re - ls_im * p_im + bus[:, :sh]
        n_im = ls_re * p_im + ls_im * p_re + bus[:, sh:]
        nres_ref[:, 0:sh] = n_re[0:rs]
        nres_ref[:, sh:2 * sh] = n_re[rs:2 * rs]
        nims_ref[:, 0:sh] = n_im[0:rs]
        nims_ref[:, sh:2 * sh] = n_im[rs:2 * rs]
        yys = _readout(jnp.concatenate([n_re, n_im], axis=1), cd_a_ref)
        y_s = jnp.concatenate([yys[:rs, :LANES], yys[rs:, LANES:]], axis=1)
        ys_ref[...] = jax.nn.gelu(y_s + d_a_ref[...] * us).astype(_BF16)
        zss_ref[...] = _silu(zz).astype(_BF16)


def _ssm_scan(x, xs, norm_g, w_in, bd, cd, lam_re, lam_im, d_skip, s0_re, s0_im, layer,
              *, rows_per_step):
    m = x.shape[0]
    rs = xs.shape[0]
    r = rows_per_step
    tm = TM_SCAN
    n_blocks = m // tm
    n_steps = n_blocks * N_CHUNKS
    sh = STATES_PER_HALF

    def stage_a(s):
        return _divmod_pow2(jnp.minimum(s, n_steps - 1), N_CHUNKS)

    def stage_b(s):
        return _divmod_pow2(jnp.maximum(s - 1, 0), N_CHUNKS)

    def per_chunk(stage, a, b):
        return pl.BlockSpec((None, a, b), lambda s: (stage(s)[1], 0, 0))

    def skip_spec(stage):
        return pl.BlockSpec((None, 1, CHUNK), lambda s: (layer, 0, stage(s)[1]))

    def live_last(stage, rows, cols, **kwargs):
        def index_map(s):
            i, j = stage(s)
            return 0, jnp.where(i == n_blocks - 1, j, 0)
        return pl.BlockSpec((rows, cols), index_map, **kwargs)

    single = dict(pipeline_mode=pl.Buffered(1))
    act_spec = pl.BlockSpec((tm, CHUNK), lambda s: stage_b(s))
    state_rows = pltpu.VMEM((2 * tm, 2 * sh), _F32)
    chunk_rows = pltpu.VMEM((tm, CHUNK), _F32)
    return pl.pallas_call(
        functools.partial(_ssm_scan_kernel, rows_per_step=r, n_blocks=n_blocks),
        grid=(n_steps + 1,),
        in_specs=[
            pl.BlockSpec((tm, D_MODEL), lambda s: (stage_a(s)[0], 0), **single),
            pl.BlockSpec((rs, D_MODEL), lambda s: (0, 0), **single),
            pl.BlockSpec((None, 1, D_MODEL), lambda s: (layer, 0, 0)),
            pl.BlockSpec((None, D_MODEL, CHUNK), lambda s: (layer, 0, stage_a(s)[1])),
            pl.BlockSpec((None, D_MODEL, CHUNK), lambda s: (layer, 0, N_CHUNKS + stage_a(s)[1])),
            per_chunk(stage_a, CHUNK, 2 * sh),
            per_chunk(stage_a, 2 * sh, CHUNK),
            per_chunk(stage_a, 2, sh), per_chunk(stage_a, 2, sh), skip_spec(stage_a),
            per_chunk(stage_b, 2 * sh, CHUNK),
            per_chunk(stage_b, 2, sh), per_chunk(stage_b, 2, sh), skip_spec(stage_b),
            live_last(stage_a, rs, STATES_PER_CHUNK, **single),
            live_last(stage_a, rs, STATES_PER_CHUNK, **single),
        ],
        out_specs=[
            act_spec, act_spec,
            live_last(stage_b, r, STATES_PER_CHUNK), live_last(stage_b, r, STATES_PER_CHUNK),
            live_last(stage_a, rs, CHUNK), live_last(stage_a, rs, CHUNK),
            live_last(stage_a, rs, STATES_PER_CHUNK), live_last(stage_a, rs, STATES_PER_CHUNK),
        ],
        out_shape=[
            jax.ShapeDtypeStruct((m, D_INNER), _BF16),
            jax.ShapeDtypeStruct((m, D_INNER), _BF16),
            jax.ShapeDtypeStruct((r, N_STATES), _F32),
            jax.ShapeDtypeStruct((r, N_STATES), _F32),
            jax.ShapeDtypeStruct((rs, D_INNER), _BF16),
            jax.ShapeDtypeStruct((rs, D_INNER), _BF16),
            jax.ShapeDtypeStruct((rs, N_STATES), _F32),
            jax.ShapeDtypeStruct((rs, N_STATES), _F32),
        ],
        scratch_shapes=[
            pltpu.VMEM((tm, D_MODEL), _BF16),
            pltpu.VMEM((rs, D_MODEL), _BF16),
            state_rows, chunk_rows, chunk_rows,
            state_rows, chunk_rows, chunk_rows,
            pltpu.VMEM((N_CHUNKS, 2 * r, sh), _F32),
            pltpu.VMEM((N_CHUNKS, 2 * r, sh), _F32),
        ],
        compiler_params=_compiler_params(1),
        name="ssm_scan",
    )(x, xs, norm_g, w_in, w_in, bd, cd, lam_re, lam_im, d_skip,
      cd, lam_re, lam_im, d_skip, s0_re, s0_im)


def _glu_kernel(y_ref, yj_ref, zs_ref, ys_ref, yjs_ref, zss_ref, wg_ref, bg_ref, o_ref, os_ref,
                w16, *, plan):
    active, i, cur = plan.stage(pl.program_id(0), wg_ref, w16)

    def gated(y_all, y_chunk, z_chunk):
        gate = jax.nn.sigmoid(_dot(y_all, w16[cur]) + bg_ref[...])
        return (y_chunk.astype(_F32) * gate * z_chunk.astype(_F32)).astype(_BF16)

    @pl.when(active)
    def _():
        o_ref[...] = gated(y_ref[...], yj_ref[...], zs_ref[...])

        @pl.when(i == plan.n_rows - 1)
        def _():
            os_ref[...] = gated(ys_ref[...], yjs_ref[...], zss_ref[...])


def _glu(y, zs, ys, zss, w_glu, b_glu, layer):
    m = y.shape[0]
    rs = ys.shape[0]
    tm = TM_ROWWISE
    plan = _StationaryWeights(D_INNER, D_INNER // WIDE, m // tm, layer)
    rows = plan.rows_spec(tm)
    sample_rows = plan.fixed_rows_spec(rs)
    return pl.pallas_call(
        functools.partial(_glu_kernel, plan=plan),
        grid=(plan.n_steps,),
        in_specs=[
            plan.rows_spec(tm, width_is_wide=False),
            rows, rows,
            pl.BlockSpec((rs, D_INNER), lambda s: (0, 0)),
            sample_rows, sample_rows,
            plan.weight_spec(),
            pl.BlockSpec((None, 1, WIDE), lambda s: (layer, 0, plan.col_row(s)[0])),
        ],
        out_specs=[rows, sample_rows],
        out_shape=[
            jax.ShapeDtypeStruct((m, D_INNER), _BF16),
            jax.ShapeDtypeStruct((rs, D_INNER), _BF16),
        ],
        scratch_shapes=[plan.scratch()],
        compiler_params=_compiler_params(1),
        name="glu",
    )(y, y, zs, ys, ys, zss, w_glu, b_glu)


def _to_time_major_kernel(x_ref, o_ref, slab):
    nb, tt, _ = x_ref.shape
    for k in range(D_MODEL // LANES):
        lanes = slice(k * LANES, (k + 1) * LANES)
        for b in range(nb):
            slab[k, pl.ds(b, tt, stride=nb), :] = x_ref[b, :, lanes]
        o_ref[:, lanes] = slab[k]


def _to_time_major(x):
    nb, seq, _ = x.shape
    tt = TT_REORDER
    return pl.pallas_call(
        _to_time_major_kernel,
        grid=(seq // tt,),
        in_specs=[pl.BlockSpec((nb, tt, D_MODEL), lambda i: (0, i, 0))],
        out_specs=pl.BlockSpec((nb * tt, D_MODEL), lambda i: (i, 0)),
        out_shape=jax.ShapeDtypeStruct((seq * nb, D_MODEL), _F32),
        scratch_shapes=[pltpu.VMEM((D_MODEL // LANES, nb * tt, LANES), _F32)],
        compiler_params=_compiler_params(1),
        name="to_time_major",
    )(x)


def _final_norm_batch_major_kernel(x_ref, g_ref, o_ref, slab):
    nb, tt, _ = o_ref.shape
    xn = _rmsnorm(x_ref[...], g_ref[...])
    for k in range(D_MODEL // LANES):
        lanes = slice(k * LANES, (k + 1) * LANES)
        slab[k] = xn[:, lanes]
        for b in range(nb):
            o_ref[b, :, lanes] = slab[k, pl.ds(b, tt, stride=nb), :]


def _final_norm_batch_major(x, g, nb):
    seq = x.shape[0] // nb
    tt = TT_REORDER
    return pl.pallas_call(
        _final_norm_batch_major_kernel,
        grid=(seq // tt,),
        in_specs=[pl.BlockSpec((nb * tt, D_MODEL), lambda i: (i, 0)),
                  pl.BlockSpec((1, D_MODEL), lambda i: (0, 0))],
        out_specs=pl.BlockSpec((nb, tt, D_MODEL), lambda i: (0, i, 0)),
        out_shape=jax.ShapeDtypeStruct((nb, seq, D_MODEL), _F32),
        scratch_shapes=[pltpu.VMEM((D_MODEL // LANES, nb * tt, LANES), _F32)],
        compiler_params=_compiler_params(1),
        name="final_norm_batch_major",
    )(x, g)


def _final_norm_kernel(x_ref, g_ref, o_ref):
    o_ref[...] = _rmsnorm(x_ref[...], g_ref[...])


def _final_norm(x, g):
    m = x.shape[0]
    tm = min(m, TM_CONV)
    row_spec = pl.BlockSpec((tm, D_MODEL), lambda i: (i, 0))
    return pl.pallas_call(
        _final_norm_kernel,
        grid=(m // tm,),
        in_specs=[row_spec, pl.BlockSpec((1, D_MODEL), lambda i: (0, 0))],
        out_specs=row_spec,
        out_shape=jax.ShapeDtypeStruct((m, D_MODEL), _F32),
        compiler_params=_compiler_params(1),
        name="final_norm",
    )(x, g)


def kernel(x_prompt, x_sample, state_conv, state_ssm_re, state_ssm_im,
           conv_norm, conv_w_in, conv_w, conv_w_out,
           ssm_norm, ssm_w_in, ssm_a_re, ssm_a_im, ssm_log_dt, ssm_b_re, ssm_b_im,
           ssm_c_re, ssm_c_im, ssm_d, ssm_w_glu, ssm_b_glu, ssm_w_out, final_norm):
    batch, seq, _ = x_prompt.shape
    dec_batch = x_sample.shape[0]
    n_conv = conv_w_in.shape[0]
    n_ssm = ssm_w_in.shape[0]
    depth = n_conv + n_ssm

    xp = _to_time_major(x_prompt)
    xs = x_sample.reshape(dec_batch, D_MODEL)
    conv_norm3 = conv_norm.reshape(n_conv, 1, D_MODEL)
    ssm_norm3 = ssm_norm.reshape(n_ssm, 1, D_MODEL)
    ssm_d3 = ssm_d.reshape(n_ssm, 1, D_INNER)
    ssm_b_glu3 = ssm_b_glu.reshape(n_ssm, 1, D_INNER)

    conv_p, conv_s, re_p, im_p, re_s, im_s = [], [], [], [], [], []
    for layer in range(depth):
        l = layer // 2
        if layer % 2 == 0:
            y, ys, ns_p, ns_s = _conv_in(
                xp, xs, conv_norm3, conv_w_in, conv_w,
                state_conv[l].reshape(dec_batch, (CONV_WIDTH - 1) * D_INNER), l,
                rows_per_step=batch)
            xp, xs = _out_proj(y, ys, conv_w_out, xp, xs, l)
            conv_p.append(ns_p)
            conv_s.append(ns_s)
        else:
            lam_re, lam_im, bd, cd = _ssm_prep(
                ssm_a_re[l], ssm_a_im[l], ssm_log_dt[l], ssm_b_re[l], ssm_b_im[l],
                ssm_c_re[l], ssm_c_im[l])
            lam_re = lam_re.reshape(N_CHUNKS, 2, STATES_PER_HALF)
            lam_im = lam_im.reshape(N_CHUNKS, 2, STATES_PER_HALF)
            y, zs, hr_p, hi_p, ys, zss, hr_s, hi_s = _ssm_scan(
                xp, xs, ssm_norm3, ssm_w_in, bd, cd, lam_re, lam_im, ssm_d3,
                state_ssm_re[l].reshape(dec_batch, N_STATES),
                state_ssm_im[l].reshape(dec_batch, N_STATES),
                l, rows_per_step=batch)
            yy, yys = _glu(y, zs, ys, zss, ssm_w_glu, ssm_b_glu3, l)
            xp, xs = _out_proj(yy, yys, ssm_w_out, xp, xs, l)
            re_p.append(hr_p.reshape(batch, N_GROUPS, STATE_DIM))
            im_p.append(hi_p.reshape(batch, N_GROUPS, STATE_DIM))
            re_s.append(hr_s.reshape(dec_batch, N_GROUPS, STATE_DIM))
            im_s.append(hi_s.reshape(dec_batch, N_GROUPS, STATE_DIM))

    final_g = final_norm.reshape(1, D_MODEL)
    y_prompt = _final_norm_batch_major(xp, final_g, batch)
    xs = _final_norm(xs, final_g)
    y_sample = xs.reshape(dec_batch, 1, D_MODEL)
    return (y_prompt, y_sample, jnp.stack(conv_p), jnp.stack(conv_s),
            jnp.stack(re_p), jnp.stack(im_p), jnp.stack(re_s), jnp.stack(im_s))
```

```python
import functools

import jax
import jax.numpy as jnp
from jax import lax
from jax.experimental import pallas as pl
from jax.experimental.pallas import tpu as pltpu

D_MODEL = 2048
D_INNER = 2 * D_MODEL
CONV_WIDTH = 3
GROUP_SIZE = 16
N_GROUPS = D_INNER // GROUP_SIZE
STATE_DIM = 64
N_STATES = N_GROUPS * STATE_DIM
EPS = 1e-6

LANES = 128
SUBLANES = 8
CHUNK = 2 * LANES
N_CHUNKS = D_INNER // CHUNK
GROUPS_PER_HALF = LANES // GROUP_SIZE
STATES_PER_HALF = GROUPS_PER_HALF * STATE_DIM
STATES_PER_CHUNK = 2 * STATES_PER_HALF
VMEM_LIMIT_BYTES = 60000 * 1024

TM_ROWWISE = 512
TM_CONV = 1024
TM_SCAN = 512
WIDE = 1024
TT_REORDER = 256
W_PIECES = 4

_F32 = jnp.float32
_BF16 = jnp.bfloat16


def _dot(a, b):
    return jnp.dot(a, b, preferred_element_type=_F32)


def _rmsnorm(x, g):
    return x * lax.rsqrt(jnp.mean(x * x, axis=-1, keepdims=True) + EPS) * g


def _to_time_major_tile(x_ref, lanes, slab, transform=lambda b, piece: piece):
    nb, tt, _ = x_ref.shape
    for b in range(nb):
        slab[pl.ds(b, tt, stride=nb), :] = transform(b, x_ref[b, :, lanes])
    return slab[...]


def _rmsnorm_to_time_major(x_ref, g_ref, h_ref, slab):
    nb = x_ref.shape[0]
    inv_rms = []
    for b in range(nb):
        xb = x_ref[b]
        inv_rms.append(lax.rsqrt(jnp.mean(xb * xb, axis=-1, keepdims=True) + EPS))
    for k in range(D_MODEL // LANES):
        lanes = slice(k * LANES, (k + 1) * LANES)
        tile = _to_time_major_tile(
            x_ref, lanes, slab, lambda b, piece: piece * inv_rms[b] * g_ref[:, lanes])
        h_ref[:, lanes] = tile.astype(_BF16)


def _silu(z):
    return z * jax.nn.sigmoid(z)


def _divmod_pow2(x, n):
    assert n > 0 and n & (n - 1) == 0
    return jnp.right_shift(x, n.bit_length() - 1), x & (n - 1)


def _compiler_params(n_axes=2):
    return pltpu.CompilerParams(
        dimension_semantics=("arbitrary",) * n_axes,
        vmem_limit_bytes=VMEM_LIMIT_BYTES)


def _last_block_cols(n_blocks, offset=0):
    return lambda i, j: (0, offset + jnp.where(i == n_blocks - 1, j, 0))


def _conv_in_kernel(x_ref, xs_ref, g_ref, wb_ref, wc_ref, wv_ref, wz_ref, cw_ref,
                    s2s_ref, s1s_ref,
                    y_ref, n2_ref, n1_ref, ys_ref, n2s_ref, n1s_ref,
                    h_scr, hs_scr, ubuf, carry, *maybe_slab, rows_per_step):
    i = pl.program_id(0)
    j = pl.program_id(1)
    last = i == pl.num_programs(0) - 1
    tm = h_scr.shape[0]
    r = rows_per_step

    @pl.when(j == 0)
    def _():
        if maybe_slab:
            _rmsnorm_to_time_major(x_ref, g_ref, h_scr, *maybe_slab)
        else:
            h_scr[...] = _rmsnorm(x_ref[...], g_ref[...]).astype(_BF16)

    @pl.when(i == 0)
    def _():
        carry[j] = jnp.zeros(carry.shape[1:], _F32)

    w16 = [w[...].astype(_BF16) for w in (wb_ref, wc_ref, wv_ref, wz_ref)]
    cw = cw_ref[...]

    h = h_scr[...]
    b, c, v, z = [_dot(h, w) for w in w16]
    ubuf[0:2 * r, :] = carry[j]
    ubuf[2 * r:2 * r + tm, :] = c * v
    conv = (cw[0:1, :] * ubuf[0:tm, :] + cw[1:2, :] * ubuf[r:r + tm, :]
            + cw[2:3, :] * ubuf[2 * r:2 * r + tm, :])
    y_ref[...] = (b * conv * _silu(z)).astype(_BF16)
    carry[j] = ubuf[tm:tm + 2 * r, :]

    @pl.when(last)
    def _():
        n2_ref[...] = ubuf[tm:tm + r, :]
        n1_ref[...] = ubuf[tm + r:tm + 2 * r, :]

        @pl.when(j == 0)
        def _():
            hs_scr[...] = _rmsnorm(xs_ref[...], g_ref[...]).astype(_BF16)

        hs = hs_scr[...]
        bs, cs, vs, zs = [_dot(hs, w) for w in w16]
        us = cs * vs
        s1s = s1s_ref[...]
        convs = cw[0:1, :] * s2s_ref[...] + cw[1:2, :] * s1s + cw[2:3, :] * us
        ys_ref[...] = (bs * convs * _silu(zs)).astype(_BF16)
        n2s_ref[...] = s1s
        n1s_ref[...] = us


def _conv_in(x, xs, norm_g, w_in, conv_w, state_s, layer, *, rows_per_step):
    rs = xs.shape[0]
    r = rows_per_step
    tm = TM_CONV
    if x.ndim == 3:
        nb, seq, _ = x.shape
        m = nb * seq
        x_spec = pl.BlockSpec((nb, tm // nb, D_MODEL), lambda i, j: (0, i, 0))
        reorder_scratch = [pltpu.VMEM((tm, LANES), _F32)]
    else:
        m = x.shape[0]
        x_spec = pl.BlockSpec((tm, D_MODEL), lambda i, j: (i, 0))
        reorder_scratch = []
    n_blocks = m // tm
    w_in_spec = lambda part: pl.BlockSpec(
        (None, D_MODEL, CHUNK), lambda i, j: (layer, 0, part * N_CHUNKS + j))
    live_last = lambda rows, offset=0: pl.BlockSpec(
        (rows, CHUNK), _last_block_cols(n_blocks, offset))
    y, n2, n1, ys, n2s, n1s = pl.pallas_call(
        functools.partial(_conv_in_kernel, rows_per_step=r),
        grid=(n_blocks, N_CHUNKS),
        in_specs=[
            x_spec,
            pl.BlockSpec((rs, D_MODEL), lambda i, j: (0, 0)),
            pl.BlockSpec((None, 1, D_MODEL), lambda i, j: (layer, 0, 0)),
            w_in_spec(0), w_in_spec(1), w_in_spec(2), w_in_spec(3),
            pl.BlockSpec((None, CONV_WIDTH, CHUNK), lambda i, j: (layer, 0, j)),
            live_last(rs), live_last(rs, N_CHUNKS),
        ],
        out_specs=[
            pl.BlockSpec((tm, CHUNK), lambda i, j: (i, j)),
            live_last(r), live_last(r),
            live_last(rs), live_last(rs), live_last(rs),
        ],
        out_shape=[
            jax.ShapeDtypeStruct((m, D_INNER), _BF16),
            jax.ShapeDtypeStruct((r, D_INNER), _F32),
            jax.ShapeDtypeStruct((r, D_INNER), _F32),
            jax.ShapeDtypeStruct((rs, D_INNER), _BF16),
            jax.ShapeDtypeStruct((rs, D_INNER), _F32),
            jax.ShapeDtypeStruct((rs, D_INNER), _F32),
        ],
        scratch_shapes=[
            pltpu.VMEM((tm, D_MODEL), _BF16),
            pltpu.VMEM((rs, D_MODEL), _BF16),
            pltpu.VMEM((tm + 2 * r, CHUNK), _F32),
            pltpu.VMEM((N_CHUNKS, 2 * r, CHUNK), _F32),
        ] + reorder_scratch,
        compiler_params=_compiler_params(),
        name="conv_in",
    )(x, xs, norm_g, w_in, w_in, w_in, w_in, conv_w, state_s, state_s)
    return y, ys, jnp.stack([n2, n1], axis=1), jnp.stack([n2s, n1s], axis=1)


class _StationaryWeights:
    def __init__(self, k_dim, n_cols, n_rows, layer):
        self.k_dim, self.n_cols, self.n_rows, self.layer = k_dim, n_cols, n_rows, layer
        self.piece = k_dim // W_PIECES
        self.n_steps = W_PIECES + n_cols * n_rows

    def col_row(self, s):
        t = jnp.maximum(s - W_PIECES, 0)
        return _divmod_pow2(t, self.n_rows)

    def _piece_and_block(self, s):
        n, i = self.col_row(s)
        ahead = i - (self.n_rows - W_PIECES)
        piece = jnp.where(s < W_PIECES, s, jnp.where(ahead >= 0, ahead, W_PIECES - 1))
        block = jnp.where((s >= W_PIECES) & (ahead >= 0),
                          jnp.minimum(n + 1, self.n_cols - 1), n)
        return piece, block

    def weight_spec(self):
        def index_map(s):
            piece, block = self._piece_and_block(s)
            return self.layer, piece, block
        return pl.BlockSpec((None, self.piece, WIDE), index_map)

    def scratch(self):
        return pltpu.VMEM((2, self.k_dim, WIDE), _BF16)

    def rows_spec(self, tm, width_is_wide=True):
        if width_is_wide:
            return pl.BlockSpec((tm, WIDE), lambda s: self.col_row(s)[::-1])
        return pl.BlockSpec((tm, self.k_dim), lambda s: (self.col_row(s)[1], 0))

    def fixed_rows_spec(self, rows):
        return pl.BlockSpec((rows, WIDE), lambda s: (0, self.col_row(s)[0]))

    def stage(self, s, w_ref, w16):
        n, i = self.col_row(s)
        piece, _ = self._piece_and_block(s)
        active = s >= W_PIECES
        ahead = i - (self.n_rows - W_PIECES)
        dst = jnp.where(active, 1 - (n & 1), 0)

        @pl.when(jnp.logical_not(active) | (ahead >= 0))
        def _():
            rows = pl.ds(pl.multiple_of(piece * self.piece, self.piece), self.piece)
            w16[dst, rows, :] = w_ref[...].astype(_BF16)

        return active, i, n & 1


def _out_proj_kernel(y_ref, ys_ref, w_ref, x_ref, xs_ref, o_ref, os_ref, w16, *maybe_slab, plan):
    active, i, cur = plan.stage(pl.program_id(0), w_ref, w16)

    @pl.when(active)
    def _():
        update = _dot(y_ref[...], w16[cur])
        if maybe_slab:
            for k in range(WIDE // LANES):
                lanes = slice(k * LANES, (k + 1) * LANES)
                o_ref[:, lanes] = _to_time_major_tile(x_ref, lanes, *maybe_slab) + update[:, lanes]
        else:
            o_ref[...] = x_ref[...] + update

        @pl.when(i == plan.n_rows - 1)
        def _():
            os_ref[...] = xs_ref[...] + _dot(ys_ref[...], w16[cur])


def _out_proj(y, ys, w_out, x, xs, layer):
    m = y.shape[0]
    rs = xs.shape[0]
    tm = TM_ROWWISE
    plan = _StationaryWeights(D_INNER, D_MODEL // WIDE, m // tm, layer)
    if x.ndim == 3:
        nb = x.shape[0]
        x_spec = pl.BlockSpec((nb, tm // nb, WIDE), lambda s: (0,) + plan.col_row(s)[::-1])
        reorder_scratch = [pltpu.VMEM((tm, LANES), _F32)]
    else:
        x_spec = plan.rows_spec(tm)
        reorder_scratch = []
    return pl.pallas_call(
        functools.partial(_out_proj_kernel, plan=plan),
        grid=(plan.n_steps,),
        in_specs=[
            plan.rows_spec(tm, width_is_wide=False),
            pl.BlockSpec((rs, D_INNER), lambda s: (0, 0)),
            plan.weight_spec(),
            x_spec, plan.fixed_rows_spec(rs),
        ],
        out_specs=[plan.rows_spec(tm), plan.fixed_rows_spec(rs)],
        out_shape=[
            jax.ShapeDtypeStruct((m, D_MODEL), _F32),
            jax.ShapeDtypeStruct((rs, D_MODEL), _F32),
        ],
        scratch_shapes=[plan.scratch()] + reorder_scratch,
        compiler_params=_compiler_params(1),
        name="out_proj",
    )(y, ys, w_out, x, xs)


def _ssm_prep_kernel(are_ref, aim_ref, logdt_ref, btre_ref, btim_ref, cre_ref, cim_ref,
                     lre_ref, lim_ref, bd_ref, cd_ref):
    a_re = are_ref[...]
    a_im = aim_ref[...]
    dt = jnp.exp(logdt_ref[...])
    mag = jnp.exp(a_re * dt)
    l_re = mag * jnp.cos(a_im * dt)
    l_im = mag * jnp.sin(a_im * dt)
    lre_ref[...] = l_re
    lim_ref[...] = l_im
    n_re = l_re - 1.0
    den = a_re * a_re + a_im * a_im
    k_re = (n_re * a_re + l_im * a_im) / den
    k_im = (l_im * a_re - n_re * a_im) / den
    g = a_re.shape[0]
    expand = lambda k: jnp.broadcast_to(
        k[:, None, :], (g, GROUP_SIZE, STATE_DIM)).reshape(g * GROUP_SIZE, STATE_DIM)
    k_re = expand(k_re)
    k_im = expand(k_im)
    bt_re = btre_ref[...]
    bt_im = btim_ref[...]
    bb_re = k_re * bt_re - k_im * bt_im
    bb_im = k_re * bt_im + k_im * bt_re

    shape = (CHUNK, STATES_PER_HALF)
    log2 = lambda n: n.bit_length() - 1
    row_group = (jnp.right_shift(lax.broadcasted_iota(jnp.int32, shape, 0), log2(GROUP_SIZE))
                 & (GROUPS_PER_HALF - 1))
    col_group = jnp.right_shift(lax.broadcasted_iota(jnp.int32, shape, 1), log2(STATE_DIM))
    own = row_group == col_group
    spread = lambda v: jnp.where(own, jnp.concatenate([v] * GROUPS_PER_HALF, axis=1), 0.0)
    bd_ref[...] = jnp.concatenate([spread(bb_re), spread(bb_im)], axis=1).astype(_BF16)
    cd_ref[...] = jnp.concatenate(
        [spread(cre_ref[...]).T, -spread(cim_ref[...]).T], axis=0).astype(_BF16)


def _ssm_prep(a_re, a_im, log_dt, b_re, b_im, c_re, c_im):
    gc = CHUNK // GROUP_SIZE
    rows = N_GROUPS * GROUP_SIZE
    bt_re = b_re.transpose(0, 2, 1).reshape(rows, STATE_DIM)
    bt_im = b_im.transpose(0, 2, 1).reshape(rows, STATE_DIM)
    gp_spec = pl.BlockSpec((gc, STATE_DIM), lambda j: (j, 0))
    ghp_spec = pl.BlockSpec((CHUNK, STATE_DIM), lambda j: (j, 0))
    per_chunk = lambda a, b: pl.BlockSpec((None, a, b), lambda j: (j, 0, 0))
    return pl.pallas_call(
        _ssm_prep_kernel,
        grid=(N_CHUNKS,),
        in_specs=[gp_spec, gp_spec, pl.BlockSpec((gc, 1), lambda j: (j, 0)),
                  ghp_spec, ghp_spec, ghp_spec, ghp_spec],
        out_specs=[gp_spec, gp_spec, per_chunk(CHUNK, 2 * STATES_PER_HALF),
                   per_chunk(2 * STATES_PER_HALF, CHUNK)],
        out_shape=[
            jax.ShapeDtypeStruct((N_GROUPS, STATE_DIM), _F32),
            jax.ShapeDtypeStruct((N_GROUPS, STATE_DIM), _F32),
            jax.ShapeDtypeStruct((N_CHUNKS, CHUNK, 2 * STATES_PER_HALF), _BF16),
            jax.ShapeDtypeStruct((N_CHUNKS, 2 * STATES_PER_HALF, CHUNK), _BF16),
        ],
        compiler_params=_compiler_params(1),
        name="ssm_prep",
    )(a_re, a_im, log_dt.reshape(N_GROUPS, 1), bt_re, bt_im,
      c_re.reshape(rows, STATE_DIM), c_im.reshape(rows, STATE_DIM))


def _lam_rows(lre_ref, lim_ref, rows_per_half):
    first_half = (lax.broadcasted_iota(jnp.int32, (2 * rows_per_half, STATES_PER_HALF), 0)
                  < rows_per_half)
    return (jnp.where(first_half, lre_ref[0:1, :], lre_ref[1:2, :]),
            jnp.where(first_half, lim_ref[0:1, :], lim_ref[1:2, :]))


def _readout(x, cd_ref):
    return _dot(x.astype(_BF16), cd_ref[...])


def _ssm_scan_kernel(x_ref, xs_ref, g_ref, wu_ref, wz_ref, bd_ref,
                     cd_a_ref, lre_a_ref, lim_a_ref, d_a_ref,
                     cd_b_ref, lre_b_ref, lim_b_ref, d_b_ref,
                     s0re_ref, s0im_ref,
                     y_ref, zs_ref, nre_ref, nim_ref, ys_ref, zss_ref, nres_ref, nims_ref,
                     h_scr, hs_scr, x0, u0, z0, x1, u1, z1, carry_re, carry_im,
                     *, rows_per_step, n_blocks):
    s = pl.program_id(0)
    n_steps = n_blocks * N_CHUNKS
    step_a = jnp.minimum(s, n_steps - 1)
    step_b = jnp.maximum(s - 1, 0)
    i_a, j_a = _divmod_pow2(step_a, N_CHUNKS)
    i_b, j_b = _divmod_pow2(step_b, N_CHUNKS)
    tm = x_ref.shape[0]
    r = rows_per_step
    rs = xs_ref.shape[0]
    sh = STATES_PER_HALF
    assert 2 * r == SUBLANES

    @pl.when(s == 0)
    def _():
        for ref in (x1, u1, z1):
            ref[...] = jnp.zeros(ref.shape, _F32)

    @pl.when(j_a == 0)
    def _():
        h_scr[...] = _rmsnorm(x_ref[...], g_ref[...]).astype(_BF16)

    @pl.when(i_b == 0)
    def _():
        zeros = jnp.zeros((2 * r, sh), _F32)
        carry_re[j_b] = zeros
        carry_im[j_b] = zeros

    def stages(x_a, u_a, z_a, x_b, u_b, z_b):
        h = h_scr[...]
        u = _dot(h, wu_ref[...].astype(_BF16))
        z = _dot(h, wz_ref[...].astype(_BF16))
        u_a[...] = u
        z_a[...] = z
        first_lanes = lax.broadcasted_iota(jnp.int32, (tm, CHUNK), 1) < LANES
        low_rows = (lax.broadcasted_iota(jnp.int32, (tm, CHUNK), 0) & r) == 0
        u_first = jnp.where(first_lanes, u, 0.0)
        u_second = jnp.where(first_lanes, 0.0, u)
        u_down = pltpu.roll(u, r, 0)
        u_up = pltpu.roll(u, tm - r, 0)
        even = jnp.where(low_rows, u_first, jnp.where(first_lanes, 0.0, u_down))
        odd = jnp.where(low_rows, jnp.where(first_lanes, u_up, 0.0), u_second)
        x_a[...] = _dot(jnp.concatenate([even, odd], axis=0).astype(_BF16), bd_ref[...])

        l_re, l_im = _lam_rows(lre_b_ref, lim_b_ref, r)
        c_re = carry_re[j_b]
        c_im = carry_im[j_b]
        for k in range(tm // SUBLANES):
            for rows in (pl.ds(k * SUBLANES, SUBLANES), pl.ds(tm + k * SUBLANES, SUBLANES)):
                c_re, c_im = (l_re * c_re - l_im * c_im + x_b[rows, 0:sh],
                              l_re * c_im + l_im * c_re + x_b[rows, sh:2 * sh])
                x_b[rows, 0:sh] = c_re
                x_b[rows, sh:2 * sh] = c_im
        carry_re[j_b] = c_re
        carry_im[j_b] = c_im

        yy = _readout(x_b[...], cd_b_ref)
        low_rows = (lax.broadcasted_iota(jnp.int32, (tm, LANES), 0) & r) == 0
        y_even = yy[:tm]
        y_odd = yy[tm:]
        y_first = jnp.where(low_rows, y_even[:, :LANES], pltpu.roll(y_odd[:, :LANES], r, 0))
        y_second = jnp.where(low_rows, pltpu.roll(y_even[:, LANES:], tm - r, 0),
                             y_odd[:, LANES:])
        y = jnp.concatenate([y_first, y_second], axis=1)
        y_ref[...] = jax.nn.gelu(y + d_b_ref[...] * u_b[...]).astype(_BF16)
        zs_ref[...] = _silu(z_b[...]).astype(_BF16)

    @pl.when(s & 1 == 0)
    def _():
        stages(x0, u0, z0, x1, u1, z1)

    @pl.when(s & 1 == 1)
    def _():
        stages(x1, u1, z1, x0, u0, z0)

    @pl.when((i_b == n_blocks - 1) & (s > 0))
    def _():
        c_re = carry_re[j_b]
        c_im = carry_im[j_b]
        nre_ref[:, 0:sh] = c_re[0:r]
        nre_ref[:, sh:2 * sh] = c_re[r:2 * r]
        nim_ref[:, 0:sh] = c_im[0:r]
        nim_ref[:, sh:2 * sh] = c_im[r:2 * r]

    @pl.when((i_a == n_blocks - 1) & (s < n_steps))
    def _():
        @pl.when(j_a == 0)
        def _():
            hs_scr[...] = _rmsnorm(xs_ref[...], g_ref[...]).astype(_BF16)

        hs = hs_scr[...]
        us = _dot(hs, wu_ref[...].astype(_BF16))
        zz = _dot(hs, wz_ref[...].astype(_BF16))
        first = lax.broadcasted_iota(jnp.int32, (rs, CHUNK), 1) < LANES
        lhs_s = jnp.concatenate([jnp.where(first, us, 0.0), jnp.where(first, 0.0, us)], axis=0)
        bus = _dot(lhs_s.astype(_BF16), bd_ref[...])
        p_re = jnp.concatenate([s0re_ref[:, 0:sh], s0re_ref[:, sh:2 * sh]], axis=0)
        p_im = jnp.concatenate([s0im_ref[:, 0:sh], s0im_ref[:, sh:2 * sh]], axis=0)
        ls_re, ls_im = _lam_rows(lre_a_ref, lim_a_ref, rs)
        n_re = ls_re * p_re - ls_im * p_im + bus[:, :sh]
        n_im = ls_re * p_im + ls_im * p_re + bus[:, sh:]
        nres_ref[:, 0:sh] = n_re[0:rs]
        nres_ref[:, sh:2 * sh] = n_re[rs:2 * rs]
        nims_ref[:, 0:sh] = n_im[0:rs]
        nims_ref[:, sh:2 * sh] = n_im[rs:2 * rs]
        yys = _readout(jnp.concatenate([n_re, n_im], axis=1), cd_a_ref)
        y_s = jnp.concatenate([yys[:rs, :LANES], yys[rs:, LANES:]], axis=1)
        ys_ref[...] = jax.nn.gelu(y_s + d_a_ref[...] * us).astype(_BF16)
        zss_ref[...] = _silu(zz).astype(_BF16)


def _ssm_scan(x, xs, norm_g, w_in, bd, cd, lam_re, lam_im, d_skip, s0_re, s0_im, layer,
              *, rows_per_step):
    m = x.shape[0]
    rs = xs.shape[0]
    r = rows_per_step
    tm = TM_SCAN
    n_blocks = m // tm
    n_steps = n_blocks * N_CHUNKS
    sh = STATES_PER_HALF

    def stage_a(s):
        return _divmod_pow2(jnp.minimum(s, n_steps - 1), N_CHUNKS)

    def stage_b(s):
        return _divmod_pow2(jnp.maximum(s - 1, 0), N_CHUNKS)

    def per_chunk(stage, a, b):
        return pl.BlockSpec((None, a, b), lambda s: (stage(s)[1], 0, 0))

    def skip_spec(stage):
        return pl.BlockSpec((None, 1, CHUNK), lambda s: (layer, 0, stage(s)[1]))

    def live_last(stage, rows, cols):
        def index_map(s):
            i, j = stage(s)
            return 0, jnp.where(i == n_blocks - 1, j, 0)
        return pl.BlockSpec((rows, cols), index_map)

    act_spec = pl.BlockSpec((tm, CHUNK), lambda s: stage_b(s))
    state_rows = pltpu.VMEM((2 * tm, 2 * sh), _F32)
    chunk_rows = pltpu.VMEM((tm, CHUNK), _F32)
    return pl.pallas_call(
        functools.partial(_ssm_scan_kernel, rows_per_step=r, n_blocks=n_blocks),
        grid=(n_steps + 1,),
        in_specs=[
            pl.BlockSpec((tm, D_MODEL), lambda s: (stage_a(s)[0], 0)),
            pl.BlockSpec((rs, D_MODEL), lambda s: (0, 0)),
            pl.BlockSpec((None, 1, D_MODEL), lambda s: (layer, 0, 0)),
            pl.BlockSpec((None, D_MODEL, CHUNK), lambda s: (layer, 0, stage_a(s)[1])),
            pl.BlockSpec((None, D_MODEL, CHUNK), lambda s: (layer, 0, N_CHUNKS + stage_a(s)[1])),
            per_chunk(stage_a, CHUNK, 2 * sh),
            per_chunk(stage_a, 2 * sh, CHUNK),
            per_chunk(stage_a, 2, sh), per_chunk(stage_a, 2, sh), skip_spec(stage_a),
            per_chunk(stage_b, 2 * sh, CHUNK),
            per_chunk(stage_b, 2, sh), per_chunk(stage_b, 2, sh), skip_spec(stage_b),
            live_last(stage_a, rs, STATES_PER_CHUNK), live_last(stage_a, rs, STATES_PER_CHUNK),
        ],
        out_specs=[
            act_spec, act_spec,
            live_last(stage_b, r, STATES_PER_CHUNK), live_last(stage_b, r, STATES_PER_CHUNK),
            live_last(stage_a, rs, CHUNK), live_last(stage_a, rs, CHUNK),
            live_last(stage_a, rs, STATES_PER_CHUNK), live_last(stage_a, rs, STATES_PER_CHUNK),
        ],
        out_shape=[
            jax.ShapeDtypeStruct((m, D_INNER), _BF16),
            jax.ShapeDtypeStruct((m, D_INNER), _BF16),
            jax.ShapeDtypeStruct((r, N_STATES), _F32),
            jax.ShapeDtypeStruct((r, N_STATES), _F32),
            jax.ShapeDtypeStruct((rs, D_INNER), _BF16),
            jax.ShapeDtypeStruct((rs, D_INNER), _BF16),
            jax.ShapeDtypeStruct((rs, N_STATES), _F32),
            jax.ShapeDtypeStruct((rs, N_STATES), _F32),
        ],
        scratch_shapes=[
            pltpu.VMEM((tm, D_MODEL), _BF16),
            pltpu.VMEM((rs, D_MODEL), _BF16),
            state_rows, chunk_rows, chunk_rows,
            state_rows, chunk_rows, chunk_rows,
            pltpu.VMEM((N_CHUNKS, 2 * r, sh), _F32),
            pltpu.VMEM((N_CHUNKS, 2 * r, sh), _F32),
        ],
        compiler_params=_compiler_params(1),
        name="ssm_scan",
    )(x, xs, norm_g, w_in, w_in, bd, cd, lam_re, lam_im, d_skip,
      cd, lam_re, lam_im, d_skip, s0_re, s0_im)


def _glu_kernel(y_ref, yj_ref, zs_ref, ys_ref, yjs_ref, zss_ref, wg_ref, bg_ref, o_ref, os_ref,
                w16, *, plan):
    active, i, cur = plan.stage(pl.program_id(0), wg_ref, w16)

    def gated(y_all, y_chunk, z_chunk):
        gate = jax.nn.sigmoid(_dot(y_all, w16[cur]) + bg_ref[...])
        return (y_chunk.astype(_F32) * gate * z_chunk.astype(_F32)).astype(_BF16)

    @pl.when(active)
    def _():
        o_ref[...] = gated(y_ref[...], yj_ref[...], zs_ref[...])

        @pl.when(i == plan.n_rows - 1)
        def _():
            os_ref[...] = gated(ys_ref[...], yjs_ref[...], zss_ref[...])


def _glu(y, zs, ys, zss, w_glu, b_glu, layer):
    m = y.shape[0]
    rs = ys.shape[0]
    tm = TM_ROWWISE
    plan = _StationaryWeights(D_INNER, D_INNER // WIDE, m // tm, layer)
    rows = plan.rows_spec(tm)
    sample_rows = plan.fixed_rows_spec(rs)
    return pl.pallas_call(
        functools.partial(_glu_kernel, plan=plan),
        grid=(plan.n_steps,),
        in_specs=[
            plan.rows_spec(tm, width_is_wide=False),
            rows, rows,
            pl.BlockSpec((rs, D_INNER), lambda s: (0, 0)),
            sample_rows, sample_rows,
            plan.weight_spec(),
            pl.BlockSpec((None, 1, WIDE), lambda s: (layer, 0, plan.col_row(s)[0])),
        ],
        out_specs=[rows, sample_rows],
        out_shape=[
            jax.ShapeDtypeStruct((m, D_INNER), _BF16),
            jax.ShapeDtypeStruct((rs, D_INNER), _BF16),
        ],
        scratch_shapes=[plan.scratch()],
        compiler_params=_compiler_params(1),
        name="glu",
    )(y, y, zs, ys, ys, zss, w_glu, b_glu)


def _final_norm_batch_major_kernel(x_ref, g_ref, o_ref, slab):
    nb, tt, _ = o_ref.shape
    xn = _rmsnorm(x_ref[...], g_ref[...])
    for k in range(D_MODEL // LANES):
        lanes = slice(k * LANES, (k + 1) * LANES)
        slab[k] = xn[:, lanes]
        for b in range(nb):
            o_ref[b, :, lanes] = slab[k, pl.ds(b, tt, stride=nb), :]


def _final_norm_batch_major(x, g, nb):
    seq = x.shape[0] // nb
    tt = TT_REORDER
    return pl.pallas_call(
        _final_norm_batch_major_kernel,
        grid=(seq // tt,),
        in_specs=[pl.BlockSpec((nb * tt, D_MODEL), lambda i: (i, 0)),
                  pl.BlockSpec((1, D_MODEL), lambda i: (0, 0))],
        out_specs=pl.BlockSpec((nb, tt, D_MODEL), lambda i: (0, i, 0)),
        out_shape=jax.ShapeDtypeStruct((nb, seq, D_MODEL), _F32),
        scratch_shapes=[pltpu.VMEM((D_MODEL // LANES, nb * tt, LANES), _F32)],
        compiler_params=_compiler_params(1),
        name="final_norm_batch_major",
    )(x, g)


def _final_norm_kernel(x_ref, g_ref, o_ref):
    o_ref[...] = _rmsnorm(x_ref[...], g_ref[...])


def _final_norm(x, g):
    m = x.shape[0]
    tm = min(m, TM_CONV)
    row_spec = pl.BlockSpec((tm, D_MODEL), lambda i: (i, 0))
    return pl.pallas_call(
        _final_norm_kernel,
        grid=(m // tm,),
        in_specs=[row_spec, pl.BlockSpec((1, D_MODEL), lambda i: (0, 0))],
        out_specs=row_spec,
        out_shape=jax.ShapeDtypeStruct((m, D_MODEL), _F32),
        compiler_params=_compiler_params(1),
        name="final_norm",
    )(x, g)


def kernel(x_prompt, x_sample, state_conv, state_ssm_re, state_ssm_im,
           conv_norm, conv_w_in, conv_w, conv_w_out,
           ssm_norm, ssm_w_in, ssm_a_re, ssm_a_im, ssm_log_dt, ssm_b_re, ssm_b_im,
           ssm_c_re, ssm_c_im, ssm_d, ssm_w_glu, ssm_b_glu, ssm_w_out, final_norm):
    batch, seq, _ = x_prompt.shape
    dec_batch = x_sample.shape[0]
    n_conv = conv_w_in.shape[0]
    n_ssm = ssm_w_in.shape[0]
    depth = n_conv + n_ssm

    xp = x_prompt
    xs = x_sample.reshape(dec_batch, D_MODEL)
    conv_norm3 = conv_norm.reshape(n_conv, 1, D_MODEL)
    ssm_norm3 = ssm_norm.reshape(n_ssm, 1, D_MODEL)
    ssm_d3 = ssm_d.reshape(n_ssm, 1, D_INNER)
    ssm_b_glu3 = ssm_b_glu.reshape(n_ssm, 1, D_INNER)

    conv_p, conv_s, re_p, im_p, re_s, im_s = [], [], [], [], [], []
    for layer in range(depth):
        l = layer // 2
        if layer % 2 == 0:
            y, ys, ns_p, ns_s = _conv_in(
                xp, xs, conv_norm3, conv_w_in, conv_w,
                state_conv[l].reshape(dec_batch, (CONV_WIDTH - 1) * D_INNER), l,
                rows_per_step=batch)
            xp, xs = _out_proj(y, ys, conv_w_out, xp, xs, l)
            conv_p.append(ns_p)
            conv_s.append(ns_s)
        else:
            lam_re, lam_im, bd, cd = _ssm_prep(
                ssm_a_re[l], ssm_a_im[l], ssm_log_dt[l], ssm_b_re[l], ssm_b_im[l],
                ssm_c_re[l], ssm_c_im[l])
            lam_re = lam_re.reshape(N_CHUNKS, 2, STATES_PER_HALF)
            lam_im = lam_im.reshape(N_CHUNKS, 2, STATES_PER_HALF)
            y, zs, hr_p, hi_p, ys, zss, hr_s, hi_s = _ssm_scan(
                xp, xs, ssm_norm3, ssm_w_in, bd, cd, lam_re, lam_im, ssm_d3,
                state_ssm_re[l].reshape(dec_batch, N_STATES),
                state_ssm_im[l].reshape(dec_batch, N_STATES),
                l, rows_per_step=batch)
            yy, yys = _glu(y, zs, ys, zss, ssm_w_glu, ssm_b_glu3, l)
            xp, xs = _out_proj(yy, yys, ssm_w_out, xp, xs, l)
            re_p.append(hr_p.reshape(batch, N_GROUPS, STATE_DIM))
            im_p.append(hi_p.reshape(batch, N_GROUPS, STATE_DIM))
            re_s.append(hr_s.reshape(dec_batch, N_GROUPS, STATE_DIM))
            im_s.append(hi_s.reshape(dec_batch, N_GROUPS, STATE_DIM))

    final_g = final_norm.reshape(1, D_MODEL)
    y_prompt = _final_norm_batch_major(xp, final_g, batch)
    xs = _final_norm(xs, final_g)
    y_sample = xs.reshape(dec_batch, 1, D_MODEL)
    return (y_prompt, y_sample, jnp.stack(conv_p), jnp.stack(conv_s),
            jnp.stack(re_p), jnp.stack(im_p), jnp.stack(re_s), jnp.stack(im_s))
```

```python
import functools

import jax
import jax.numpy as jnp
from jax import lax
from jax.experimental import pallas as pl
from jax.experimental.pallas import tpu as pltpu

D_MODEL = 2048
D_INNER = 2 * D_MODEL
CONV_WIDTH = 3
GROUP_SIZE = 16
N_GROUPS = D_INNER // GROUP_SIZE
STATE_DIM = 64
N_STATES = N_GROUPS * STATE_DIM
EPS = 1e-6

LANES = 128
SUBLANES = 8
CHUNK = 2 * LANES
N_CHUNKS = D_INNER // CHUNK
GROUPS_PER_HALF = LANES // GROUP_SIZE
STATES_PER_HALF = GROUPS_PER_HALF * STATE_DIM
STATES_PER_CHUNK = 2 * STATES_PER_HALF
VMEM_LIMIT_BYTES = 60000 * 1024

TM_ROWWISE = 512
TM_CONV = 1024
TM_SCAN = 512
WIDE = 1024
TT_REORDER = 256
W_PIECES = 4
SCAN_PIECES = 4

_F32 = jnp.float32
_BF16 = jnp.bfloat16


def _dot(a, b):
    return jnp.dot(a, b, preferred_element_type=_F32)


def _rmsnorm(x, g):
    return x * lax.rsqrt(jnp.mean(x * x, axis=-1, keepdims=True) + EPS) * g


def _to_time_major_tile(x_ref, lanes, slab, transform=lambda b, piece: piece):
    nb, tt, _ = x_ref.shape
    for b in range(nb):
        slab[pl.ds(b, tt, stride=nb), :] = transform(b, x_ref[b, :, lanes])
    return slab[...]


def _rmsnorm_to_time_major(x_ref, g_ref, h_ref, slab):
    nb = x_ref.shape[0]
    inv_rms = []
    for b in range(nb):
        xb = x_ref[b]
        inv_rms.append(lax.rsqrt(jnp.mean(xb * xb, axis=-1, keepdims=True) + EPS))
    for k in range(D_MODEL // LANES):
        lanes = slice(k * LANES, (k + 1) * LANES)
        tile = _to_time_major_tile(
            x_ref, lanes, slab, lambda b, piece: piece * inv_rms[b] * g_ref[:, lanes])
        h_ref[:, lanes] = tile.astype(_BF16)


def _silu(z):
    return z * jax.nn.sigmoid(z)


def _divmod_pow2(x, n):
    assert n > 0 and n & (n - 1) == 0
    return jnp.right_shift(x, n.bit_length() - 1), x & (n - 1)


def _compiler_params(n_axes=2):
    return pltpu.CompilerParams(
        dimension_semantics=("arbitrary",) * n_axes,
        vmem_limit_bytes=VMEM_LIMIT_BYTES)


def _last_block_cols(n_blocks, offset=0):
    return lambda i, j: (0, offset + jnp.where(i == n_blocks - 1, j, 0))


def _conv_in_kernel(x_ref, xs_ref, g_ref, wb_ref, wc_ref, wv_ref, wz_ref, cw_ref,
                    s2s_ref, s1s_ref,
                    y_ref, n2_ref, n1_ref, ys_ref, n2s_ref, n1s_ref,
                    h_scr, hs_scr, ubuf, carry, *maybe_slab, rows_per_step):
    i = pl.program_id(0)
    j = pl.program_id(1)
    last = i == pl.num_programs(0) - 1
    tm = h_scr.shape[0]
    r = rows_per_step

    @pl.when(j == 0)
    def _():
        if maybe_slab:
            _rmsnorm_to_time_major(x_ref, g_ref, h_scr, *maybe_slab)
        else:
            h_scr[...] = _rmsnorm(x_ref[...], g_ref[...]).astype(_BF16)

    @pl.when(i == 0)
    def _():
        carry[j] = jnp.zeros(carry.shape[1:], _F32)

    w16 = [w[...].astype(_BF16) for w in (wb_ref, wc_ref, wv_ref, wz_ref)]
    cw = cw_ref[...]

    h = h_scr[...]
    c = _dot(h, w16[1])
    v = _dot(h, w16[2])
    ubuf[0:2 * r, :] = carry[j]
    ubuf[2 * r:2 * r + tm, :] = c * v
    conv = (cw[0:1, :] * ubuf[0:tm, :] + cw[1:2, :] * ubuf[r:r + tm, :]
            + cw[2:3, :] * ubuf[2 * r:2 * r + tm, :])
    gate = _silu(_dot(h, w16[3]))
    b = _dot(h, w16[0])
    y_ref[...] = (b * conv * gate).astype(_BF16)
    carry[j] = ubuf[tm:tm + 2 * r, :]

    @pl.when(last)
    def _():
        n2_ref[...] = ubuf[tm:tm + r, :]
        n1_ref[...] = ubuf[tm + r:tm + 2 * r, :]

        @pl.when(j == 0)
        def _():
            hs_scr[...] = _rmsnorm(xs_ref[...], g_ref[...]).astype(_BF16)

        hs = hs_scr[...]
        bs, cs, vs, zs = [_dot(hs, w) for w in w16]
        us = cs * vs
        s1s = s1s_ref[...]
        convs = cw[0:1, :] * s2s_ref[...] + cw[1:2, :] * s1s + cw[2:3, :] * us
        ys_ref[...] = (bs * convs * _silu(zs)).astype(_BF16)
        n2s_ref[...] = s1s
        n1s_ref[...] = us


def _conv_in(x, xs, norm_g, w_in, conv_w, state_s, layer, *, rows_per_step):
    rs = xs.shape[0]
    r = rows_per_step
    tm = TM_CONV
    if x.ndim == 3:
        nb, seq, _ = x.shape
        m = nb * seq
        x_spec = pl.BlockSpec((nb, tm // nb, D_MODEL), lambda i, j: (0, i, 0))
        reorder_scratch = [pltpu.VMEM((tm, LANES), _F32)]
    else:
        m = x.shape[0]
        x_spec = pl.BlockSpec((tm, D_MODEL), lambda i, j: (i, 0))
        reorder_scratch = []
    n_blocks = m // tm
    w_in_spec = lambda part: pl.BlockSpec(
        (None, D_MODEL, CHUNK), lambda i, j: (layer, 0, part * N_CHUNKS + j))
    live_last = lambda rows, offset=0: pl.BlockSpec(
        (rows, CHUNK), _last_block_cols(n_blocks, offset))
    y, n2, n1, ys, n2s, n1s = pl.pallas_call(
        functools.partial(_conv_in_kernel, rows_per_step=r),
        grid=(n_blocks, N_CHUNKS),
        in_specs=[
            x_spec,
            pl.BlockSpec((rs, D_MODEL), lambda i, j: (0, 0)),
            pl.BlockSpec((None, 1, D_MODEL), lambda i, j: (layer, 0, 0)),
            w_in_spec(0), w_in_spec(1), w_in_spec(2), w_in_spec(3),
            pl.BlockSpec((None, CONV_WIDTH, CHUNK), lambda i, j: (layer, 0, j)),
            live_last(rs), live_last(rs, N_CHUNKS),
        ],
        out_specs=[
            pl.BlockSpec((tm, CHUNK), lambda i, j: (i, j)),
            live_last(r), live_last(r),
            live_last(rs), live_last(rs), live_last(rs),
        ],
        out_shape=[
            jax.ShapeDtypeStruct((m, D_INNER), _BF16),
            jax.ShapeDtypeStruct((r, D_INNER), _F32),
            jax.ShapeDtypeStruct((r, D_INNER), _F32),
            jax.ShapeDtypeStruct((rs, D_INNER), _BF16),
            jax.ShapeDtypeStruct((rs, D_INNER), _F32),
            jax.ShapeDtypeStruct((rs, D_INNER), _F32),
        ],
        scratch_shapes=[
            pltpu.VMEM((tm, D_MODEL), _BF16),
            pltpu.VMEM((rs, D_MODEL), _BF16),
            pltpu.VMEM((tm + 2 * r, CHUNK), _F32),
            pltpu.VMEM((N_CHUNKS, 2 * r, CHUNK), _F32),
        ] + reorder_scratch,
        compiler_params=_compiler_params(),
        name="conv_in",
    )(x, xs, norm_g, w_in, w_in, w_in, w_in, conv_w, state_s, state_s)
    return y, ys, jnp.stack([n2, n1], axis=1), jnp.stack([n2s, n1s], axis=1)


class _StationaryWeights:
    def __init__(self, k_dim, n_cols, n_rows, layer):
        self.k_dim, self.n_cols, self.n_rows, self.layer = k_dim, n_cols, n_rows, layer
        self.piece = k_dim // W_PIECES
        self.n_steps = W_PIECES + n_cols * n_rows

    def col_row(self, s):
        t = jnp.maximum(s - W_PIECES, 0)
        return _divmod_pow2(t, self.n_rows)

    def _piece_and_block(self, s):
        n, i = self.col_row(s)
        ahead = i - (self.n_rows - W_PIECES)
        piece = jnp.where(s < W_PIECES, s, jnp.where(ahead >= 0, ahead, W_PIECES - 1))
        block = jnp.where((s >= W_PIECES) & (ahead >= 0),
                          jnp.minimum(n + 1, self.n_cols - 1), n)
        return piece, block

    def weight_spec(self):
        def index_map(s):
            piece, block = self._piece_and_block(s)
            return self.layer, piece, block
        return pl.BlockSpec((None, self.piece, WIDE), index_map)

    def scratch(self):
        return pltpu.VMEM((2, self.k_dim, WIDE), _BF16)

    def rows_spec(self, tm, width_is_wide=True):
        if width_is_wide:
            return pl.BlockSpec((tm, WIDE), lambda s: self.col_row(s)[::-1])
        return pl.BlockSpec((tm, self.k_dim), lambda s: (self.col_row(s)[1], 0))

    def fixed_rows_spec(self, rows):
        return pl.BlockSpec((rows, WIDE), lambda s: (0, self.col_row(s)[0]))

    def stage(self, s, w_ref, w16):
        n, i = self.col_row(s)
        piece, _ = self._piece_and_block(s)
        active = s >= W_PIECES
        ahead = i - (self.n_rows - W_PIECES)
        dst = jnp.where(active, 1 - (n & 1), 0)

        @pl.when(jnp.logical_not(active) | (ahead >= 0))
        def _():
            rows = pl.ds(pl.multiple_of(piece * self.piece, self.piece), self.piece)
            w16[dst, rows, :] = w_ref[...].astype(_BF16)

        return active, i, n & 1


def _out_proj_kernel(y_ref, ys_ref, w_ref, x_ref, xs_ref, o_ref, os_ref, w16, *maybe_slab, plan):
    active, i, cur = plan.stage(pl.program_id(0), w_ref, w16)

    @pl.when(active)
    def _():
        update = _dot(y_ref[...], w16[cur])
        if maybe_slab:
            for k in range(WIDE // LANES):
                lanes = slice(k * LANES, (k + 1) * LANES)
                o_ref[:, lanes] = _to_time_major_tile(x_ref, lanes, *maybe_slab) + update[:, lanes]
        else:
            o_ref[...] = x_ref[...] + update

        @pl.when(i == plan.n_rows - 1)
        def _():
            os_ref[...] = xs_ref[...] + _dot(ys_ref[...], w16[cur])


def _out_proj(y, ys, w_out, x, xs, layer):
    m = y.shape[0]
    rs = xs.shape[0]
    tm = TM_ROWWISE
    plan = _StationaryWeights(D_INNER, D_MODEL // WIDE, m // tm, layer)
    if x.ndim == 3:
        nb = x.shape[0]
        x_spec = pl.BlockSpec((nb, tm // nb, WIDE), lambda s: (0,) + plan.col_row(s)[::-1])
        reorder_scratch = [pltpu.VMEM((tm, LANES), _F32)]
    else:
        x_spec = plan.rows_spec(tm)
        reorder_scratch = []
    return pl.pallas_call(
        functools.partial(_out_proj_kernel, plan=plan),
        grid=(plan.n_steps,),
        in_specs=[
            plan.rows_spec(tm, width_is_wide=False),
            pl.BlockSpec((rs, D_INNER), lambda s: (0, 0)),
            plan.weight_spec(),
            x_spec, plan.fixed_rows_spec(rs),
        ],
        out_specs=[plan.rows_spec(tm), plan.fixed_rows_spec(rs)],
        out_shape=[
            jax.ShapeDtypeStruct((m, D_MODEL), _F32),
            jax.ShapeDtypeStruct((rs, D_MODEL), _F32),
        ],
        scratch_shapes=[plan.scratch()] + reorder_scratch,
        compiler_params=_compiler_params(1),
        name="out_proj",
    )(y, ys, w_out, x, xs)


def _ssm_prep_kernel(are_ref, aim_ref, logdt_ref, btre_ref, btim_ref, cre_ref, cim_ref,
                     lre_ref, lim_ref, bd_ref, cd_ref):
    a_re = are_ref[...]
    a_im = aim_ref[...]
    dt = jnp.exp(logdt_ref[...])
    mag = jnp.exp(a_re * dt)
    l_re = mag * jnp.cos(a_im * dt)
    l_im = mag * jnp.sin(a_im * dt)
    lre_ref[...] = l_re
    lim_ref[...] = l_im
    n_re = l_re - 1.0
    den = a_re * a_re + a_im * a_im
    k_re = (n_re * a_re + l_im * a_im) / den
    k_im = (l_im * a_re - n_re * a_im) / den
    g = a_re.shape[0]
    expand = lambda k: jnp.broadcast_to(
        k[:, None, :], (g, GROUP_SIZE, STATE_DIM)).reshape(g * GROUP_SIZE, STATE_DIM)
    k_re = expand(k_re)
    k_im = expand(k_im)
    bt_re = btre_ref[...]
    bt_im = btim_ref[...]
    bb_re = k_re * bt_re - k_im * bt_im
    bb_im = k_re * bt_im + k_im * bt_re

    shape = (CHUNK, STATES_PER_HALF)
    log2 = lambda n: n.bit_length() - 1
    row_group = (jnp.right_shift(lax.broadcasted_iota(jnp.int32, shape, 0), log2(GROUP_SIZE))
                 & (GROUPS_PER_HALF - 1))
    col_group = jnp.right_shift(lax.broadcasted_iota(jnp.int32, shape, 1), log2(STATE_DIM))
    own = row_group == col_group
    spread = lambda v: jnp.where(own, jnp.concatenate([v] * GROUPS_PER_HALF, axis=1), 0.0)
    bd_ref[...] = jnp.concatenate([spread(bb_re), spread(bb_im)], axis=1).astype(_BF16)
    cd_ref[...] = jnp.concatenate(
        [spread(cre_ref[...]).T, -spread(cim_ref[...]).T], axis=0).astype(_BF16)


def _ssm_prep(a_re, a_im, log_dt, b_re, b_im, c_re, c_im):
    gc = CHUNK // GROUP_SIZE
    rows = N_GROUPS * GROUP_SIZE
    bt_re = b_re.transpose(0, 2, 1).reshape(rows, STATE_DIM)
    bt_im = b_im.transpose(0, 2, 1).reshape(rows, STATE_DIM)
    gp_spec = pl.BlockSpec((gc, STATE_DIM), lambda j: (j, 0))
    ghp_spec = pl.BlockSpec((CHUNK, STATE_DIM), lambda j: (j, 0))
    per_chunk = lambda a, b: pl.BlockSpec((None, a, b), lambda j: (j, 0, 0))
    return pl.pallas_call(
        _ssm_prep_kernel,
        grid=(N_CHUNKS,),
        in_specs=[gp_spec, gp_spec, pl.BlockSpec((gc, 1), lambda j: (j, 0)),
                  ghp_spec, ghp_spec, ghp_spec, ghp_spec],
        out_specs=[gp_spec, gp_spec, per_chunk(CHUNK, 2 * STATES_PER_HALF),
                   per_chunk(2 * STATES_PER_HALF, CHUNK)],
        out_shape=[
            jax.ShapeDtypeStruct((N_GROUPS, STATE_DIM), _F32),
            jax.ShapeDtypeStruct((N_GROUPS, STATE_DIM), _F32),
            jax.ShapeDtypeStruct((N_CHUNKS, CHUNK, 2 * STATES_PER_HALF), _BF16),
            jax.ShapeDtypeStruct((N_CHUNKS, 2 * STATES_PER_HALF, CHUNK), _BF16),
        ],
        compiler_params=_compiler_params(1),
        name="ssm_prep",
    )(a_re, a_im, log_dt.reshape(N_GROUPS, 1), bt_re, bt_im,
      c_re.reshape(rows, STATE_DIM), c_im.reshape(rows, STATE_DIM))


def _lam_rows(lre_ref, lim_ref, rows_per_half):
    first_half = (lax.broadcasted_iota(jnp.int32, (2 * rows_per_half, STATES_PER_HALF), 0)
                  < rows_per_half)
    return (jnp.where(first_half, lre_ref[0:1, :], lre_ref[1:2, :]),
            jnp.where(first_half, lim_ref[0:1, :], lim_ref[1:2, :]))


def _readout(x, cd_ref):
    return _dot(x.astype(_BF16), cd_ref[...])


def _ssm_scan_kernel(x_ref, xs_ref, g_ref, wu_ref, wz_ref, bd_ref,
                     cd_a_ref, lre_a_ref, lim_a_ref, d_a_ref,
                     cd_b_ref, lre_b_ref, lim_b_ref, d_b_ref,
                     s0re_ref, s0im_ref,
                     y_ref, zs_ref, nre_ref, nim_ref, ys_ref, zss_ref, nres_ref, nims_ref,
                     h_scr, hs_scr, x0, u0, z0, x1, u1, z1, carry_re, carry_im,
                     *, rows_per_step, n_blocks):
    s = pl.program_id(0)
    n_steps = n_blocks * N_CHUNKS
    step_a = jnp.minimum(s, n_steps - 1)
    step_b = jnp.maximum(s - 1, 0)
    i_a, j_a = _divmod_pow2(step_a, N_CHUNKS)
    i_b, j_b = _divmod_pow2(step_b, N_CHUNKS)
    tm = x_ref.shape[0]
    r = rows_per_step
    rs = xs_ref.shape[0]
    sh = STATES_PER_HALF
    assert 2 * r == SUBLANES

    @pl.when(s == 0)
    def _():
        for ref in (x1, u1, z1):
            ref[...] = jnp.zeros(ref.shape, _F32)

    @pl.when(j_a == 0)
    def _():
        h_scr[...] = _rmsnorm(x_ref[...], g_ref[...]).astype(_BF16)

    @pl.when(i_b == 0)
    def _():
        zeros = jnp.zeros((2 * r, sh), _F32)
        carry_re[j_b] = zeros
        carry_im[j_b] = zeros

    def stages(x_a, u_a, z_a, x_b, u_b, z_b):
        n = tm // SCAN_PIECES
        l_re, l_im = _lam_rows(lre_b_ref, lim_b_ref, r)

        def recurrence(c, lo):
            c_re, c_im = c
            for k in range(lo // SUBLANES, (lo + n) // SUBLANES):
                for rows in (pl.ds(k * SUBLANES, SUBLANES), pl.ds(tm + k * SUBLANES, SUBLANES)):
                    c_re, c_im = (l_re * c_re - l_im * c_im + x_b[rows, 0:sh],
                                  l_re * c_im + l_im * c_re + x_b[rows, sh:2 * sh])
                    x_b[rows, 0:sh] = c_re
                    x_b[rows, sh:2 * sh] = c_im
            return c_re, c_im

        def readout(lo):
            states = jnp.concatenate([x_b[lo:lo + n, :], x_b[tm + lo:tm + lo + n, :]], axis=0)
            yy = _readout(states, cd_b_ref)
            low_rows = (lax.broadcasted_iota(jnp.int32, (n, LANES), 0) & r) == 0
            y_even = yy[:n]
            y_odd = yy[n:]
            y_first = jnp.where(low_rows, y_even[:, :LANES], pltpu.roll(y_odd[:, :LANES], r, 0))
            y_second = jnp.where(low_rows, pltpu.roll(y_even[:, LANES:], n - r, 0),
                                 y_odd[:, LANES:])
            y = jnp.concatenate([y_first, y_second], axis=1)
            rows = slice(lo, lo + n)
            y_ref[rows, :] = jax.nn.gelu(y + d_b_ref[...] * u_b[rows, :]).astype(_BF16)
            zs_ref[rows, :] = _silu(z_b[rows, :]).astype(_BF16)

        h = h_scr[...]
        u = _dot(h, wu_ref[...].astype(_BF16))
        u_a[...] = u
        carry = (carry_re[j_b], carry_im[j_b])
        for piece in range(SCAN_PIECES):
            if piece == SCAN_PIECES // 2:
                z_a[...] = _dot(h, wz_ref[...].astype(_BF16))
            carry = recurrence(carry, piece * n)
            readout(piece * n)
        carry_re[j_b], carry_im[j_b] = carry

        first_lanes = lax.broadcasted_iota(jnp.int32, (tm, CHUNK), 1) < LANES
        low_rows = (lax.broadcasted_iota(jnp.int32, (tm, CHUNK), 0) & r) == 0
        u_first = jnp.where(first_lanes, u, 0.0)
        u_second = jnp.where(first_lanes, 0.0, u)
        u_down = pltpu.roll(u, r, 0)
        u_up = pltpu.roll(u, tm - r, 0)
        even = jnp.where(low_rows, u_first, jnp.where(first_lanes, 0.0, u_down))
        odd = jnp.where(low_rows, jnp.where(first_lanes, u_up, 0.0), u_second)
        x_a[...] = _dot(jnp.concatenate([even, odd], axis=0).astype(_BF16), bd_ref[...])

    @pl.when(s & 1 == 0)
    def _():
        stages(x0, u0, z0, x1, u1, z1)

    @pl.when(s & 1 == 1)
    def _():
        stages(x1, u1, z1, x0, u0, z0)

    @pl.when((i_b == n_blocks - 1) & (s > 0))
    def _():
        c_re = carry_re[j_b]
        c_im = carry_im[j_b]
        nre_ref[:, 0:sh] = c_re[0:r]
        nre_ref[:, sh:2 * sh] = c_re[r:2 * r]
        nim_ref[:, 0:sh] = c_im[0:r]
        nim_ref[:, sh:2 * sh] = c_im[r:2 * r]

    @pl.when((i_a == n_blocks - 1) & (s < n_steps))
    def _():
        @pl.when(j_a == 0)
        def _():
            hs_scr[...] = _rmsnorm(xs_ref[...], g_ref[...]).astype(_BF16)

        hs = hs_scr[...]
        us = _dot(hs, wu_ref[...].astype(_BF16))
        zz = _dot(hs, wz_ref[...].astype(_BF16))
        first = lax.broadcasted_iota(jnp.int32, (rs, CHUNK), 1) < LANES
        lhs_s = jnp.concatenate([jnp.where(first, us, 0.0), jnp.where(first, 0.0, us)], axis=0)
        bus = _dot(lhs_s.astype(_BF16), bd_ref[...])
        p_re = jnp.concatenate([s0re_ref[:, 0:sh], s0re_ref[:, sh:2 * sh]], axis=0)
        p_im = jnp.concatenate([s0im_ref[:, 0:sh], s0im_ref[:, sh:2 * sh]], axis=0)
        ls_re, ls_im = _lam_rows(lre_a_ref, lim_a_ref, rs)
        n_re = ls_re * p_re - ls_im * p_im + bus[:, :sh]
        n_im = ls_re * p_im + ls_im * p_re + bus[:, sh:]
        nres_ref[:, 0:sh] = n_re[0:rs]
        nres_ref[:, sh:2 * sh] = n_re[rs:2 * rs]
        nims_ref[:, 0:sh] = n_im[0:rs]
        nims_ref[:, sh:2 * sh] = n_im[rs:2 * rs]
        yys = _readout(jnp.concatenate([n_re, n_im], axis=1), cd_a_ref)
        y_s = jnp.concatenate([yys[:rs, :LANES], yys[rs:, LANES:]], axis=1)
        ys_ref[...] = jax.nn.gelu(y_s + d_a_ref[...] * us).astype(_BF16)
        zss_ref[...] = _silu(zz).astype(_BF16)


def _ssm_scan(x, xs, norm_g, w_in, bd, cd, lam_re, lam_im, d_skip, s0_re, s0_im, layer,
              *, rows_per_step):
    m = x.shape[0]
    rs = xs.shape[0]
    r = rows_per_step
    tm = TM_SCAN
    n_blocks = m // tm
    n_steps = n_blocks * N_CHUNKS
    sh = STATES_PER_HALF

    def stage_a(s):
        return _divmod_pow2(jnp.minimum(s, n_steps - 1), N_CHUNKS)

    def stage_b(s):
        return _divmod_pow2(jnp.maximum(s - 1, 0), N_CHUNKS)

    def per_chunk(stage, a, b):
        return pl.BlockSpec((None, a, b), lambda s: (stage(s)[1], 0, 0))

    def skip_spec(stage):
        return pl.BlockSpec((None, 1, CHUNK), lambda s: (layer, 0, stage(s)[1]))

    def live_last(stage, rows, cols):
        def index_map(s):
            i, j = stage(s)
            return 0, jnp.where(i == n_blocks - 1, j, 0)
        return pl.BlockSpec((rows, cols), index_map)

    act_spec = pl.BlockSpec((tm, CHUNK), lambda s: stage_b(s))
    state_rows = pltpu.VMEM((2 * tm, 2 * sh), _F32)
    chunk_rows = pltpu.VMEM((tm, CHUNK), _F32)
    return pl.pallas_call(
        functools.partial(_ssm_scan_kernel, rows_per_step=r, n_blocks=n_blocks),
        grid=(n_steps + 1,),
        in_specs=[
            pl.BlockSpec((tm, D_MODEL), lambda s: (stage_a(s)[0], 0)),
            pl.BlockSpec((rs, D_MODEL), lambda s: (0, 0)),
            pl.BlockSpec((None, 1, D_MODEL), lambda s: (layer, 0, 0)),
            pl.BlockSpec((None, D_MODEL, CHUNK), lambda s: (layer, 0, stage_a(s)[1])),
            pl.BlockSpec((None, D_MODEL, CHUNK), lambda s: (layer, 0, N_CHUNKS + stage_a(s)[1])),
            per_chunk(stage_a, CHUNK, 2 * sh),
            per_chunk(stage_a, 2 * sh, CHUNK),
            per_chunk(stage_a, 2, sh), per_chunk(stage_a, 2, sh), skip_spec(stage_a),
            per_chunk(stage_b, 2 * sh, CHUNK),
            per_chunk(stage_b, 2, sh), per_chunk(stage_b, 2, sh), skip_spec(stage_b),
            live_last(stage_a, rs, STATES_PER_CHUNK), live_last(stage_a, rs, STATES_PER_CHUNK),
        ],
        out_specs=[
            act_spec, act_spec,
            live_last(stage_b, r, STATES_PER_CHUNK), live_last(stage_b, r, STATES_PER_CHUNK),
            live_last(stage_a, rs, CHUNK), live_last(stage_a, rs, CHUNK),
            live_last(stage_a, rs, STATES_PER_CHUNK), live_last(stage_a, rs, STATES_PER_CHUNK),
        ],
        out_shape=[
            jax.ShapeDtypeStruct((m, D_INNER), _BF16),
            jax.ShapeDtypeStruct((m, D_INNER), _BF16),
            jax.ShapeDtypeStruct((r, N_STATES), _F32),
            jax.ShapeDtypeStruct((r, N_STATES), _F32),
            jax.ShapeDtypeStruct((rs, D_INNER), _BF16),
            jax.ShapeDtypeStruct((rs, D_INNER), _BF16),
            jax.ShapeDtypeStruct((rs, N_STATES), _F32),
            jax.ShapeDtypeStruct((rs, N_STATES), _F32),
        ],
        scratch_shapes=[
            pltpu.VMEM((tm, D_MODEL), _BF16),
            pltpu.VMEM((rs, D_MODEL), _BF16),
            state_rows, chunk_rows, chunk_rows,
            state_rows, chunk_rows, chunk_rows,
            pltpu.VMEM((N_CHUNKS, 2 * r, sh), _F32),
            pltpu.VMEM((N_CHUNKS, 2 * r, sh), _F32),
        ],
        compiler_params=_compiler_params(1),
        name="ssm_scan",
    )(x, xs, norm_g, w_in, w_in, bd, cd, lam_re, lam_im, d_skip,
      cd, lam_re, lam_im, d_skip, s0_re, s0_im)


def _glu_kernel(y_ref, yj_ref, zs_ref, ys_ref, yjs_ref, zss_ref, wg_ref, bg_ref, o_ref, os_ref,
                w16, *, plan):
    active, i, cur = plan.stage(pl.program_id(0), wg_ref, w16)

    def gated(y_all, y_chunk, z_chunk):
        gate = jax.nn.sigmoid(_dot(y_all, w16[cur]) + bg_ref[...])
        return (y_chunk.astype(_F32) * gate * z_chunk.astype(_F32)).astype(_BF16)

    @pl.when(active)
    def _():
        o_ref[...] = gated(y_ref[...], yj_ref[...], zs_ref[...])

        @pl.when(i == plan.n_rows - 1)
        def _():
            os_ref[...] = gated(ys_ref[...], yjs_ref[...], zss_ref[...])


def _glu(y, zs, ys, zss, w_glu, b_glu, layer):
    m = y.shape[0]
    rs = ys.shape[0]
    tm = TM_ROWWISE
    plan = _StationaryWeights(D_INNER, D_INNER // WIDE, m // tm, layer)
    rows = plan.rows_spec(tm)
    sample_rows = plan.fixed_rows_spec(rs)
    return pl.pallas_call(
        functools.partial(_glu_kernel, plan=plan),
        grid=(plan.n_steps,),
        in_specs=[
            plan.rows_spec(tm, width_is_wide=False),
            rows, rows,
            pl.BlockSpec((rs, D_INNER), lambda s: (0, 0)),
            sample_rows, sample_rows,
            plan.weight_spec(),
            pl.BlockSpec((None, 1, WIDE), lambda s: (layer, 0, plan.col_row(s)[0])),
        ],
        out_specs=[rows, sample_rows],
        out_shape=[
            jax.ShapeDtypeStruct((m, D_INNER), _BF16),
            jax.ShapeDtypeStruct((rs, D_INNER), _BF16),
        ],
        scratch_shapes=[plan.scratch()],
        compiler_params=_compiler_params(1),
        name="glu",
    )(y, y, zs, ys, ys, zss, w_glu, b_glu)


def _final_norm_batch_major_kernel(x_ref, g_ref, o_ref, slab):
    nb, tt, _ = o_ref.shape
    xn = _rmsnorm(x_ref[...], g_ref[...])
    for k in range(D_MODEL // LANES):
        lanes = slice(k * LANES, (k + 1) * LANES)
        slab[k] = xn[:, lanes]
        for b in range(nb):
            o_ref[b, :, lanes] = slab[k, pl.ds(b, tt, stride=nb), :]


def _final_norm_batch_major(x, g, nb):
    seq = x.shape[0] // nb
    tt = TT_REORDER
    return pl.pallas_call(
        _final_norm_batch_major_kernel,
        grid=(seq // tt,),
        in_specs=[pl.BlockSpec((nb * tt, D_MODEL), lambda i: (i, 0)),
                  pl.BlockSpec((1, D_MODEL), lambda i: (0, 0))],
        out_specs=pl.BlockSpec((nb, tt, D_MODEL), lambda i: (0, i, 0)),
        out_shape=jax.ShapeDtypeStruct((nb, seq, D_MODEL), _F32),
        scratch_shapes=[pltpu.VMEM((D_MODEL // LANES, nb * tt, LANES), _F32)],
        compiler_params=_compiler_params(1),
        name="final_norm_batch_major",
    )(x, g)


def _final_norm_kernel(x_ref, g_ref, o_ref):
    o_ref[...] = _rmsnorm(x_ref[...], g_ref[...])


def _final_norm(x, g):
    m = x.shape[0]
    tm = min(m, TM_CONV)
    row_spec = pl.BlockSpec((tm, D_MODEL), lambda i: (i, 0))
    return pl.pallas_call(
        _final_norm_kernel,
        grid=(m // tm,),
        in_specs=[row_spec, pl.BlockSpec((1, D_MODEL), lambda i: (0, 0))],
        out_specs=row_spec,
        out_shape=jax.ShapeDtypeStruct((m, D_MODEL), _F32),
        compiler_params=_compiler_params(1),
        name="final_norm",
    )(x, g)


def kernel(x_prompt, x_sample, state_conv, state_ssm_re, state_ssm_im,
           conv_norm, conv_w_in, conv_w, conv_w_out,
           ssm_norm, ssm_w_in, ssm_a_re, ssm_a_im, ssm_log_dt, ssm_b_re, ssm_b_im,
           ssm_c_re, ssm_c_im, ssm_d, ssm_w_glu, ssm_b_glu, ssm_w_out, final_norm):
    batch, seq, _ = x_prompt.shape
    dec_batch = x_sample.shape[0]
    n_conv = conv_w_in.shape[0]
    n_ssm = ssm_w_in.shape[0]
    depth = n_conv + n_ssm

    xp = x_prompt
    xs = x_sample.reshape(dec_batch, D_MODEL)
    conv_norm3 = conv_norm.reshape(n_conv, 1, D_MODEL)
    ssm_norm3 = ssm_norm.reshape(n_ssm, 1, D_MODEL)
    ssm_d3 = ssm_d.reshape(n_ssm, 1, D_INNER)
    ssm_b_glu3 = ssm_b_glu.reshape(n_ssm, 1, D_INNER)

    conv_p, conv_s, re_p, im_p, re_s, im_s = [], [], [], [], [], []
    for layer in range(depth):
        l = layer // 2
        if layer % 2 == 0:
            y, ys, ns_p, ns_s = _conv_in(
                xp, xs, conv_norm3, conv_w_in, conv_w,
                state_conv[l].reshape(dec_batch, (CONV_WIDTH - 1) * D_INNER), l,
                rows_per_step=batch)
            xp, xs = _out_proj(y, ys, conv_w_out, xp, xs, l)
            conv_p.append(ns_p)
            conv_s.append(ns_s)
        else:
            lam_re, lam_im, bd, cd = _ssm_prep(
                ssm_a_re[l], ssm_a_im[l], ssm_log_dt[l], ssm_b_re[l], ssm_b_im[l],
                ssm_c_re[l], ssm_c_im[l])
            lam_re = lam_re.reshape(N_CHUNKS, 2, STATES_PER_HALF)
            lam_im = lam_im.reshape(N_CHUNKS, 2, STATES_PER_HALF)
            y, zs, hr_p, hi_p, ys, zss, hr_s, hi_s = _ssm_scan(
                xp, xs, ssm_norm3, ssm_w_in, bd, cd, lam_re, lam_im, ssm_d3,
                state_ssm_re[l].reshape(dec_batch, N_STATES),
                state_ssm_im[l].reshape(dec_batch, N_STATES),
                l, rows_per_step=batch)
            yy, yys = _glu(y, zs, ys, zss, ssm_w_glu, ssm_b_glu3, l)
            xp, xs = _out_proj(yy, yys, ssm_w_out, xp, xs, l)
            re_p.append(hr_p.reshape(batch, N_GROUPS, STATE_DIM))
            im_p.append(hi_p.reshape(batch, N_GROUPS, STATE_DIM))
            re_s.append(hr_s.reshape(dec_batch, N_GROUPS, STATE_DIM))
            im_s.append(hi_s.reshape(dec_batch, N_GROUPS, STATE_DIM))

    final_g = final_norm.reshape(1, D_MODEL)
    y_prompt = _final_norm_batch_major(xp, final_g, batch)
    xs = _final_norm(xs, final_g)
    y_sample = xs.reshape(dec_batch, 1, D_MODEL)
    return (y_prompt, y_sample, jnp.stack(conv_p), jnp.stack(conv_s),
            jnp.stack(re_p), jnp.stack(im_p), jnp.stack(re_s), jnp.stack(im_s))
```

```python
import functools

import jax
import jax.numpy as jnp
from jax import lax
from jax.experimental import pallas as pl
from jax.experimental.pallas import tpu as pltpu

D_MODEL = 2048
D_INNER = 2 * D_MODEL
CONV_WIDTH = 3
GROUP_SIZE = 16
N_GROUPS = D_INNER // GROUP_SIZE
STATE_DIM = 64
N_STATES = N_GROUPS * STATE_DIM
EPS = 1e-6

LANES = 128
SUBLANES = 8
CHUNK = 2 * LANES
N_CHUNKS = D_INNER // CHUNK
GROUPS_PER_HALF = LANES // GROUP_SIZE
STATES_PER_HALF = GROUPS_PER_HALF * STATE_DIM
STATES_PER_CHUNK = 2 * STATES_PER_HALF
VMEM_LIMIT_BYTES = 60000 * 1024

TM_ROWWISE = 512
TM_CONV = 1024
TM_SCAN = 512
WIDE = 1024
TT_REORDER = 256
W_PIECES = 4
SCAN_PIECES = 8

_F32 = jnp.float32
_BF16 = jnp.bfloat16


def _dot(a, b):
    return jnp.dot(a, b, preferred_element_type=_F32)


def _rmsnorm(x, g):
    return x * lax.rsqrt(jnp.mean(x * x, axis=-1, keepdims=True) + EPS) * g


def _to_time_major_tile(x_ref, lanes, slab, transform=lambda b, piece: piece):
    nb, tt, _ = x_ref.shape
    for b in range(nb):
        slab[pl.ds(b, tt, stride=nb), :] = transform(b, x_ref[b, :, lanes])
    return slab[...]


def _rmsnorm_to_time_major(x_ref, g_ref, h_ref, slab):
    nb = x_ref.shape[0]
    inv_rms = []
    for b in range(nb):
        xb = x_ref[b]
        inv_rms.append(lax.rsqrt(jnp.mean(xb * xb, axis=-1, keepdims=True) + EPS))
    for k in range(D_MODEL // LANES):
        lanes = slice(k * LANES, (k + 1) * LANES)
        tile = _to_time_major_tile(
            x_ref, lanes, slab, lambda b, piece: piece * inv_rms[b] * g_ref[:, lanes])
        h_ref[:, lanes] = tile.astype(_BF16)


def _silu(z):
    return z * jax.nn.sigmoid(z)


def _divmod_pow2(x, n):
    assert n > 0 and n & (n - 1) == 0
    return jnp.right_shift(x, n.bit_length() - 1), x & (n - 1)


def _compiler_params(n_axes=2):
    return pltpu.CompilerParams(
        dimension_semantics=("arbitrary",) * n_axes,
        vmem_limit_bytes=VMEM_LIMIT_BYTES)


def _last_block_cols(n_blocks, offset=0):
    return lambda i, j: (0, offset + jnp.where(i == n_blocks - 1, j, 0))


def _conv_in_kernel(x_ref, xs_ref, g_ref, wb_ref, wc_ref, wv_ref, wz_ref, cw_ref,
                    s2s_ref, s1s_ref,
                    y_ref, n2_ref, n1_ref, ys_ref, n2s_ref, n1s_ref,
                    h_scr, hs_scr, ubuf, carry, *maybe_slab, rows_per_step):
    i = pl.program_id(0)
    j = pl.program_id(1)
    last = i == pl.num_programs(0) - 1
    tm = h_scr.shape[0]
    r = rows_per_step

    @pl.when(j == 0)
    def _():
        if maybe_slab:
            _rmsnorm_to_time_major(x_ref, g_ref, h_scr, *maybe_slab)
        else:
            h_scr[...] = _rmsnorm(x_ref[...], g_ref[...]).astype(_BF16)

    @pl.when(i == 0)
    def _():
        carry[j] = jnp.zeros(carry.shape[1:], _F32)

    w16 = [w[...].astype(_BF16) for w in (wb_ref, wc_ref, wv_ref, wz_ref)]
    cw = cw_ref[...]

    h = h_scr[...]
    c = _dot(h, w16[1])
    v = _dot(h, w16[2])
    ubuf[0:2 * r, :] = carry[j]
    ubuf[2 * r:2 * r + tm, :] = c * v
    conv = (cw[0:1, :] * ubuf[0:tm, :] + cw[1:2, :] * ubuf[r:r + tm, :]
            + cw[2:3, :] * ubuf[2 * r:2 * r + tm, :])
    gate = _silu(_dot(h, w16[3]))
    b = _dot(h, w16[0])
    y_ref[...] = (b * conv * gate).astype(_BF16)
    carry[j] = ubuf[tm:tm + 2 * r, :]

    @pl.when(last)
    def _():
        n2_ref[...] = ubuf[tm:tm + r, :]
        n1_ref[...] = ubuf[tm + r:tm + 2 * r, :]

        @pl.when(j == 0)
        def _():
            hs_scr[...] = _rmsnorm(xs_ref[...], g_ref[...]).astype(_BF16)

        hs = hs_scr[...]
        bs, cs, vs, zs = [_dot(hs, w) for w in w16]
        us = cs * vs
        s1s = s1s_ref[...]
        convs = cw[0:1, :] * s2s_ref[...] + cw[1:2, :] * s1s + cw[2:3, :] * us
        ys_ref[...] = (bs * convs * _silu(zs)).astype(_BF16)
        n2s_ref[...] = s1s
        n1s_ref[...] = us


def _conv_in(x, xs, norm_g, w_in, conv_w, state_s, layer, *, rows_per_step):
    rs = xs.shape[0]
    r = rows_per_step
    tm = TM_CONV
    if x.ndim == 3:
        nb, seq, _ = x.shape
        m = nb * seq
        x_spec = pl.BlockSpec((nb, tm // nb, D_MODEL), lambda i, j: (0, i, 0))
        reorder_scratch = [pltpu.VMEM((tm, LANES), _F32)]
    else:
        m = x.shape[0]
        x_spec = pl.BlockSpec((tm, D_MODEL), lambda i, j: (i, 0))
        reorder_scratch = []
    n_blocks = m // tm
    w_in_spec = lambda part: pl.BlockSpec(
        (None, D_MODEL, CHUNK), lambda i, j: (layer, 0, part * N_CHUNKS + j))
    live_last = lambda rows, offset=0: pl.BlockSpec(
        (rows, CHUNK), _last_block_cols(n_blocks, offset))
    y, n2, n1, ys, n2s, n1s = pl.pallas_call(
        functools.partial(_conv_in_kernel, rows_per_step=r),
        grid=(n_blocks, N_CHUNKS),
        in_specs=[
            x_spec,
            pl.BlockSpec((rs, D_MODEL), lambda i, j: (0, 0)),
            pl.BlockSpec((None, 1, D_MODEL), lambda i, j: (layer, 0, 0)),
            w_in_spec(0), w_in_spec(1), w_in_spec(2), w_in_spec(3),
            pl.BlockSpec((None, CONV_WIDTH, CHUNK), lambda i, j: (layer, 0, j)),
            live_last(rs), live_last(rs, N_CHUNKS),
        ],
        out_specs=[
            pl.BlockSpec((tm, CHUNK), lambda i, j: (i, j)),
            live_last(r), live_last(r),
            live_last(rs), live_last(rs), live_last(rs),
        ],
        out_shape=[
            jax.ShapeDtypeStruct((m, D_INNER), _BF16),
            jax.ShapeDtypeStruct((r, D_INNER), _F32),
            jax.ShapeDtypeStruct((r, D_INNER), _F32),
            jax.ShapeDtypeStruct((rs, D_INNER), _BF16),
            jax.ShapeDtypeStruct((rs, D_INNER), _F32),
            jax.ShapeDtypeStruct((rs, D_INNER), _F32),
        ],
        scratch_shapes=[
            pltpu.VMEM((tm, D_MODEL), _BF16),
            pltpu.VMEM((rs, D_MODEL), _BF16),
            pltpu.VMEM((tm + 2 * r, CHUNK), _F32),
            pltpu.VMEM((N_CHUNKS, 2 * r, CHUNK), _F32),
        ] + reorder_scratch,
        compiler_params=_compiler_params(),
        name="conv_in",
    )(x, xs, norm_g, w_in, w_in, w_in, w_in, conv_w, state_s, state_s)
    return y, ys, jnp.stack([n2, n1], axis=1), jnp.stack([n2s, n1s], axis=1)


class _StationaryWeights:
    def __init__(self, k_dim, n_cols, n_rows, layer):
        self.k_dim, self.n_cols, self.n_rows, self.layer = k_dim, n_cols, n_rows, layer
        self.piece = k_dim // W_PIECES
        self.n_steps = W_PIECES + n_cols * n_rows

    def col_row(self, s):
        t = jnp.maximum(s - W_PIECES, 0)
        return _divmod_pow2(t, self.n_rows)

    def _piece_and_block(self, s):
        n, i = self.col_row(s)
        ahead = i - (self.n_rows - W_PIECES)
        piece = jnp.where(s < W_PIECES, s, jnp.where(ahead >= 0, ahead, W_PIECES - 1))
        block = jnp.where((s >= W_PIECES) & (ahead >= 0),
                          jnp.minimum(n + 1, self.n_cols - 1), n)
        return piece, block

    def weight_spec(self):
        def index_map(s):
            piece, block = self._piece_and_block(s)
            return self.layer, piece, block
        return pl.BlockSpec((None, self.piece, WIDE), index_map)

    def scratch(self):
        return pltpu.VMEM((2, self.k_dim, WIDE), _BF16)

    def rows_spec(self, tm, width_is_wide=True):
        if width_is_wide:
            return pl.BlockSpec((tm, WIDE), lambda s: self.col_row(s)[::-1])
        return pl.BlockSpec((tm, self.k_dim), lambda s: (self.col_row(s)[1], 0))

    def fixed_rows_spec(self, rows):
        return pl.BlockSpec((rows, WIDE), lambda s: (0, self.col_row(s)[0]))

    def stage(self, s, w_ref, w16):
        n, i = self.col_row(s)
        piece, _ = self._piece_and_block(s)
        active = s >= W_PIECES
        ahead = i - (self.n_rows - W_PIECES)
        dst = jnp.where(active, 1 - (n & 1), 0)

        @pl.when(jnp.logical_not(active) | (ahead >= 0))
        def _():
            rows = pl.ds(pl.multiple_of(piece * self.piece, self.piece), self.piece)
            w16[dst, rows, :] = w_ref[...].astype(_BF16)

        return active, i, n & 1


def _out_proj_kernel(y_ref, ys_ref, w_ref, x_ref, xs_ref, o_ref, os_ref, w16, *maybe_slab, plan):
    active, i, cur = plan.stage(pl.program_id(0), w_ref, w16)

    @pl.when(active)
    def _():
        update = _dot(y_ref[...], w16[cur])
        if maybe_slab:
            for k in range(WIDE // LANES):
                lanes = slice(k * LANES, (k + 1) * LANES)
                o_ref[:, lanes] = _to_time_major_tile(x_ref, lanes, *maybe_slab) + update[:, lanes]
        else:
            o_ref[...] = x_ref[...] + update

        @pl.when(i == plan.n_rows - 1)
        def _():
            os_ref[...] = xs_ref[...] + _dot(ys_ref[...], w16[cur])


def _out_proj(y, ys, w_out, x, xs, layer):
    m = y.shape[0]
    rs = xs.shape[0]
    tm = TM_ROWWISE
    plan = _StationaryWeights(D_INNER, D_MODEL // WIDE, m // tm, layer)
    if x.ndim == 3:
        nb = x.shape[0]
        x_spec = pl.BlockSpec((nb, tm // nb, WIDE), lambda s: (0,) + plan.col_row(s)[::-1])
        reorder_scratch = [pltpu.VMEM((tm, LANES), _F32)]
    else:
        x_spec = plan.rows_spec(tm)
        reorder_scratch = []
    return pl.pallas_call(
        functools.partial(_out_proj_kernel, plan=plan),
        grid=(plan.n_steps,),
        in_specs=[
            plan.rows_spec(tm, width_is_wide=False),
            pl.BlockSpec((rs, D_INNER), lambda s: (0, 0)),
            plan.weight_spec(),
            x_spec, plan.fixed_rows_spec(rs),
        ],
        out_specs=[plan.rows_spec(tm), plan.fixed_rows_spec(rs)],
        out_shape=[
            jax.ShapeDtypeStruct((m, D_MODEL), _F32),
            jax.ShapeDtypeStruct((rs, D_MODEL), _F32),
        ],
        scratch_shapes=[plan.scratch()] + reorder_scratch,
        compiler_params=_compiler_params(1),
        name="out_proj",
    )(y, ys, w_out, x, xs)


def _ssm_prep_kernel(are_ref, aim_ref, logdt_ref, btre_ref, btim_ref, cre_ref, cim_ref,
                     lre_ref, lim_ref, bd_ref, cd_ref):
    a_re = are_ref[...]
    a_im = aim_ref[...]
    dt = jnp.exp(logdt_ref[...])
    mag = jnp.exp(a_re * dt)
    l_re = mag * jnp.cos(a_im * dt)
    l_im = mag * jnp.sin(a_im * dt)
    lre_ref[...] = l_re
    lim_ref[...] = l_im
    n_re = l_re - 1.0
    den = a_re * a_re + a_im * a_im
    k_re = (n_re * a_re + l_im * a_im) / den
    k_im = (l_im * a_re - n_re * a_im) / den
    g = a_re.shape[0]
    expand = lambda k: jnp.broadcast_to(
        k[:, None, :], (g, GROUP_SIZE, STATE_DIM)).reshape(g * GROUP_SIZE, STATE_DIM)
    k_re = expand(k_re)
    k_im = expand(k_im)
    bt_re = btre_ref[...]
    bt_im = btim_ref[...]
    bb_re = k_re * bt_re - k_im * bt_im
    bb_im = k_re * bt_im + k_im * bt_re

    shape = (CHUNK, STATES_PER_HALF)
    log2 = lambda n: n.bit_length() - 1
    row_group = (jnp.right_shift(lax.broadcasted_iota(jnp.int32, shape, 0), log2(GROUP_SIZE))
                 & (GROUPS_PER_HALF - 1))
    col_group = jnp.right_shift(lax.broadcasted_iota(jnp.int32, shape, 1), log2(STATE_DIM))
    own = row_group == col_group
    spread = lambda v: jnp.where(own, jnp.concatenate([v] * GROUPS_PER_HALF, axis=1), 0.0)
    bd_ref[...] = jnp.concatenate([spread(bb_re), spread(bb_im)], axis=1).astype(_BF16)
    cd_ref[...] = jnp.concatenate(
        [spread(cre_ref[...]).T, -spread(cim_ref[...]).T], axis=0).astype(_BF16)


def _ssm_prep(a_re, a_im, log_dt, b_re, b_im, c_re, c_im):
    gc = CHUNK // GROUP_SIZE
    rows = N_GROUPS * GROUP_SIZE
    bt_re = b_re.transpose(0, 2, 1).reshape(rows, STATE_DIM)
    bt_im = b_im.transpose(0, 2, 1).reshape(rows, STATE_DIM)
    gp_spec = pl.BlockSpec((gc, STATE_DIM), lambda j: (j, 0))
    ghp_spec = pl.BlockSpec((CHUNK, STATE_DIM), lambda j: (j, 0))
    per_chunk = lambda a, b: pl.BlockSpec((None, a, b), lambda j: (j, 0, 0))
    return pl.pallas_call(
        _ssm_prep_kernel,
        grid=(N_CHUNKS,),
        in_specs=[gp_spec, gp_spec, pl.BlockSpec((gc, 1), lambda j: (j, 0)),
                  ghp_spec, ghp_spec, ghp_spec, ghp_spec],
        out_specs=[gp_spec, gp_spec, per_chunk(CHUNK, 2 * STATES_PER_HALF),
                   per_chunk(2 * STATES_PER_HALF, CHUNK)],
        out_shape=[
            jax.ShapeDtypeStruct((N_GROUPS, STATE_DIM), _F32),
            jax.ShapeDtypeStruct((N_GROUPS, STATE_DIM), _F32),
            jax.ShapeDtypeStruct((N_CHUNKS, CHUNK, 2 * STATES_PER_HALF), _BF16),
            jax.ShapeDtypeStruct((N_CHUNKS, 2 * STATES_PER_HALF, CHUNK), _BF16),
        ],
        compiler_params=_compiler_params(1),
        name="ssm_prep",
    )(a_re, a_im, log_dt.reshape(N_GROUPS, 1), bt_re, bt_im,
      c_re.reshape(rows, STATE_DIM), c_im.reshape(rows, STATE_DIM))


def _lam_rows(lre_ref, lim_ref, rows_per_half):
    first_half = (lax.broadcasted_iota(jnp.int32, (2 * rows_per_half, STATES_PER_HALF), 0)
                  < rows_per_half)
    return (jnp.where(first_half, lre_ref[0:1, :], lre_ref[1:2, :]),
            jnp.where(first_half, lim_ref[0:1, :], lim_ref[1:2, :]))


def _readout(x, cd_ref):
    return _dot(x.astype(_BF16), cd_ref[...])


def _ssm_scan_kernel(x_ref, xs_ref, g_ref, wu_ref, wz_ref, bd_ref,
                     cd_a_ref, lre_a_ref, lim_a_ref, d_a_ref,
                     cd_b_ref, lre_b_ref, lim_b_ref, d_b_ref,
                     s0re_ref, s0im_ref,
                     y_ref, zs_ref, nre_ref, nim_ref, ys_ref, zss_ref, nres_ref, nims_ref,
                     h_scr, hs_scr, x0, u0, z0, x1, u1, z1, carry_re, carry_im,
                     *, rows_per_step, n_blocks):
    s = pl.program_id(0)
    n_steps = n_blocks * N_CHUNKS
    step_a = jnp.minimum(s, n_steps - 1)
    step_b = jnp.maximum(s - 1, 0)
    i_a, j_a = _divmod_pow2(step_a, N_CHUNKS)
    i_b, j_b = _divmod_pow2(step_b, N_CHUNKS)
    tm = x_ref.shape[0]
    r = rows_per_step
    rs = xs_ref.shape[0]
    sh = STATES_PER_HALF
    assert 2 * r == SUBLANES

    @pl.when(s == 0)
    def _():
        for ref in (x1, u1, z1):
            ref[...] = jnp.zeros(ref.shape, _F32)

    @pl.when(j_a == 0)
    def _():
        h_scr[...] = _rmsnorm(x_ref[...], g_ref[...]).astype(_BF16)

    @pl.when(i_b == 0)
    def _():
        zeros = jnp.zeros((2 * r, sh), _F32)
        carry_re[j_b] = zeros
        carry_im[j_b] = zeros

    def stages(x_a, u_a, z_a, x_b, u_b, z_b):
        n = tm // SCAN_PIECES
        l_re, l_im = _lam_rows(lre_b_ref, lim_b_ref, r)

        def recurrence(c, lo):
            c_re, c_im = c
            for k in range(lo // SUBLANES, (lo + n) // SUBLANES):
                for rows in (pl.ds(k * SUBLANES, SUBLANES), pl.ds(tm + k * SUBLANES, SUBLANES)):
                    c_re, c_im = (l_re * c_re - l_im * c_im + x_b[rows, 0:sh],
                                  l_re * c_im + l_im * c_re + x_b[rows, sh:2 * sh])
                    x_b[rows, 0:sh] = c_re
                    x_b[rows, sh:2 * sh] = c_im
            return c_re, c_im

        def readout(lo):
            states = jnp.concatenate([x_b[lo:lo + n, :], x_b[tm + lo:tm + lo + n, :]], axis=0)
            yy = _readout(states, cd_b_ref)
            low_rows = (lax.broadcasted_iota(jnp.int32, (n, LANES), 0) & r) == 0
            y_even = yy[:n]
            y_odd = yy[n:]
            y_first = jnp.where(low_rows, y_even[:, :LANES], pltpu.roll(y_odd[:, :LANES], r, 0))
            y_second = jnp.where(low_rows, pltpu.roll(y_even[:, LANES:], n - r, 0),
                                 y_odd[:, LANES:])
            y = jnp.concatenate([y_first, y_second], axis=1)
            rows = slice(lo, lo + n)
            y_ref[rows, :] = jax.nn.gelu(y + d_b_ref[...] * u_b[rows, :]).astype(_BF16)
            zs_ref[rows, :] = _silu(z_b[rows, :]).astype(_BF16)

        h = h_scr[...]
        u = _dot(h, wu_ref[...].astype(_BF16))
        u_a[...] = u
        carry = (carry_re[j_b], carry_im[j_b])
        for piece in range(SCAN_PIECES):
            if piece == SCAN_PIECES // 2:
                z_a[...] = _dot(h, wz_ref[...].astype(_BF16))
            carry = recurrence(carry, piece * n)
            readout(piece * n)
        carry_re[j_b], carry_im[j_b] = carry

        first_lanes = lax.broadcasted_iota(jnp.int32, (tm, CHUNK), 1) < LANES
        low_rows = (lax.broadcasted_iota(jnp.int32, (tm, CHUNK), 0) & r) == 0
        u_first = jnp.where(first_lanes, u, 0.0)
        u_second = jnp.where(first_lanes, 0.0, u)
        u_down = pltpu.roll(u, r, 0)
        u_up = pltpu.roll(u, tm - r, 0)
        even = jnp.where(low_rows, u_first, jnp.where(first_lanes, 0.0, u_down))
        odd = jnp.where(low_rows, jnp.where(first_lanes, u_up, 0.0), u_second)
        x_a[...] = _dot(jnp.concatenate([even, odd], axis=0).astype(_BF16), bd_ref[...])

    @pl.when(s & 1 == 0)
    def _():
        stages(x0, u0, z0, x1, u1, z1)

    @pl.when(s & 1 == 1)
    def _():
        stages(x1, u1, z1, x0, u0, z0)

    @pl.when((i_b == n_blocks - 1) & (s > 0))
    def _():
        c_re = carry_re[j_b]
        c_im = carry_im[j_b]
        nre_ref[:, 0:sh] = c_re[0:r]
        nre_ref[:, sh:2 * sh] = c_re[r:2 * r]
        nim_ref[:, 0:sh] = c_im[0:r]
        nim_ref[:, sh:2 * sh] = c_im[r:2 * r]

    @pl.when((i_a == n_blocks - 1) & (s < n_steps))
    def _():
        @pl.when(j_a == 0)
        def _():
            hs_scr[...] = _rmsnorm(xs_ref[...], g_ref[...]).astype(_BF16)

        hs = hs_scr[...]
        us = _dot(hs, wu_ref[...].astype(_BF16))
        zz = _dot(hs, wz_ref[...].astype(_BF16))
        first = lax.broadcasted_iota(jnp.int32, (rs, CHUNK), 1) < LANES
        lhs_s = jnp.concatenate([jnp.where(first, us, 0.0), jnp.where(first, 0.0, us)], axis=0)
        bus = _dot(lhs_s.astype(_BF16), bd_ref[...])
        p_re = jnp.concatenate([s0re_ref[:, 0:sh], s0re_ref[:, sh:2 * sh]], axis=0)
        p_im = jnp.concatenate([s0im_ref[:, 0:sh], s0im_ref[:, sh:2 * sh]], axis=0)
        ls_re, ls_im = _lam_rows(lre_a_ref, lim_a_ref, rs)
        n_re = ls_re * p_re - ls_im * p_im + bus[:, :sh]
        n_im = ls_re * p_im + ls_im * p_re + bus[:, sh:]
        nres_ref[:, 0:sh] = n_re[0:rs]
        nres_ref[:, sh:2 * sh] = n_re[rs:2 * rs]
        nims_ref[:, 0:sh] = n_im[0:rs]
        nims_ref[:, sh:2 * sh] = n_im[rs:2 * rs]
        yys = _readout(jnp.concatenate([n_re, n_im], axis=1), cd_a_ref)
        y_s = jnp.concatenate([yys[:rs, :LANES], yys[rs:, LANES:]], axis=1)
        ys_ref[...] = jax.nn.gelu(y_s + d_a_ref[...] * us).astype(_BF16)
        zss_ref[...] = _silu(zz).astype(_BF16)


def _ssm_scan(x, xs, norm_g, w_in, bd, cd, lam_re, lam_im, d_skip, s0_re, s0_im, layer,
              *, rows_per_step):
    m = x.shape[0]
    rs = xs.shape[0]
    r = rows_per_step
    tm = TM_SCAN
    n_blocks = m // tm
    n_steps = n_blocks * N_CHUNKS
    sh = STATES_PER_HALF

    def stage_a(s):
        return _divmod_pow2(jnp.minimum(s, n_steps - 1), N_CHUNKS)

    def stage_b(s):
        return _divmod_pow2(jnp.maximum(s - 1, 0), N_CHUNKS)

    def per_chunk(stage, a, b):
        return pl.BlockSpec((None, a, b), lambda s: (stage(s)[1], 0, 0))

    def skip_spec(stage):
        return pl.BlockSpec((None, 1, CHUNK), lambda s: (layer, 0, stage(s)[1]))

    def live_last(stage, rows, cols):
        def index_map(s):
            i, j = stage(s)
            return 0, jnp.where(i == n_blocks - 1, j, 0)
        return pl.BlockSpec((rows, cols), index_map)

    act_spec = pl.BlockSpec((tm, CHUNK), lambda s: stage_b(s))
    state_rows = pltpu.VMEM((2 * tm, 2 * sh), _F32)
    chunk_rows = pltpu.VMEM((tm, CHUNK), _F32)
    return pl.pallas_call(
        functools.partial(_ssm_scan_kernel, rows_per_step=r, n_blocks=n_blocks),
        grid=(n_steps + 1,),
        in_specs=[
            pl.BlockSpec((tm, D_MODEL), lambda s: (stage_a(s)[0], 0)),
            pl.BlockSpec((rs, D_MODEL), lambda s: (0, 0)),
            pl.BlockSpec((None, 1, D_MODEL), lambda s: (layer, 0, 0)),
            pl.BlockSpec((None, D_MODEL, CHUNK), lambda s: (layer, 0, stage_a(s)[1])),
            pl.BlockSpec((None, D_MODEL, CHUNK), lambda s: (layer, 0, N_CHUNKS + stage_a(s)[1])),
            per_chunk(stage_a, CHUNK, 2 * sh),
            per_chunk(stage_a, 2 * sh, CHUNK),
            per_chunk(stage_a, 2, sh), per_chunk(stage_a, 2, sh), skip_spec(stage_a),
            per_chunk(stage_b, 2 * sh, CHUNK),
            per_chunk(stage_b, 2, sh), per_chunk(stage_b, 2, sh), skip_spec(stage_b),
            live_last(stage_a, rs, STATES_PER_CHUNK), live_last(stage_a, rs, STATES_PER_CHUNK),
        ],
        out_specs=[
            act_spec, act_spec,
            live_last(stage_b, r, STATES_PER_CHUNK), live_last(stage_b, r, STATES_PER_CHUNK),
            live_last(stage_a, rs, CHUNK), live_last(stage_a, rs, CHUNK),
            live_last(stage_a, rs, STATES_PER_CHUNK), live_last(stage_a, rs, STATES_PER_CHUNK),
        ],
        out_shape=[
            jax.ShapeDtypeStruct((m, D_INNER), _BF16),
            jax.ShapeDtypeStruct((m, D_INNER), _BF16),
            jax.ShapeDtypeStruct((r, N_STATES), _F32),
            jax.ShapeDtypeStruct((r, N_STATES), _F32),
            jax.ShapeDtypeStruct((rs, D_INNER), _BF16),
            jax.ShapeDtypeStruct((rs, D_INNER), _BF16),
            jax.ShapeDtypeStruct((rs, N_STATES), _F32),
            jax.ShapeDtypeStruct((rs, N_STATES), _F32),
        ],
        scratch_shapes=[
            pltpu.VMEM((tm, D_MODEL), _BF16),
            pltpu.VMEM((rs, D_MODEL), _BF16),
            state_rows, chunk_rows, chunk_rows,
            state_rows, chunk_rows, chunk_rows,
            pltpu.VMEM((N_CHUNKS, 2 * r, sh), _F32),
            pltpu.VMEM((N_CHUNKS, 2 * r, sh), _F32),
        ],
        compiler_params=_compiler_params(1),
        name="ssm_scan",
    )(x, xs, norm_g, w_in, w_in, bd, cd, lam_re, lam_im, d_skip,
      cd, lam_re, lam_im, d_skip, s0_re, s0_im)


def _glu_kernel(y_ref, yj_ref, zs_ref, ys_ref, yjs_ref, zss_ref, wg_ref, bg_ref, o_ref, os_ref,
                w16, *, plan):
    active, i, cur = plan.stage(pl.program_id(0), wg_ref, w16)

    def gated(y_all, y_chunk, z_chunk):
        gate = jax.nn.sigmoid(_dot(y_all, w16[cur]) + bg_ref[...])
        return (y_chunk.astype(_F32) * gate * z_chunk.astype(_F32)).astype(_BF16)

    @pl.when(active)
    def _():
        tm = o_ref.shape[0]
        for lo in range(0, tm, tm // 2):
            rows = slice(lo, lo + tm // 2)
            o_ref[rows, :] = gated(y_ref[rows, :], yj_ref[rows, :], zs_ref[rows, :])

        @pl.when(i == plan.n_rows - 1)
        def _():
            os_ref[...] = gated(ys_ref[...], yjs_ref[...], zss_ref[...])


def _glu(y, zs, ys, zss, w_glu, b_glu, layer):
    m = y.shape[0]
    rs = ys.shape[0]
    tm = TM_ROWWISE
    plan = _StationaryWeights(D_INNER, D_INNER // WIDE, m // tm, layer)
    rows = plan.rows_spec(tm)
    sample_rows = plan.fixed_rows_spec(rs)
    return pl.pallas_call(
        functools.partial(_glu_kernel, plan=plan),
        grid=(plan.n_steps,),
        in_specs=[
            plan.rows_spec(tm, width_is_wide=False),
            rows, rows,
            pl.BlockSpec((rs, D_INNER), lambda s: (0, 0)),
            sample_rows, sample_rows,
            plan.weight_spec(),
            pl.BlockSpec((None, 1, WIDE), lambda s: (layer, 0, plan.col_row(s)[0])),
        ],
        out_specs=[rows, sample_rows],
        out_shape=[
            jax.ShapeDtypeStruct((m, D_INNER), _BF16),
            jax.ShapeDtypeStruct((rs, D_INNER), _BF16),
        ],
        scratch_shapes=[plan.scratch()],
        compiler_params=_compiler_params(1),
        name="glu",
    )(y, y, zs, ys, ys, zss, w_glu, b_glu)


def _final_norm_batch_major_kernel(x_ref, g_ref, o_ref, slab):
    nb, tt, _ = o_ref.shape
    xn = _rmsnorm(x_ref[...], g_ref[...])
    for k in range(D_MODEL // LANES):
        lanes = slice(k * LANES, (k + 1) * LANES)
        slab[k] = xn[:, lanes]
        for b in range(nb):
            o_ref[b, :, lanes] = slab[k, pl.ds(b, tt, stride=nb), :]


def _final_norm_batch_major(x, g, nb):
    seq = x.shape[0] // nb
    tt = TT_REORDER
    return pl.pallas_call(
        _final_norm_batch_major_kernel,
        grid=(seq // tt,),
        in_specs=[pl.BlockSpec((nb * tt, D_MODEL), lambda i: (i, 0)),
                  pl.BlockSpec((1, D_MODEL), lambda i: (0, 0))],
        out_specs=pl.BlockSpec((nb, tt, D_MODEL), lambda i: (0, i, 0)),
        out_shape=jax.ShapeDtypeStruct((nb, seq, D_MODEL), _F32),
        scratch_shapes=[pltpu.VMEM((D_MODEL // LANES, nb * tt, LANES), _F32)],
        compiler_params=_compiler_params(1),
        name="final_norm_batch_major",
    )(x, g)


def _final_norm_kernel(x_ref, g_ref, o_ref):
    o_ref[...] = _rmsnorm(x_ref[...], g_ref[...])


def _final_norm(x, g):
    m = x.shape[0]
    tm = min(m, TM_CONV)
    row_spec = pl.BlockSpec((tm, D_MODEL), lambda i: (i, 0))
    return pl.pallas_call(
        _final_norm_kernel,
        grid=(m // tm,),
        in_specs=[row_spec, pl.BlockSpec((1, D_MODEL), lambda i: (0, 0))],
        out_specs=row_spec,
        out_shape=jax.ShapeDtypeStruct((m, D_MODEL), _F32),
        compiler_params=_compiler_params(1),
        name="final_norm",
    )(x, g)


def kernel(x_prompt, x_sample, state_conv, state_ssm_re, state_ssm_im,
           conv_norm, conv_w_in, conv_w, conv_w_out,
           ssm_norm, ssm_w_in, ssm_a_re, ssm_a_im, ssm_log_dt, ssm_b_re, ssm_b_im,
           ssm_c_re, ssm_c_im, ssm_d, ssm_w_glu, ssm_b_glu, ssm_w_out, final_norm):
    batch, seq, _ = x_prompt.shape
    dec_batch = x_sample.shape[0]
    n_conv = conv_w_in.shape[0]
    n_ssm = ssm_w_in.shape[0]
    depth = n_conv + n_ssm

    xp = x_prompt
    xs = x_sample.reshape(dec_batch, D_MODEL)
    conv_norm3 = conv_norm.reshape(n_conv, 1, D_MODEL)
    ssm_norm3 = ssm_norm.reshape(n_ssm, 1, D_MODEL)
    ssm_d3 = ssm_d.reshape(n_ssm, 1, D_INNER)
    ssm_b_glu3 = ssm_b_glu.reshape(n_ssm, 1, D_INNER)

    conv_p, conv_s, re_p, im_p, re_s, im_s = [], [], [], [], [], []
    for layer in range(depth):
        l = layer // 2
        if layer % 2 == 0:
            y, ys, ns_p, ns_s = _conv_in(
                xp, xs, conv_norm3, conv_w_in, conv_w,
                state_conv[l].reshape(dec_batch, (CONV_WIDTH - 1) * D_INNER), l,
                rows_per_step=batch)
            xp, xs = _out_proj(y, ys, conv_w_out, xp, xs, l)
            conv_p.append(ns_p)
            conv_s.append(ns_s)
        else:
            lam_re, lam_im, bd, cd = _ssm_prep(
                ssm_a_re[l], ssm_a_im[l], ssm_log_dt[l], ssm_b_re[l], ssm_b_im[l],
                ssm_c_re[l], ssm_c_im[l])
            lam_re = lam_re.reshape(N_CHUNKS, 2, STATES_PER_HALF)
            lam_im = lam_im.reshape(N_CHUNKS, 2, STATES_PER_HALF)
            y, zs, hr_p, hi_p, ys, zss, hr_s, hi_s = _ssm_scan(
                xp, xs, ssm_norm3, ssm_w_in, bd, cd, lam_re, lam_im, ssm_d3,
                state_ssm_re[l].reshape(dec_batch, N_STATES),
                state_ssm_im[l].reshape(dec_batch, N_STATES),
                l, rows_per_step=batch)
            yy, yys = _glu(y, zs, ys, zss, ssm_w_glu, ssm_b_glu3, l)
            xp, xs = _out_proj(yy, yys, ssm_w_out, xp, xs, l)
            re_p.append(hr_p.reshape(batch, N_GROUPS, STATE_DIM))
            im_p.append(hi_p.reshape(batch, N_GROUPS, STATE_DIM))
            re_s.append(hr_s.reshape(dec_batch, N_GROUPS, STATE_DIM))
            im_s.append(hi_s.reshape(dec_batch, N_GROUPS, STATE_DIM))

    final_g = final_norm.reshape(1, D_MODEL)
    y_prompt = _final_norm_batch_major(xp, final_g, batch)
    xs = _final_norm(xs, final_g)
    y_sample = xs.reshape(dec_batch, 1, D_MODEL)
    return (y_prompt, y_sample, jnp.stack(conv_p), jnp.stack(conv_s),
            jnp.stack(re_p), jnp.stack(im_p), jnp.stack(re_s), jnp.stack(im_s))
```

```python
import functools

import jax
import jax.numpy as jnp
from jax import lax
from jax.experimental import pallas as pl
from jax.experimental.pallas import tpu as pltpu

D_MODEL = 2048
D_INNER = 2 * D_MODEL
CONV_WIDTH = 3
GROUP_SIZE = 16
N_GROUPS = D_INNER // GROUP_SIZE
STATE_DIM = 64
N_STATES = N_GROUPS * STATE_DIM
EPS = 1e-6

LANES = 128
SUBLANES = 8
CHUNK = 2 * LANES
N_CHUNKS = D_INNER // CHUNK
GROUPS_PER_HALF = LANES // GROUP_SIZE
STATES_PER_HALF = GROUPS_PER_HALF * STATE_DIM
STATES_PER_CHUNK = 2 * STATES_PER_HALF
VMEM_LIMIT_BYTES = 60000 * 1024

TM_ROWWISE = 512
TM_GLU = 1024
VMEM_LIMIT_GLU_BYTES = 60 * 1024 * 1024
TM_CONV = 1024
TM_SCAN = 512
WIDE = 1024
TT_REORDER = 256
W_PIECES = 8
SCAN_PIECES = 4

_F32 = jnp.float32
_BF16 = jnp.bfloat16


def _dot(a, b):
    return jnp.dot(a, b, preferred_element_type=_F32)


def _rmsnorm(x, g):
    return x * lax.rsqrt(jnp.mean(x * x, axis=-1, keepdims=True) + EPS) * g


def _to_time_major_tile(x_ref, lanes, slab, transform=lambda b, piece: piece):
    nb, tt, _ = x_ref.shape
    for b in range(nb):
        slab[pl.ds(b, tt, stride=nb), :] = transform(b, x_ref[b, :, lanes])
    return slab[...]


def _rmsnorm_to_time_major(x_ref, g_ref, h_ref, slab):
    nb = x_ref.shape[0]
    inv_rms = []
    for b in range(nb):
        xb = x_ref[b]
        inv_rms.append(lax.rsqrt(jnp.mean(xb * xb, axis=-1, keepdims=True) + EPS))
    for k in range(D_MODEL // LANES):
        lanes = slice(k * LANES, (k + 1) * LANES)
        tile = _to_time_major_tile(
            x_ref, lanes, slab, lambda b, piece: piece * inv_rms[b] * g_ref[:, lanes])
        h_ref[:, lanes] = tile.astype(_BF16)


def _silu(z):
    return z * jax.nn.sigmoid(z)


def _divmod_pow2(x, n):
    assert n > 0 and n & (n - 1) == 0
    return jnp.right_shift(x, n.bit_length() - 1), x & (n - 1)


def _compiler_params(n_axes=2):
    return pltpu.CompilerParams(
        dimension_semantics=("arbitrary",) * n_axes,
        vmem_limit_bytes=VMEM_LIMIT_BYTES)


def _last_block_cols(n_blocks, offset=0):
    return lambda i, j: (0, offset + jnp.where(i == n_blocks - 1, j, 0))


def _conv_in_kernel(x_ref, xs_ref, g_ref, wb_ref, wc_ref, wv_ref, wz_ref, cw_ref,
                    s2s_ref, s1s_ref,
                    y_ref, n2_ref, n1_ref, ys_ref, n2s_ref, n1s_ref,
                    h_scr, hs_scr, ubuf, carry, *maybe_slab, rows_per_step):
    i = pl.program_id(0)
    j = pl.program_id(1)
    last = i == pl.num_programs(0) - 1
    tm = h_scr.shape[0]
    r = rows_per_step

    @pl.when(j == 0)
    def _():
        if maybe_slab:
            _rmsnorm_to_time_major(x_ref, g_ref, h_scr, *maybe_slab)
        else:
            h_scr[...] = _rmsnorm(x_ref[...], g_ref[...]).astype(_BF16)

    @pl.when(i == 0)
    def _():
        carry[j] = jnp.zeros(carry.shape[1:], _F32)

    w16 = [w[...].astype(_BF16) for w in (wb_ref, wc_ref, wv_ref, wz_ref)]
    cw = cw_ref[...]

    h = h_scr[...]
    c = _dot(h, w16[1])
    v = _dot(h, w16[2])
    ubuf[0:2 * r, :] = carry[j]
    ubuf[2 * r:2 * r + tm, :] = c * v
    conv = (cw[0:1, :] * ubuf[0:tm, :] + cw[1:2, :] * ubuf[r:r + tm, :]
            + cw[2:3, :] * ubuf[2 * r:2 * r + tm, :])
    gate = _silu(_dot(h, w16[3]))
    b = _dot(h, w16[0])
    y_ref[...] = (b * conv * gate).astype(_BF16)
    carry[j] = ubuf[tm:tm + 2 * r, :]

    @pl.when(last)
    def _():
        n2_ref[...] = ubuf[tm:tm + r, :]
        n1_ref[...] = ubuf[tm + r:tm + 2 * r, :]

        @pl.when(j == 0)
        def _():
            hs_scr[...] = _rmsnorm(xs_ref[...], g_ref[...]).astype(_BF16)

        hs = hs_scr[...]
        bs, cs, vs, zs = [_dot(hs, w) for w in w16]
        us = cs * vs
        s1s = s1s_ref[...]
        convs = cw[0:1, :] * s2s_ref[...] + cw[1:2, :] * s1s + cw[2:3, :] * us
        ys_ref[...] = (bs * convs * _silu(zs)).astype(_BF16)
        n2s_ref[...] = s1s
        n1s_ref[...] = us


def _conv_in(x, xs, norm_g, w_in, conv_w, state_s, layer, *, rows_per_step):
    rs = xs.shape[0]
    r = rows_per_step
    tm = TM_CONV
    if x.ndim == 3:
        nb, seq, _ = x.shape
        m = nb * seq
        x_spec = pl.BlockSpec((nb, tm // nb, D_MODEL), lambda i, j: (0, i, 0))
        reorder_scratch = [pltpu.VMEM((tm, LANES), _F32)]
    else:
        m = x.shape[0]
        x_spec = pl.BlockSpec((tm, D_MODEL), lambda i, j: (i, 0))
        reorder_scratch = []
    n_blocks = m // tm
    w_in_spec = lambda part: pl.BlockSpec(
        (None, D_MODEL, CHUNK), lambda i, j: (layer, 0, part * N_CHUNKS + j))
    live_last = lambda rows, offset=0: pl.BlockSpec(
        (rows, CHUNK), _last_block_cols(n_blocks, offset))
    y, n2, n1, ys, n2s, n1s = pl.pallas_call(
        functools.partial(_conv_in_kernel, rows_per_step=r),
        grid=(n_blocks, N_CHUNKS),
        in_specs=[
            x_spec,
            pl.BlockSpec((rs, D_MODEL), lambda i, j: (0, 0)),
            pl.BlockSpec((None, 1, D_MODEL), lambda i, j: (layer, 0, 0)),
            w_in_spec(0), w_in_spec(1), w_in_spec(2), w_in_spec(3),
            pl.BlockSpec((None, CONV_WIDTH, CHUNK), lambda i, j: (layer, 0, j)),
            live_last(rs), live_last(rs, N_CHUNKS),
        ],
        out_specs=[
            pl.BlockSpec((tm, CHUNK), lambda i, j: (i, j)),
            live_last(r), live_last(r),
            live_last(rs), live_last(rs), live_last(rs),
        ],
        out_shape=[
            jax.ShapeDtypeStruct((m, D_INNER), _BF16),
            jax.ShapeDtypeStruct((r, D_INNER), _F32),
            jax.ShapeDtypeStruct((r, D_INNER), _F32),
            jax.ShapeDtypeStruct((rs, D_INNER), _BF16),
            jax.ShapeDtypeStruct((rs, D_INNER), _F32),
            jax.ShapeDtypeStruct((rs, D_INNER), _F32),
        ],
        scratch_shapes=[
            pltpu.VMEM((tm, D_MODEL), _BF16),
            pltpu.VMEM((rs, D_MODEL), _BF16),
            pltpu.VMEM((tm + 2 * r, CHUNK), _F32),
            pltpu.VMEM((N_CHUNKS, 2 * r, CHUNK), _F32),
        ] + reorder_scratch,
        compiler_params=_compiler_params(),
        name="conv_in",
    )(x, xs, norm_g, w_in, w_in, w_in, w_in, conv_w, state_s, state_s)
    return y, ys, jnp.stack([n2, n1], axis=1), jnp.stack([n2s, n1s], axis=1)


class _StationaryWeights:
    def __init__(self, k_dim, n_cols, n_rows, layer):
        self.k_dim, self.n_cols, self.n_rows, self.layer = k_dim, n_cols, n_rows, layer
        self.piece = k_dim // W_PIECES
        self.n_steps = W_PIECES + n_cols * n_rows

    def col_row(self, s):
        t = jnp.maximum(s - W_PIECES, 0)
        return _divmod_pow2(t, self.n_rows)

    def _piece_and_block(self, s):
        n, i = self.col_row(s)
        ahead = i - (self.n_rows - W_PIECES)
        piece = jnp.where(s < W_PIECES, s, jnp.where(ahead >= 0, ahead, W_PIECES - 1))
        block = jnp.where((s >= W_PIECES) & (ahead >= 0),
                          jnp.minimum(n + 1, self.n_cols - 1), n)
        return piece, block

    def weight_spec(self):
        def index_map(s):
            piece, block = self._piece_and_block(s)
            return self.layer, piece, block
        return pl.BlockSpec((None, self.piece, WIDE), index_map)

    def scratch(self):
        return pltpu.VMEM((2, self.k_dim, WIDE), _BF16)

    def rows_spec(self, tm, width_is_wide=True):
        if width_is_wide:
            return pl.BlockSpec((tm, WIDE), lambda s: self.col_row(s)[::-1])
        return pl.BlockSpec((tm, self.k_dim), lambda s: (self.col_row(s)[1], 0))

    def fixed_rows_spec(self, rows):
        return pl.BlockSpec((rows, WIDE), lambda s: (0, self.col_row(s)[0]))

    def stage(self, s, w_ref, w16):
        n, i = self.col_row(s)
        piece, _ = self._piece_and_block(s)
        active = s >= W_PIECES
        ahead = i - (self.n_rows - W_PIECES)
        dst = jnp.where(active, 1 - (n & 1), 0)

        @pl.when(jnp.logical_not(active) | (ahead >= 0))
        def _():
            rows = pl.ds(pl.multiple_of(piece * self.piece, self.piece), self.piece)
            w16[dst, rows, :] = w_ref[...].astype(_BF16)

        return active, i, n & 1


def _out_proj_kernel(y_ref, ys_ref, w_ref, x_ref, xs_ref, o_ref, os_ref, w16, *maybe_slab, plan):
    active, i, cur = plan.stage(pl.program_id(0), w_ref, w16)

    @pl.when(active)
    def _():
        update = _dot(y_ref[...], w16[cur])
        if maybe_slab:
            for k in range(WIDE // LANES):
                lanes = slice(k * LANES, (k + 1) * LANES)
                o_ref[:, lanes] = _to_time_major_tile(x_ref, lanes, *maybe_slab) + update[:, lanes]
        else:
            o_ref[...] = x_ref[...] + update

        @pl.when(i == plan.n_rows - 1)
        def _():
            os_ref[...] = xs_ref[...] + _dot(ys_ref[...], w16[cur])


def _out_proj(y, ys, w_out, x, xs, layer):
    m = y.shape[0]
    rs = xs.shape[0]
    tm = TM_ROWWISE
    plan = _StationaryWeights(D_INNER, D_MODEL // WIDE, m // tm, layer)
    if x.ndim == 3:
        nb = x.shape[0]
        x_spec = pl.BlockSpec((nb, tm // nb, WIDE), lambda s: (0,) + plan.col_row(s)[::-1])
        reorder_scratch = [pltpu.VMEM((tm, LANES), _F32)]
    else:
        x_spec = plan.rows_spec(tm)
        reorder_scratch = []
    return pl.pallas_call(
        functools.partial(_out_proj_kernel, plan=plan),
        grid=(plan.n_steps,),
        in_specs=[
            plan.rows_spec(tm, width_is_wide=False),
            pl.BlockSpec((rs, D_INNER), lambda s: (0, 0)),
            plan.weight_spec(),
            x_spec, plan.fixed_rows_spec(rs),
        ],
        out_specs=[plan.rows_spec(tm), plan.fixed_rows_spec(rs)],
        out_shape=[
            jax.ShapeDtypeStruct((m, D_MODEL), _F32),
            jax.ShapeDtypeStruct((rs, D_MODEL), _F32),
        ],
        scratch_shapes=[plan.scratch()] + reorder_scratch,
        compiler_params=_compiler_params(1),
        name="out_proj",
    )(y, ys, w_out, x, xs)


def _ssm_prep_kernel(are_ref, aim_ref, logdt_ref, btre_ref, btim_ref, cre_ref, cim_ref,
                     lre_ref, lim_ref, bd_ref, cd_ref):
    a_re = are_ref[...]
    a_im = aim_ref[...]
    dt = jnp.exp(logdt_ref[...])
    mag = jnp.exp(a_re * dt)
    l_re = mag * jnp.cos(a_im * dt)
    l_im = mag * jnp.sin(a_im * dt)
    lre_ref[...] = l_re
    lim_ref[...] = l_im
    n_re = l_re - 1.0
    den = a_re * a_re + a_im * a_im
    k_re = (n_re * a_re + l_im * a_im) / den
    k_im = (l_im * a_re - n_re * a_im) / den
    g = a_re.shape[0]
    expand = lambda k: jnp.broadcast_to(
        k[:, None, :], (g, GROUP_SIZE, STATE_DIM)).reshape(g * GROUP_SIZE, STATE_DIM)
    k_re = expand(k_re)
    k_im = expand(k_im)
    bt_re = btre_ref[...]
    bt_im = btim_ref[...]
    bb_re = k_re * bt_re - k_im * bt_im
    bb_im = k_re * bt_im + k_im * bt_re

    shape = (CHUNK, STATES_PER_HALF)
    log2 = lambda n: n.bit_length() - 1
    row_group = (jnp.right_shift(lax.broadcasted_iota(jnp.int32, shape, 0), log2(GROUP_SIZE))
                 & (GROUPS_PER_HALF - 1))
    col_group = jnp.right_shift(lax.broadcasted_iota(jnp.int32, shape, 1), log2(STATE_DIM))
    own = row_group == col_group
    spread = lambda v: jnp.where(own, jnp.concatenate([v] * GROUPS_PER_HALF, axis=1), 0.0)
    bd_ref[...] = jnp.concatenate([spread(bb_re), spread(bb_im)], axis=1).astype(_BF16)
    cd_ref[...] = jnp.concatenate(
        [spread(cre_ref[...]).T, -spread(cim_ref[...]).T], axis=0).astype(_BF16)


def _ssm_prep(a_re, a_im, log_dt, b_re, b_im, c_re, c_im):
    gc = CHUNK // GROUP_SIZE
    rows = N_GROUPS * GROUP_SIZE
    bt_re = b_re.transpose(0, 2, 1).reshape(rows, STATE_DIM)
    bt_im = b_im.transpose(0, 2, 1).reshape(rows, STATE_DIM)
    gp_spec = pl.BlockSpec((gc, STATE_DIM), lambda j: (j, 0))
    ghp_spec = pl.BlockSpec((CHUNK, STATE_DIM), lambda j: (j, 0))
    per_chunk = lambda a, b: pl.BlockSpec((None, a, b), lambda j: (j, 0, 0))
    return pl.pallas_call(
        _ssm_prep_kernel,
        grid=(N_CHUNKS,),
        in_specs=[gp_spec, gp_spec, pl.BlockSpec((gc, 1), lambda j: (j, 0)),
                  ghp_spec, ghp_spec, ghp_spec, ghp_spec],
        out_specs=[gp_spec, gp_spec, per_chunk(CHUNK, 2 * STATES_PER_HALF),
                   per_chunk(2 * STATES_PER_HALF, CHUNK)],
        out_shape=[
            jax.ShapeDtypeStruct((N_GROUPS, STATE_DIM), _F32),
            jax.ShapeDtypeStruct((N_GROUPS, STATE_DIM), _F32),
            jax.ShapeDtypeStruct((N_CHUNKS, CHUNK, 2 * STATES_PER_HALF), _BF16),
            jax.ShapeDtypeStruct((N_CHUNKS, 2 * STATES_PER_HALF, CHUNK), _BF16),
        ],
        compiler_params=_compiler_params(1),
        name="ssm_prep",
    )(a_re, a_im, log_dt.reshape(N_GROUPS, 1), bt_re, bt_im,
      c_re.reshape(rows, STATE_DIM), c_im.reshape(rows, STATE_DIM))


def _lam_rows(lre_ref, lim_ref, rows_per_half):
    first_half = (lax.broadcasted_iota(jnp.int32, (2 * rows_per_half, STATES_PER_HALF), 0)
                  < rows_per_half)
    return (jnp.where(first_half, lre_ref[0:1, :], lre_ref[1:2, :]),
            jnp.where(first_half, lim_ref[0:1, :], lim_ref[1:2, :]))


def _readout(x, cd_ref):
    return _dot(x.astype(_BF16), cd_ref[...])


def _ssm_scan_kernel(x_ref, xs_ref, g_ref, wu_ref, wz_ref, bd_ref,
                     cd_a_ref, lre_a_ref, lim_a_ref, d_a_ref,
                     cd_b_ref, lre_b_ref, lim_b_ref, d_b_ref,
                     s0re_ref, s0im_ref,
                     y_ref, zs_ref, nre_ref, nim_ref, ys_ref, zss_ref, nres_ref, nims_ref,
                     h_scr, hs_scr, x0, u0, z0, x1, u1, z1, carry_re, carry_im,
                     *, rows_per_step, n_blocks):
    s = pl.program_id(0)
    n_steps = n_blocks * N_CHUNKS
    step_a = jnp.minimum(s, n_steps - 1)
    step_b = jnp.maximum(s - 1, 0)
    i_a, j_a = _divmod_pow2(step_a, N_CHUNKS)
    i_b, j_b = _divmod_pow2(step_b, N_CHUNKS)
    tm = x_ref.shape[0]
    r = rows_per_step
    rs = xs_ref.shape[0]
    sh = STATES_PER_HALF
    assert 2 * r == SUBLANES

    @pl.when(s == 0)
    def _():
        for ref in (x1, u1, z1):
            ref[...] = jnp.zeros(ref.shape, _F32)

    @pl.when(j_a == 0)
    def _():
        h_scr[...] = _rmsnorm(x_ref[...], g_ref[...]).astype(_BF16)

    @pl.when(i_b == 0)
    def _():
        zeros = jnp.zeros((2 * r, sh), _F32)
        carry_re[j_b] = zeros
        carry_im[j_b] = zeros

    def stages(x_a, u_a, z_a, x_b, u_b, z_b):
        n = tm // SCAN_PIECES
        l_re, l_im = _lam_rows(lre_b_ref, lim_b_ref, r)

        def recurrence(c, lo):
            c_re, c_im = c
            for k in range(lo // SUBLANES, (lo + n) // SUBLANES):
                for rows in (pl.ds(k * SUBLANES, SUBLANES), pl.ds(tm + k * SUBLANES, SUBLANES)):
                    c_re, c_im = (l_re * c_re - l_im * c_im + x_b[rows, 0:sh],
                                  l_re * c_im + l_im * c_re + x_b[rows, sh:2 * sh])
                    x_b[rows, 0:sh] = c_re
                    x_b[rows, sh:2 * sh] = c_im
            return c_re, c_im

        def readout(lo):
            states = jnp.concatenate([x_b[lo:lo + n, :], x_b[tm + lo:tm + lo + n, :]], axis=0)
            yy = _readout(states, cd_b_ref)
            low_rows = (lax.broadcasted_iota(jnp.int32, (n, LANES), 0) & r) == 0
            y_even = yy[:n]
            y_odd = yy[n:]
            y_first = jnp.where(low_rows, y_even[:, :LANES], pltpu.roll(y_odd[:, :LANES], r, 0))
            y_second = jnp.where(low_rows, pltpu.roll(y_even[:, LANES:], n - r, 0),
                                 y_odd[:, LANES:])
            y = jnp.concatenate([y_first, y_second], axis=1)
            rows = slice(lo, lo + n)
            y_ref[rows, :] = jax.nn.gelu(y + d_b_ref[...] * u_b[rows, :]).astype(_BF16)
            zs_ref[rows, :] = _silu(z_b[rows, :]).astype(_BF16)

        h = h_scr[...]
        u = _dot(h, wu_ref[...].astype(_BF16))
        u_a[...] = u
        carry = (carry_re[j_b], carry_im[j_b])
        for piece in range(SCAN_PIECES):
            if piece == SCAN_PIECES // 2:
                z_a[...] = _dot(h, wz_ref[...].astype(_BF16))
            carry = recurrence(carry, piece * n)
            readout(piece * n)
        carry_re[j_b], carry_im[j_b] = carry

        first_lanes = lax.broadcasted_iota(jnp.int32, (tm, CHUNK), 1) < LANES
        low_rows = (lax.broadcasted_iota(jnp.int32, (tm, CHUNK), 0) & r) == 0
        u_first = jnp.where(first_lanes, u, 0.0)
        u_second = jnp.where(first_lanes, 0.0, u)
        u_down = pltpu.roll(u, r, 0)
        u_up = pltpu.roll(u, tm - r, 0)
        even = jnp.where(low_rows, u_first, jnp.where(first_lanes, 0.0, u_down))
        odd = jnp.where(low_rows, jnp.where(first_lanes, u_up, 0.0), u_second)
        x_a[...] = _dot(jnp.concatenate([even, odd], axis=0).astype(_BF16), bd_ref[...])

    @pl.when(s & 1 == 0)
    def _():
        stages(x0, u0, z0, x1, u1, z1)

    @pl.when(s & 1 == 1)
    def _():
        stages(x1, u1, z1, x0, u0, z0)

    @pl.when((i_b == n_blocks - 1) & (s > 0))
    def _():
        c_re = carry_re[j_b]
        c_im = carry_im[j_b]
        nre_ref[:, 0:sh] = c_re[0:r]
        nre_ref[:, sh:2 * sh] = c_re[r:2 * r]
        nim_ref[:, 0:sh] = c_im[0:r]
        nim_ref[:, sh:2 * sh] = c_im[r:2 * r]

    @pl.when((i_a == n_blocks - 1) & (s < n_steps))
    def _():
        @pl.when(j_a == 0)
        def _():
            hs_scr[...] = _rmsnorm(xs_ref[...], g_ref[...]).astype(_BF16)

        hs = hs_scr[...]
        us = _dot(hs, wu_ref[...].astype(_BF16))
        zz = _dot(hs, wz_ref[...].astype(_BF16))
        first = lax.broadcasted_iota(jnp.int32, (rs, CHUNK), 1) < LANES
        lhs_s = jnp.concatenate([jnp.where(first, us, 0.0), jnp.where(first, 0.0, us)], axis=0)
        bus = _dot(lhs_s.astype(_BF16), bd_ref[...])
        p_re = jnp.concatenate([s0re_ref[:, 0:sh], s0re_ref[:, sh:2 * sh]], axis=0)
        p_im = jnp.concatenate([s0im_ref[:, 0:sh], s0im_ref[:, sh:2 * sh]], axis=0)
        ls_re, ls_im = _lam_rows(lre_a_ref, lim_a_ref, rs)
        n_re = ls_re * p_re - ls_im * p_im + bus[:, :sh]
        n_im = ls_re * p_im + ls_im * p_re + bus[:, sh:]
        nres_ref[:, 0:sh] = n_re[0:rs]
        nres_ref[:, sh:2 * sh] = n_re[rs:2 * rs]
        nims_ref[:, 0:sh] = n_im[0:rs]
        nims_ref[:, sh:2 * sh] = n_im[rs:2 * rs]
        yys = _readout(jnp.concatenate([n_re, n_im], axis=1), cd_a_ref)
        y_s = jnp.concatenate([yys[:rs, :LANES], yys[rs:, LANES:]], axis=1)
        ys_ref[...] = jax.nn.gelu(y_s + d_a_ref[...] * us).astype(_BF16)
        zss_ref[...] = _silu(zz).astype(_BF16)


def _ssm_scan(x, xs, norm_g, w_in, bd, cd, lam_re, lam_im, d_skip, s0_re, s0_im, layer,
              *, rows_per_step):
    m = x.shape[0]
    rs = xs.shape[0]
    r = rows_per_step
    tm = TM_SCAN
    n_blocks = m // tm
    n_steps = n_blocks * N_CHUNKS
    sh = STATES_PER_HALF

    def stage_a(s):
        return _divmod_pow2(jnp.minimum(s, n_steps - 1), N_CHUNKS)

    def stage_b(s):
        return _divmod_pow2(jnp.maximum(s - 1, 0), N_CHUNKS)

    def per_chunk(stage, a, b):
        return pl.BlockSpec((None, a, b), lambda s: (stage(s)[1], 0, 0))

    def skip_spec(stage):
        return pl.BlockSpec((None, 1, CHUNK), lambda s: (layer, 0, stage(s)[1]))

    def live_last(stage, rows, cols):
        def index_map(s):
            i, j = stage(s)
            return 0, jnp.where(i == n_blocks - 1, j, 0)
        return pl.BlockSpec((rows, cols), index_map)

    act_spec = pl.BlockSpec((tm, CHUNK), lambda s: stage_b(s))
    state_rows = pltpu.VMEM((2 * tm, 2 * sh), _F32)
    chunk_rows = pltpu.VMEM((tm, CHUNK), _F32)
    return pl.pallas_call(
        functools.partial(_ssm_scan_kernel, rows_per_step=r, n_blocks=n_blocks),
        grid=(n_steps + 1,),
        in_specs=[
            pl.BlockSpec((tm, D_MODEL), lambda s: (stage_a(s)[0], 0)),
            pl.BlockSpec((rs, D_MODEL), lambda s: (0, 0)),
            pl.BlockSpec((None, 1, D_MODEL), lambda s: (layer, 0, 0)),
            pl.BlockSpec((None, D_MODEL, CHUNK), lambda s: (layer, 0, stage_a(s)[1])),
            pl.BlockSpec((None, D_MODEL, CHUNK), lambda s: (layer, 0, N_CHUNKS + stage_a(s)[1])),
            per_chunk(stage_a, CHUNK, 2 * sh),
            per_chunk(stage_a, 2 * sh, CHUNK),
            per_chunk(stage_a, 2, sh), per_chunk(stage_a, 2, sh), skip_spec(stage_a),
            per_chunk(stage_b, 2 * sh, CHUNK),
            per_chunk(stage_b, 2, sh), per_chunk(stage_b, 2, sh), skip_spec(stage_b),
            live_last(stage_a, rs, STATES_PER_CHUNK), live_last(stage_a, rs, STATES_PER_CHUNK),
        ],
        out_specs=[
            act_spec, act_spec,
            live_last(stage_b, r, STATES_PER_CHUNK), live_last(stage_b, r, STATES_PER_CHUNK),
            live_last(stage_a, rs, CHUNK), live_last(stage_a, rs, CHUNK),
            live_last(stage_a, rs, STATES_PER_CHUNK), live_last(stage_a, rs, STATES_PER_CHUNK),
        ],
        out_shape=[
            jax.ShapeDtypeStruct((m, D_INNER), _BF16),
            jax.ShapeDtypeStruct((m, D_INNER), _BF16),
            jax.ShapeDtypeStruct((r, N_STATES), _F32),
            jax.ShapeDtypeStruct((r, N_STATES), _F32),
            jax.ShapeDtypeStruct((rs, D_INNER), _BF16),
            jax.ShapeDtypeStruct((rs, D_INNER), _BF16),
            jax.ShapeDtypeStruct((rs, N_STATES), _F32),
            jax.ShapeDtypeStruct((rs, N_STATES), _F32),
        ],
        scratch_shapes=[
            pltpu.VMEM((tm, D_MODEL), _BF16),
            pltpu.VMEM((rs, D_MODEL), _BF16),
            state_rows, chunk_rows, chunk_rows,
            state_rows, chunk_rows, chunk_rows,
            pltpu.VMEM((N_CHUNKS, 2 * r, sh), _F32),
            pltpu.VMEM((N_CHUNKS, 2 * r, sh), _F32),
        ],
        compiler_params=_compiler_params(1),
        name="ssm_scan",
    )(x, xs, norm_g, w_in, w_in, bd, cd, lam_re, lam_im, d_skip,
      cd, lam_re, lam_im, d_skip, s0_re, s0_im)


def _glu_kernel(y_ref, yj_ref, zs_ref, ys_ref, yjs_ref, zss_ref, wg_ref, bg_ref, o_ref, os_ref,
                w16, *, plan):
    active, i, cur = plan.stage(pl.program_id(0), wg_ref, w16)

    def gated(y_all, y_chunk, z_chunk):
        gate = jax.nn.sigmoid(_dot(y_all, w16[cur]) + bg_ref[...])
        return (y_chunk.astype(_F32) * gate * z_chunk.astype(_F32)).astype(_BF16)

    @pl.when(active)
    def _():
        o_ref[...] = gated(y_ref[...], yj_ref[...], zs_ref[...])

        @pl.when(i == plan.n_rows - 1)
        def _():
            os_ref[...] = gated(ys_ref[...], yjs_ref[...], zss_ref[...])


def _glu(y, zs, ys, zss, w_glu, b_glu, layer):
    m = y.shape[0]
    rs = ys.shape[0]
    tm = TM_GLU
    plan = _StationaryWeights(D_INNER, D_INNER // WIDE, m // tm, layer)
    rows = plan.rows_spec(tm)
    sample_rows = plan.fixed_rows_spec(rs)
    return pl.pallas_call(
        functools.partial(_glu_kernel, plan=plan),
        grid=(plan.n_steps,),
        in_specs=[
            plan.rows_spec(tm, width_is_wide=False),
            rows, rows,
            pl.BlockSpec((rs, D_INNER), lambda s: (0, 0)),
            sample_rows, sample_rows,
            plan.weight_spec(),
            pl.BlockSpec((None, 1, WIDE), lambda s: (layer, 0, plan.col_row(s)[0])),
        ],
        out_specs=[rows, sample_rows],
        out_shape=[
            jax.ShapeDtypeStruct((m, D_INNER), _BF16),
            jax.ShapeDtypeStruct((rs, D_INNER), _BF16),
        ],
        scratch_shapes=[plan.scratch()],
        compiler_params=pltpu.CompilerParams(
            dimension_semantics=("arbitrary",), vmem_limit_bytes=VMEM_LIMIT_GLU_BYTES),
        name="glu",
    )(y, y, zs, ys, ys, zss, w_glu, b_glu)


def _final_norm_batch_major_kernel(x_ref, g_ref, o_ref, slab):
    nb, tt, _ = o_ref.shape
    xn = _rmsnorm(x_ref[...], g_ref[...])
    for k in range(D_MODEL // LANES):
        lanes = slice(k * LANES, (k + 1) * LANES)
        slab[k] = xn[:, lanes]
        for b in range(nb):
            o_ref[b, :, lanes] = slab[k, pl.ds(b, tt, stride=nb), :]


def _final_norm_batch_major(x, g, nb):
    seq = x.shape[0] // nb
    tt = TT_REORDER
    return pl.pallas_call(
        _final_norm_batch_major_kernel,
        grid=(seq // tt,),
        in_specs=[pl.BlockSpec((nb * tt, D_MODEL), lambda i: (i, 0)),
                  pl.BlockSpec((1, D_MODEL), lambda i: (0, 0))],
        out_specs=pl.BlockSpec((nb, tt, D_MODEL), lambda i: (0, i, 0)),
        out_shape=jax.ShapeDtypeStruct((nb, seq, D_MODEL), _F32),
        scratch_shapes=[pltpu.VMEM((D_MODEL // LANES, nb * tt, LANES), _F32)],
        compiler_params=_compiler_params(1),
        name="final_norm_batch_major",
    )(x, g)


def _final_norm_kernel(x_ref, g_ref, o_ref):
    o_ref[...] = _rmsnorm(x_ref[...], g_ref[...])


def _final_norm(x, g):
    m = x.shape[0]
    tm = min(m, TM_CONV)
    row_spec = pl.BlockSpec((tm, D_MODEL), lambda i: (i, 0))
    return pl.pallas_call(
        _final_norm_kernel,
        grid=(m // tm,),
        in_specs=[row_spec, pl.BlockSpec((1, D_MODEL), lambda i: (0, 0))],
        out_specs=row_spec,
        out_shape=jax.ShapeDtypeStruct((m, D_MODEL), _F32),
        compiler_params=_compiler_params(1),
        name="final_norm",
    )(x, g)


def kernel(x_prompt, x_sample, state_conv, state_ssm_re, state_ssm_im,
           conv_norm, conv_w_in, conv_w, conv_w_out,
           ssm_norm, ssm_w_in, ssm_a_re, ssm_a_im, ssm_log_dt, ssm_b_re, ssm_b_im,
           ssm_c_re, ssm_c_im, ssm_d, ssm_w_glu, ssm_b_glu, ssm_w_out, final_norm):
    batch, seq, _ = x_prompt.shape
    dec_batch = x_sample.shape[0]
    n_conv = conv_w_in.shape[0]
    n_ssm = ssm_w_in.shape[0]
    depth = n_conv + n_ssm

    xp = x_prompt
    xs = x_sample.reshape(dec_batch, D_MODEL)
    conv_norm3 = conv_norm.reshape(n_conv, 1, D_MODEL)
    ssm_norm3 = ssm_norm.reshape(n_ssm, 1, D_MODEL)
    ssm_d3 = ssm_d.reshape(n_ssm, 1, D_INNER)
    ssm_b_glu3 = ssm_b_glu.reshape(n_ssm, 1, D_INNER)

    conv_p, conv_s, re_p, im_p, re_s, im_s = [], [], [], [], [], []
    for layer in range(depth):
        l = layer // 2
        if layer % 2 == 0:
            y, ys, ns_p, ns_s = _conv_in(
                xp, xs, conv_norm3, conv_w_in, conv_w,
                state_conv[l].reshape(dec_batch, (CONV_WIDTH - 1) * D_INNER), l,
                rows_per_step=batch)
            xp, xs = _out_proj(y, ys, conv_w_out, xp, xs, l)
            conv_p.append(ns_p)
            conv_s.append(ns_s)
        else:
            lam_re, lam_im, bd, cd = _ssm_prep(
                ssm_a_re[l], ssm_a_im[l], ssm_log_dt[l], ssm_b_re[l], ssm_b_im[l],
                ssm_c_re[l], ssm_c_im[l])
            lam_re = lam_re.reshape(N_CHUNKS, 2, STATES_PER_HALF)
            lam_im = lam_im.reshape(N_CHUNKS, 2, STATES_PER_HALF)
            y, zs, hr_p, hi_p, ys, zss, hr_s, hi_s = _ssm_scan(
                xp, xs, ssm_norm3, ssm_w_in, bd, cd, lam_re, lam_im, ssm_d3,
                state_ssm_re[l].reshape(dec_batch, N_STATES),
                state_ssm_im[l].reshape(dec_batch, N_STATES),
                l, rows_per_step=batch)
            yy, yys = _glu(y, zs, ys, zss, ssm_w_glu, ssm_b_glu3, l)
            xp, xs = _out_proj(yy, yys, ssm_w_out, xp, xs, l)
            re_p.append(hr_p.reshape(batch, N_GROUPS, STATE_DIM))
            im_p.append(hi_p.reshape(batch, N_GROUPS, STATE_DIM))
            re_s.append(hr_s.reshape(dec_batch, N_GROUPS, STATE_DIM))
            im_s.append(hi_s.reshape(dec_batch, N_GROUPS, STATE_DIM))

    final_g = final_norm.reshape(1, D_MODEL)
    y_prompt = _final_norm_batch_major(xp, final_g, batch)
    xs = _final_norm(xs, final_g)
    y_sample = xs.reshape(dec_batch, 1, D_MODEL)
    return (y_prompt, y_sample, jnp.stack(conv_p), jnp.stack(conv_s),
            jnp.stack(re_p), jnp.stack(im_p), jnp.stack(re_s), jnp.stack(im_s))
```

```python
import functools

import jax
import jax.numpy as jnp
from jax import lax
from jax.experimental import pallas as pl
from jax.experimental.pallas import tpu as pltpu

D_MODEL = 2048
D_INNER = 2 * D_MODEL
CONV_WIDTH = 3
GROUP_SIZE = 16
N_GROUPS = D_INNER // GROUP_SIZE
STATE_DIM = 64
N_STATES = N_GROUPS * STATE_DIM
EPS = 1e-6

LANES = 128
SUBLANES = 8
CHUNK = 2 * LANES
N_CHUNKS = D_INNER // CHUNK
GROUPS_PER_HALF = LANES // GROUP_SIZE
STATES_PER_HALF = GROUPS_PER_HALF * STATE_DIM
STATES_PER_CHUNK = 2 * STATES_PER_HALF
VMEM_LIMIT_BYTES = 60000 * 1024

TM_ROWWISE = 512
TM_GLU = 1024
VMEM_LIMIT_GLU_BYTES = 60 * 1024 * 1024
TM_CONV = 1024
TM_SCAN = 1024
WIDE = 1024
TT_REORDER = 256
W_PIECES = 8
SCAN_PIECES = 4

_F32 = jnp.float32
_BF16 = jnp.bfloat16


def _dot(a, b):
    return jnp.dot(a, b, preferred_element_type=_F32)


def _rmsnorm(x, g):
    return x * lax.rsqrt(jnp.mean(x * x, axis=-1, keepdims=True) + EPS) * g


def _to_time_major_tile(x_ref, lanes, slab, transform=lambda b, piece: piece):
    nb, tt, _ = x_ref.shape
    for b in range(nb):
        slab[pl.ds(b, tt, stride=nb), :] = transform(b, x_ref[b, :, lanes])
    return slab[...]


def _rmsnorm_to_time_major(x_ref, g_ref, h_ref, slab):
    nb = x_ref.shape[0]
    inv_rms = []
    for b in range(nb):
        xb = x_ref[b]
        inv_rms.append(lax.rsqrt(jnp.mean(xb * xb, axis=-1, keepdims=True) + EPS))
    for k in range(D_MODEL // LANES):
        lanes = slice(k * LANES, (k + 1) * LANES)
        tile = _to_time_major_tile(
            x_ref, lanes, slab, lambda b, piece: piece * inv_rms[b] * g_ref[:, lanes])
        h_ref[:, lanes] = tile.astype(_BF16)


def _silu(z):
    return z * jax.nn.sigmoid(z)


def _divmod_pow2(x, n):
    assert n > 0 and n & (n - 1) == 0
    return jnp.right_shift(x, n.bit_length() - 1), x & (n - 1)


def _compiler_params(n_axes=2):
    return pltpu.CompilerParams(
        dimension_semantics=("arbitrary",) * n_axes,
        vmem_limit_bytes=VMEM_LIMIT_BYTES)


def _last_block_cols(n_blocks, offset=0):
    return lambda i, j: (0, offset + jnp.where(i == n_blocks - 1, j, 0))


def _conv_in_kernel(x_ref, xs_ref, g_ref, wb_ref, wc_ref, wv_ref, wz_ref, cw_ref,
                    s2s_ref, s1s_ref,
                    y_ref, n2_ref, n1_ref, ys_ref, n2s_ref, n1s_ref,
                    h_scr, hs_scr, ubuf, carry, *maybe_slab, rows_per_step):
    i = pl.program_id(0)
    j = pl.program_id(1)
    last = i == pl.num_programs(0) - 1
    tm = h_scr.shape[0]
    r = rows_per_step

    @pl.when(j == 0)
    def _():
        if maybe_slab:
            _rmsnorm_to_time_major(x_ref, g_ref, h_scr, *maybe_slab)
        else:
            h_scr[...] = _rmsnorm(x_ref[...], g_ref[...]).astype(_BF16)

    @pl.when(i == 0)
    def _():
        carry[j] = jnp.zeros(carry.shape[1:], _F32)

    w16 = [w[...].astype(_BF16) for w in (wb_ref, wc_ref, wv_ref, wz_ref)]
    cw = cw_ref[...]

    h = h_scr[...]
    c = _dot(h, w16[1])
    v = _dot(h, w16[2])
    ubuf[0:2 * r, :] = carry[j]
    ubuf[2 * r:2 * r + tm, :] = c * v
    conv = (cw[0:1, :] * ubuf[0:tm, :] + cw[1:2, :] * ubuf[r:r + tm, :]
            + cw[2:3, :] * ubuf[2 * r:2 * r + tm, :])
    gate = _silu(_dot(h, w16[3]))
    b = _dot(h, w16[0])
    y_ref[...] = (b * conv * gate).astype(_BF16)
    carry[j] = ubuf[tm:tm + 2 * r, :]

    @pl.when(last)
    def _():
        n2_ref[...] = ubuf[tm:tm + r, :]
        n1_ref[...] = ubuf[tm + r:tm + 2 * r, :]

        @pl.when(j == 0)
        def _():
            hs_scr[...] = _rmsnorm(xs_ref[...], g_ref[...]).astype(_BF16)

        hs = hs_scr[...]
        bs, cs, vs, zs = [_dot(hs, w) for w in w16]
        us = cs * vs
        s1s = s1s_ref[...]
        convs = cw[0:1, :] * s2s_ref[...] + cw[1:2, :] * s1s + cw[2:3, :] * us
        ys_ref[...] = (bs * convs * _silu(zs)).astype(_BF16)
        n2s_ref[...] = s1s
        n1s_ref[...] = us


def _conv_in(x, xs, norm_g, w_in, conv_w, state_s, layer, *, rows_per_step):
    rs = xs.shape[0]
    r = rows_per_step
    tm = TM_CONV
    if x.ndim == 3:
        nb, seq, _ = x.shape
        m = nb * seq
        x_spec = pl.BlockSpec((nb, tm // nb, D_MODEL), lambda i, j: (0, i, 0))
        reorder_scratch = [pltpu.VMEM((tm, LANES), _F32)]
    else:
        m = x.shape[0]
        x_spec = pl.BlockSpec((tm, D_MODEL), lambda i, j: (i, 0))
        reorder_scratch = []
    n_blocks = m // tm
    w_in_spec = lambda part: pl.BlockSpec(
        (None, D_MODEL, CHUNK), lambda i, j: (layer, 0, part * N_CHUNKS + j))
    live_last = lambda rows, offset=0: pl.BlockSpec(
        (rows, CHUNK), _last_block_cols(n_blocks, offset))
    y, n2, n1, ys, n2s, n1s = pl.pallas_call(
        functools.partial(_conv_in_kernel, rows_per_step=r),
        grid=(n_blocks, N_CHUNKS),
        in_specs=[
            x_spec,
            pl.BlockSpec((rs, D_MODEL), lambda i, j: (0, 0)),
            pl.BlockSpec((None, 1, D_MODEL), lambda i, j: (layer, 0, 0)),
            w_in_spec(0), w_in_spec(1), w_in_spec(2), w_in_spec(3),
            pl.BlockSpec((None, CONV_WIDTH, CHUNK), lambda i, j: (layer, 0, j)),
            live_last(rs), live_last(rs, N_CHUNKS),
        ],
        out_specs=[
            pl.BlockSpec((tm, CHUNK), lambda i, j: (i, j)),
            live_last(r), live_last(r),
            live_last(rs), live_last(rs), live_last(rs),
        ],
        out_shape=[
            jax.ShapeDtypeStruct((m, D_INNER), _BF16),
            jax.ShapeDtypeStruct((r, D_INNER), _F32),
            jax.ShapeDtypeStruct((r, D_INNER), _F32),
            jax.ShapeDtypeStruct((rs, D_INNER), _BF16),
            jax.ShapeDtypeStruct((rs, D_INNER), _F32),
            jax.ShapeDtypeStruct((rs, D_INNER), _F32),
        ],
        scratch_shapes=[
            pltpu.VMEM((tm, D_MODEL), _BF16),
            pltpu.VMEM((rs, D_MODEL), _BF16),
            pltpu.VMEM((tm + 2 * r, CHUNK), _F32),
            pltpu.VMEM((N_CHUNKS, 2 * r, CHUNK), _F32),
        ] + reorder_scratch,
        compiler_params=_compiler_params(),
        name="conv_in",
    )(x, xs, norm_g, w_in, w_in, w_in, w_in, conv_w, state_s, state_s)
    return y, ys, jnp.stack([n2, n1], axis=1), jnp.stack([n2s, n1s], axis=1)


class _StationaryWeights:
    def __init__(self, k_dim, n_cols, n_rows, layer):
        self.k_dim, self.n_cols, self.n_rows, self.layer = k_dim, n_cols, n_rows, layer
        self.piece = k_dim // W_PIECES
        self.n_steps = W_PIECES + n_cols * n_rows

    def col_row(self, s):
        t = jnp.maximum(s - W_PIECES, 0)
        return _divmod_pow2(t, self.n_rows)

    def _piece_and_block(self, s):
        n, i = self.col_row(s)
        ahead = i - (self.n_rows - W_PIECES)
        piece = jnp.where(s < W_PIECES, s, jnp.where(ahead >= 0, ahead, W_PIECES - 1))
        block = jnp.where((s >= W_PIECES) & (ahead >= 0),
                          jnp.minimum(n + 1, self.n_cols - 1), n)
        return piece, block

    def weight_spec(self):
        def index_map(s):
            piece, block = self._piece_and_block(s)
            return self.layer, piece, block
        return pl.BlockSpec((None, self.piece, WIDE), index_map)

    def scratch(self):
        return pltpu.VMEM((2, self.k_dim, WIDE), _BF16)

    def rows_spec(self, tm, width_is_wide=True):
        if width_is_wide:
            return pl.BlockSpec((tm, WIDE), lambda s: self.col_row(s)[::-1])
        return pl.BlockSpec((tm, self.k_dim), lambda s: (self.col_row(s)[1], 0))

    def fixed_rows_spec(self, rows):
        return pl.BlockSpec((rows, WIDE), lambda s: (0, self.col_row(s)[0]))

    def stage(self, s, w_ref, w16):
        n, i = self.col_row(s)
        piece, _ = self._piece_and_block(s)
        active = s >= W_PIECES
        ahead = i - (self.n_rows - W_PIECES)
        dst = jnp.where(active, 1 - (n & 1), 0)

        @pl.when(jnp.logical_not(active) | (ahead >= 0))
        def _():
            rows = pl.ds(pl.multiple_of(piece * self.piece, self.piece), self.piece)
            w16[dst, rows, :] = w_ref[...].astype(_BF16)

        return active, i, n & 1


def _out_proj_kernel(y_ref, ys_ref, w_ref, x_ref, xs_ref, o_ref, os_ref, w16, *maybe_slab, plan):
    active, i, cur = plan.stage(pl.program_id(0), w_ref, w16)

    @pl.when(active)
    def _():
        update = _dot(y_ref[...], w16[cur])
        if maybe_slab:
            for k in range(WIDE // LANES):
                lanes = slice(k * LANES, (k + 1) * LANES)
                o_ref[:, lanes] = _to_time_major_tile(x_ref, lanes, *maybe_slab) + update[:, lanes]
        else:
            o_ref[...] = x_ref[...] + update

        @pl.when(i == plan.n_rows - 1)
        def _():
            os_ref[...] = xs_ref[...] + _dot(ys_ref[...], w16[cur])


def _out_proj(y, ys, w_out, x, xs, layer):
    m = y.shape[0]
    rs = xs.shape[0]
    tm = TM_ROWWISE
    plan = _StationaryWeights(D_INNER, D_MODEL // WIDE, m // tm, layer)
    if x.ndim == 3:
        nb = x.shape[0]
        x_spec = pl.BlockSpec((nb, tm // nb, WIDE), lambda s: (0,) + plan.col_row(s)[::-1])
        reorder_scratch = [pltpu.VMEM((tm, LANES), _F32)]
    else:
        x_spec = plan.rows_spec(tm)
        reorder_scratch = []
    return pl.pallas_call(
        functools.partial(_out_proj_kernel, plan=plan),
        grid=(plan.n_steps,),
        in_specs=[
            plan.rows_spec(tm, width_is_wide=False),
            pl.BlockSpec((rs, D_INNER), lambda s: (0, 0)),
            plan.weight_spec(),
            x_spec, plan.fixed_rows_spec(rs),
        ],
        out_specs=[plan.rows_spec(tm), plan.fixed_rows_spec(rs)],
        out_shape=[
            jax.ShapeDtypeStruct((m, D_MODEL), _F32),
            jax.ShapeDtypeStruct((rs, D_MODEL), _F32),
        ],
        scratch_shapes=[plan.scratch()] + reorder_scratch,
        compiler_params=_compiler_params(1),
        name="out_proj",
    )(y, ys, w_out, x, xs)


def _ssm_prep_kernel(are_ref, aim_ref, logdt_ref, btre_ref, btim_ref, cre_ref, cim_ref,
                     lre_ref, lim_ref, bd_ref, cd_ref):
    a_re = are_ref[...]
    a_im = aim_ref[...]
    dt = jnp.exp(logdt_ref[...])
    mag = jnp.exp(a_re * dt)
    l_re = mag * jnp.cos(a_im * dt)
    l_im = mag * jnp.sin(a_im * dt)
    lre_ref[...] = l_re
    lim_ref[...] = l_im
    n_re = l_re - 1.0
    den = a_re * a_re + a_im * a_im
    k_re = (n_re * a_re + l_im * a_im) / den
    k_im = (l_im * a_re - n_re * a_im) / den
    g = a_re.shape[0]
    expand = lambda k: jnp.broadcast_to(
        k[:, None, :], (g, GROUP_SIZE, STATE_DIM)).reshape(g * GROUP_SIZE, STATE_DIM)
    k_re = expand(k_re)
    k_im = expand(k_im)
    bt_re = btre_ref[...]
    bt_im = btim_ref[...]
    bb_re = k_re * bt_re - k_im * bt_im
    bb_im = k_re * bt_im + k_im * bt_re

    shape = (CHUNK, STATES_PER_HALF)
    log2 = lambda n: n.bit_length() - 1
    row_group = (jnp.right_shift(lax.broadcasted_iota(jnp.int32, shape, 0), log2(GROUP_SIZE))
                 & (GROUPS_PER_HALF - 1))
    col_group = jnp.right_shift(lax.broadcasted_iota(jnp.int32, shape, 1), log2(STATE_DIM))
    own = row_group == col_group
    spread = lambda v: jnp.where(own, jnp.concatenate([v] * GROUPS_PER_HALF, axis=1), 0.0)
    bd_ref[...] = jnp.concatenate([spread(bb_re), spread(bb_im)], axis=1).astype(_BF16)
    cd_ref[...] = jnp.concatenate(
        [spread(cre_ref[...]).T, -spread(cim_ref[...]).T], axis=0).astype(_BF16)


def _ssm_prep(a_re, a_im, log_dt, b_re, b_im, c_re, c_im):
    gc = CHUNK // GROUP_SIZE
    rows = N_GROUPS * GROUP_SIZE
    bt_re = b_re.transpose(0, 2, 1).reshape(rows, STATE_DIM)
    bt_im = b_im.transpose(0, 2, 1).reshape(rows, STATE_DIM)
    gp_spec = pl.BlockSpec((gc, STATE_DIM), lambda j: (j, 0))
    ghp_spec = pl.BlockSpec((CHUNK, STATE_DIM), lambda j: (j, 0))
    per_chunk = lambda a, b: pl.BlockSpec((None, a, b), lambda j: (j, 0, 0))
    return pl.pallas_call(
        _ssm_prep_kernel,
        grid=(N_CHUNKS,),
        in_specs=[gp_spec, gp_spec, pl.BlockSpec((gc, 1), lambda j: (j, 0)),
                  ghp_spec, ghp_spec, ghp_spec, ghp_spec],
        out_specs=[gp_spec, gp_spec, per_chunk(CHUNK, 2 * STATES_PER_HALF),
                   per_chunk(2 * STATES_PER_HALF, CHUNK)],
        out_shape=[
            jax.ShapeDtypeStruct((N_GROUPS, STATE_DIM), _F32),
            jax.ShapeDtypeStruct((N_GROUPS, STATE_DIM), _F32),
            jax.ShapeDtypeStruct((N_CHUNKS, CHUNK, 2 * STATES_PER_HALF), _BF16),
            jax.ShapeDtypeStruct((N_CHUNKS, 2 * STATES_PER_HALF, CHUNK), _BF16),
        ],
        compiler_params=_compiler_params(1),
        name="ssm_prep",
    )(a_re, a_im, log_dt.reshape(N_GROUPS, 1), bt_re, bt_im,
      c_re.reshape(rows, STATE_DIM), c_im.reshape(rows, STATE_DIM))


def _lam_rows(lre_ref, lim_ref, rows_per_half):
    first_half = (lax.broadcasted_iota(jnp.int32, (2 * rows_per_half, STATES_PER_HALF), 0)
                  < rows_per_half)
    return (jnp.where(first_half, lre_ref[0:1, :], lre_ref[1:2, :]),
            jnp.where(first_half, lim_ref[0:1, :], lim_ref[1:2, :]))


def _readout(x, cd_ref):
    return _dot(x.astype(_BF16), cd_ref[...])


def _ssm_scan_kernel(x_ref, xs_ref, g_ref, wu_ref, wz_ref, bd_ref,
                     cd_a_ref, lre_a_ref, lim_a_ref, d_a_ref,
                     cd_b_ref, lre_b_ref, lim_b_ref, d_b_ref,
                     s0re_ref, s0im_ref,
                     y_ref, zs_ref, nre_ref, nim_ref, ys_ref, zss_ref, nres_ref, nims_ref,
                     h_scr, hs_scr, x0, u0, z0, x1, u1, z1, carry_re, carry_im,
                     *, rows_per_step, n_blocks):
    s = pl.program_id(0)
    n_steps = n_blocks * N_CHUNKS
    step_a = jnp.minimum(s, n_steps - 1)
    step_b = jnp.maximum(s - 1, 0)
    i_a, j_a = _divmod_pow2(step_a, N_CHUNKS)
    i_b, j_b = _divmod_pow2(step_b, N_CHUNKS)
    tm = x_ref.shape[0]
    r = rows_per_step
    rs = xs_ref.shape[0]
    sh = STATES_PER_HALF
    assert 2 * r == SUBLANES

    @pl.when(s == 0)
    def _():
        for ref in (x1, u1, z1):
            ref[...] = jnp.zeros(ref.shape, _F32)

    @pl.when(j_a == 0)
    def _():
        h_scr[...] = _rmsnorm(x_ref[...], g_ref[...]).astype(_BF16)

    @pl.when(i_b == 0)
    def _():
        zeros = jnp.zeros((2 * r, sh), _F32)
        carry_re[j_b] = zeros
        carry_im[j_b] = zeros

    def stages(x_a, u_a, z_a, x_b, u_b, z_b):
        n = tm // SCAN_PIECES
        l_re, l_im = _lam_rows(lre_b_ref, lim_b_ref, r)

        def recurrence(c, lo):
            c_re, c_im = c
            for k in range(lo // SUBLANES, (lo + n) // SUBLANES):
                for rows in (pl.ds(k * SUBLANES, SUBLANES), pl.ds(tm + k * SUBLANES, SUBLANES)):
                    c_re, c_im = (l_re * c_re - l_im * c_im + x_b[rows, 0:sh],
                                  l_re * c_im + l_im * c_re + x_b[rows, sh:2 * sh])
                    x_b[rows, 0:sh] = c_re
                    x_b[rows, sh:2 * sh] = c_im
            return c_re, c_im

        def readout(lo):
            states = jnp.concatenate([x_b[lo:lo + n, :], x_b[tm + lo:tm + lo + n, :]], axis=0)
            yy = _readout(states, cd_b_ref)
            low_rows = (lax.broadcasted_iota(jnp.int32, (n, LANES), 0) & r) == 0
            y_even = yy[:n]
            y_odd = yy[n:]
            y_first = jnp.where(low_rows, y_even[:, :LANES], pltpu.roll(y_odd[:, :LANES], r, 0))
            y_second = jnp.where(low_rows, pltpu.roll(y_even[:, LANES:], n - r, 0),
                                 y_odd[:, LANES:])
            y = jnp.concatenate([y_first, y_second], axis=1)
            rows = slice(lo, lo + n)
            y_ref[rows, :] = jax.nn.gelu(y + d_b_ref[...] * u_b[rows, :]).astype(_BF16)
            zs_ref[rows, :] = _silu(z_b[rows, :]).astype(_BF16)

        h = h_scr[...]
        u = _dot(h, wu_ref[...].astype(_BF16))
        u_a[...] = u
        carry = (carry_re[j_b], carry_im[j_b])
        for piece in range(SCAN_PIECES):
            if piece == SCAN_PIECES // 2:
                z_a[...] = _dot(h, wz_ref[...].astype(_BF16))
            carry = recurrence(carry, piece * n)
            readout(piece * n)
        carry_re[j_b], carry_im[j_b] = carry

        first_lanes = lax.broadcasted_iota(jnp.int32, (tm, CHUNK), 1) < LANES
        low_rows = (lax.broadcasted_iota(jnp.int32, (tm, CHUNK), 0) & r) == 0
        u_first = jnp.where(first_lanes, u, 0.0)
        u_second = jnp.where(first_lanes, 0.0, u)
        u_down = pltpu.roll(u, r, 0)
        u_up = pltpu.roll(u, tm - r, 0)
        even = jnp.where(low_rows, u_first, jnp.where(first_lanes, 0.0, u_down))
        odd = jnp.where(low_rows, jnp.where(first_lanes, u_up, 0.0), u_second)
        x_a[...] = _dot(jnp.concatenate([even, odd], axis=0).astype(_BF16), bd_ref[...])

    @pl.when(s & 1 == 0)
    def _():
        stages(x0, u0, z0, x1, u1, z1)

    @pl.when(s & 1 == 1)
    def _():
        stages(x1, u1, z1, x0, u0, z0)

    @pl.when((i_b == n_blocks - 1) & (s > 0))
    def _():
        c_re = carry_re[j_b]
        c_im = carry_im[j_b]
        nre_ref[:, 0:sh] = c_re[0:r]
        nre_ref[:, sh:2 * sh] = c_re[r:2 * r]
        nim_ref[:, 0:sh] = c_im[0:r]
        nim_ref[:, sh:2 * sh] = c_im[r:2 * r]

    @pl.when((i_a == n_blocks - 1) & (s < n_steps))
    def _():
        @pl.when(j_a == 0)
        def _():
            hs_scr[...] = _rmsnorm(xs_ref[...], g_ref[...]).astype(_BF16)

        hs = hs_scr[...]
        us = _dot(hs, wu_ref[...].astype(_BF16))
        zz = _dot(hs, wz_ref[...].astype(_BF16))
        first = lax.broadcasted_iota(jnp.int32, (rs, CHUNK), 1) < LANES
        lhs_s = jnp.concatenate([jnp.where(first, us, 0.0), jnp.where(first, 0.0, us)], axis=0)
        bus = _dot(lhs_s.astype(_BF16), bd_ref[...])
        p_re = jnp.concatenate([s0re_ref[:, 0:sh], s0re_ref[:, sh:2 * sh]], axis=0)
        p_im = jnp.concatenate([s0im_ref[:, 0:sh], s0im_ref[:, sh:2 * sh]], axis=0)
        ls_re, ls_im = _lam_rows(lre_a_ref, lim_a_ref, rs)
        n_re = ls_re * p_re - ls_im * p_im + bus[:, :sh]
        n_im = ls_re * p_im + ls_im * p_re + bus[:, sh:]
        nres_ref[:, 0:sh] = n_re[0:rs]
        nres_ref[:, sh:2 * sh] = n_re[rs:2 * rs]
        nims_ref[:, 0:sh] = n_im[0:rs]
        nims_ref[:, sh:2 * sh] = n_im[rs:2 * rs]
        yys = _readout(jnp.concatenate([n_re, n_im], axis=1), cd_a_ref)
        y_s = jnp.concatenate([yys[:rs, :LANES], yys[rs:, LANES:]], axis=1)
        ys_ref[...] = jax.nn.gelu(y_s + d_a_ref[...] * us).astype(_BF16)
        zss_ref[...] = _silu(zz).astype(_BF16)


def _ssm_scan(x, xs, norm_g, w_in, bd, cd, lam_re, lam_im, d_skip, s0_re, s0_im, layer,
              *, rows_per_step):
    m = x.shape[0]
    rs = xs.shape[0]
    r = rows_per_step
    tm = TM_SCAN
    n_blocks = m // tm
    n_steps = n_blocks * N_CHUNKS
    sh = STATES_PER_HALF

    def stage_a(s):
        return _divmod_pow2(jnp.minimum(s, n_steps - 1), N_CHUNKS)

    def stage_b(s):
        return _divmod_pow2(jnp.maximum(s - 1, 0), N_CHUNKS)

    def per_chunk(stage, a, b):
        return pl.BlockSpec((None, a, b), lambda s: (stage(s)[1], 0, 0))

    def skip_spec(stage):
        return pl.BlockSpec((None, 1, CHUNK), lambda s: (layer, 0, stage(s)[1]))

    def live_last(stage, rows, cols):
        def index_map(s):
            i, j = stage(s)
            return 0, jnp.where(i == n_blocks - 1, j, 0)
        return pl.BlockSpec((rows, cols), index_map)

    act_spec = pl.BlockSpec((tm, CHUNK), lambda s: stage_b(s))
    state_rows = pltpu.VMEM((2 * tm, 2 * sh), _F32)
    chunk_rows = pltpu.VMEM((tm, CHUNK), _F32)
    return pl.pallas_call(
        functools.partial(_ssm_scan_kernel, rows_per_step=r, n_blocks=n_blocks),
        grid=(n_steps + 1,),
        in_specs=[
            pl.BlockSpec((tm, D_MODEL), lambda s: (stage_a(s)[0], 0),
                         pipeline_mode=pl.Buffered(1)),
            pl.BlockSpec((rs, D_MODEL), lambda s: (0, 0)),
            pl.BlockSpec((None, 1, D_MODEL), lambda s: (layer, 0, 0)),
            pl.BlockSpec((None, D_MODEL, CHUNK), lambda s: (layer, 0, stage_a(s)[1])),
            pl.BlockSpec((None, D_MODEL, CHUNK), lambda s: (layer, 0, N_CHUNKS + stage_a(s)[1])),
            per_chunk(stage_a, CHUNK, 2 * sh),
            per_chunk(stage_a, 2 * sh, CHUNK),
            per_chunk(stage_a, 2, sh), per_chunk(stage_a, 2, sh), skip_spec(stage_a),
            per_chunk(stage_b, 2 * sh, CHUNK),
            per_chunk(stage_b, 2, sh), per_chunk(stage_b, 2, sh), skip_spec(stage_b),
            live_last(stage_a, rs, STATES_PER_CHUNK), live_last(stage_a, rs, STATES_PER_CHUNK),
        ],
        out_specs=[
            act_spec, act_spec,
            live_last(stage_b, r, STATES_PER_CHUNK), live_last(stage_b, r, STATES_PER_CHUNK),
            live_last(stage_a, rs, CHUNK), live_last(stage_a, rs, CHUNK),
            live_last(stage_a, rs, STATES_PER_CHUNK), live_last(stage_a, rs, STATES_PER_CHUNK),
        ],
        out_shape=[
            jax.ShapeDtypeStruct((m, D_INNER), _BF16),
            jax.ShapeDtypeStruct((m, D_INNER), _BF16),
            jax.ShapeDtypeStruct((r, N_STATES), _F32),
            jax.ShapeDtypeStruct((r, N_STATES), _F32),
            jax.ShapeDtypeStruct((rs, D_INNER), _BF16),
            jax.ShapeDtypeStruct((rs, D_INNER), _BF16),
            jax.ShapeDtypeStruct((rs, N_STATES), _F32),
            jax.ShapeDtypeStruct((rs, N_STATES), _F32),
        ],
        scratch_shapes=[
            pltpu.VMEM((tm, D_MODEL), _BF16),
            pltpu.VMEM((rs, D_MODEL), _BF16),
            state_rows, chunk_rows, chunk_rows,
            state_rows, chunk_rows, chunk_rows,
            pltpu.VMEM((N_CHUNKS, 2 * r, sh), _F32),
            pltpu.VMEM((N_CHUNKS, 2 * r, sh), _F32),
        ],
        compiler_params=_compiler_params(1),
        name="ssm_scan",
    )(x, xs, norm_g, w_in, w_in, bd, cd, lam_re, lam_im, d_skip,
      cd, lam_re, lam_im, d_skip, s0_re, s0_im)


def _glu_kernel(y_ref, yj_ref, zs_ref, ys_ref, yjs_ref, zss_ref, wg_ref, bg_ref, o_ref, os_ref,
                w16, *, plan):
    active, i, cur = plan.stage(pl.program_id(0), wg_ref, w16)

    def gated(y_all, y_chunk, z_chunk):
        gate = jax.nn.sigmoid(_dot(y_all, w16[cur]) + bg_ref[...])
        return (y_chunk.astype(_F32) * gate * z_chunk.astype(_F32)).astype(_BF16)

    @pl.when(active)
    def _():
        o_ref[...] = gated(y_ref[...], yj_ref[...], zs_ref[...])

        @pl.when(i == plan.n_rows - 1)
        def _():
            os_ref[...] = gated(ys_ref[...], yjs_ref[...], zss_ref[...])


def _glu(y, zs, ys, zss, w_glu, b_glu, layer):
    m = y.shape[0]
    rs = ys.shape[0]
    tm = TM_GLU
    plan = _StationaryWeights(D_INNER, D_INNER // WIDE, m // tm, layer)
    rows = plan.rows_spec(tm)
    sample_rows = plan.fixed_rows_spec(rs)
    return pl.pallas_call(
        functools.partial(_glu_kernel, plan=plan),
        grid=(plan.n_steps,),
        in_specs=[
            plan.rows_spec(tm, width_is_wide=False),
            rows, rows,
            pl.BlockSpec((rs, D_INNER), lambda s: (0, 0)),
            sample_rows, sample_rows,
            plan.weight_spec(),
            pl.BlockSpec((None, 1, WIDE), lambda s: (layer, 0, plan.col_row(s)[0])),
        ],
        out_specs=[rows, sample_rows],
        out_shape=[
            jax.ShapeDtypeStruct((m, D_INNER), _BF16),
            jax.ShapeDtypeStruct((rs, D_INNER), _BF16),
        ],
        scratch_shapes=[plan.scratch()],
        compiler_params=pltpu.CompilerParams(
            dimension_semantics=("arbitrary",), vmem_limit_bytes=VMEM_LIMIT_GLU_BYTES),
        name="glu",
    )(y, y, zs, ys, ys, zss, w_glu, b_glu)


def _final_norm_batch_major_kernel(x_ref, g_ref, o_ref, slab):
    nb, tt, _ = o_ref.shape
    xn = _rmsnorm(x_ref[...], g_ref[...])
    for k in range(D_MODEL // LANES):
        lanes = slice(k * LANES, (k + 1) * LANES)
        slab[k] = xn[:, lanes]
        for b in range(nb):
            o_ref[b, :, lanes] = slab[k, pl.ds(b, tt, stride=nb), :]


def _final_norm_batch_major(x, g, nb):
    seq = x.shape[0] // nb
    tt = TT_REORDER
    return pl.pallas_call(
        _final_norm_batch_major_kernel,
        grid=(seq // tt,),
        in_specs=[pl.BlockSpec((nb * tt, D_MODEL), lambda i: (i, 0)),
                  pl.BlockSpec((1, D_MODEL), lambda i: (0, 0))],
        out_specs=pl.BlockSpec((nb, tt, D_MODEL), lambda i: (0, i, 0)),
        out_shape=jax.ShapeDtypeStruct((nb, seq, D_MODEL), _F32),
        scratch_shapes=[pltpu.VMEM((D_MODEL // LANES, nb * tt, LANES), _F32)],
        compiler_params=_compiler_params(1),
        name="final_norm_batch_major",
    )(x, g)


def _final_norm_kernel(x_ref, g_ref, o_ref):
    o_ref[...] = _rmsnorm(x_ref[...], g_ref[...])


def _final_norm(x, g):
    m = x.shape[0]
    tm = min(m, TM_CONV)
    row_spec = pl.BlockSpec((tm, D_MODEL), lambda i: (i, 0))
    return pl.pallas_call(
        _final_norm_kernel,
        grid=(m // tm,),
        in_specs=[row_spec, pl.BlockSpec((1, D_MODEL), lambda i: (0, 0))],
        out_specs=row_spec,
        out_shape=jax.ShapeDtypeStruct((m, D_MODEL), _F32),
        compiler_params=_compiler_params(1),
        name="final_norm",
    )(x, g)


def kernel(x_prompt, x_sample, state_conv, state_ssm_re, state_ssm_im,
           conv_norm, conv_w_in, conv_w, conv_w_out,
           ssm_norm, ssm_w_in, ssm_a_re, ssm_a_im, ssm_log_dt, ssm_b_re, ssm_b_im,
           ssm_c_re, ssm_c_im, ssm_d, ssm_w_glu, ssm_b_glu, ssm_w_out, final_norm):
    batch, seq, _ = x_prompt.shape
    dec_batch = x_sample.shape[0]
    n_conv = conv_w_in.shape[0]
    n_ssm = ssm_w_in.shape[0]
    depth = n_conv + n_ssm

    xp = x_prompt
    xs = x_sample.reshape(dec_batch, D_MODEL)
    conv_norm3 = conv_norm.reshape(n_conv, 1, D_MODEL)
    ssm_norm3 = ssm_norm.reshape(n_ssm, 1, D_MODEL)
    ssm_d3 = ssm_d.reshape(n_ssm, 1, D_INNER)
    ssm_b_glu3 = ssm_b_glu.reshape(n_ssm, 1, D_INNER)

    conv_p, conv_s, re_p, im_p, re_s, im_s = [], [], [], [], [], []
    for layer in range(depth):
        l = layer // 2
        if layer % 2 == 0:
            y, ys, ns_p, ns_s = _conv_in(
                xp, xs, conv_norm3, conv_w_in, conv_w,
                state_conv[l].reshape(dec_batch, (CONV_WIDTH - 1) * D_INNER), l,
                rows_per_step=batch)
            xp, xs = _out_proj(y, ys, conv_w_out, xp, xs, l)
            conv_p.append(ns_p)
            conv_s.append(ns_s)
        else:
            lam_re, lam_im, bd, cd = _ssm_prep(
                ssm_a_re[l], ssm_a_im[l], ssm_log_dt[l], ssm_b_re[l], ssm_b_im[l],
                ssm_c_re[l], ssm_c_im[l])
            lam_re = lam_re.reshape(N_CHUNKS, 2, STATES_PER_HALF)
            lam_im = lam_im.reshape(N_CHUNKS, 2, STATES_PER_HALF)
            y, zs, hr_p, hi_p, ys, zss, hr_s, hi_s = _ssm_scan(
                xp, xs, ssm_norm3, ssm_w_in, bd, cd, lam_re, lam_im, ssm_d3,
                state_ssm_re[l].reshape(dec_batch, N_STATES),
                state_ssm_im[l].reshape(dec_batch, N_STATES),
                l, rows_per_step=batch)
            yy, yys = _glu(y, zs, ys, zss, ssm_w_glu, ssm_b_glu3, l)
            xp, xs = _out_proj(yy, yys, ssm_w_out, xp, xs, l)
            re_p.append(hr_p.reshape(batch, N_GROUPS, STATE_DIM))
            im_p.append(hi_p.reshape(batch, N_GROUPS, STATE_DIM))
            re_s.append(hr_s.reshape(dec_batch, N_GROUPS, STATE_DIM))
            im_s.append(hi_s.reshape(dec_batch, N_GROUPS, STATE_DIM))

    final_g = final_norm.reshape(1, D_MODEL)
    y_prompt = _final_norm_batch_major(xp, final_g, batch)
    xs = _final_norm(xs, final_g)
    y_sample = xs.reshape(dec_batch, 1, D_MODEL)
    return (y_prompt, y_sample, jnp.stack(conv_p), jnp.stack(conv_s),
            jnp.stack(re_p), jnp.stack(im_p), jnp.stack(re_s), jnp.stack(im_s))
```
